```python
import math
import jax, jax.numpy as jnp
from jax import lax
import numpy as np

D_MODEL = 1024
BATCH = 8
SEQ = 4096
DEPTH = 1

MIX_WIDTH = D_MODEL
GLA_HEADS = 4
GLA_DV = MIX_WIDTH // 2 // GLA_HEADS
GLA_DK = GLA_DV // 2
GLA_GATE_RANK = 16
GLA_GATE_NORM = 16.0
GLA_CHUNK = 64
SWA_HEAD_DIM = 64
SWA_Q_HEADS = MIX_WIDTH // 2 // SWA_HEAD_DIM
SWA_KV_HEADS = 2
SWA_BLOCK = 128
SWA_WINDOW = 128
REL_BUCKETS = 32
REL_MAX_DIST = 128
D_FF = 4 * D_MODEL
NORM_EPS = 1e-6

COL_SIZES = (
    GLA_HEADS * GLA_DK,
    GLA_HEADS * GLA_DK,
    GLA_HEADS * GLA_DV,
    GLA_HEADS * GLA_DV,
    2 * GLA_GATE_RANK,
    SWA_Q_HEADS * SWA_HEAD_DIM,
    SWA_KV_HEADS * SWA_HEAD_DIM,
    SWA_KV_HEADS * SWA_HEAD_DIM,
)
IN_COLS = int(sum(COL_SIZES))
SPLITS = [int(s) for s in np.cumsum(COL_SIZES)[:-1]]

kernel_name = "hybrid_gla_swa_bidir_encoder_layer"


def rmsnorm(x, g):
    xf = x.astype(jnp.float32)
    r = lax.rsqrt(jnp.mean(xf * xf, axis=-1, keepdims=True) + NORM_EPS)
    return (xf * r).astype(x.dtype) * g


def t5_buckets(rel):
    nb = REL_BUCKETS // 2
    ret = (rel > 0).astype(np.int32) * nb
    n = np.abs(rel)
    max_exact = nb // 2
    large = max_exact + (np.log(np.maximum(n, 1).astype(np.float32) / max_exact)
                         / math.log(REL_MAX_DIST / max_exact) * (nb - max_exact)).astype(np.int32)
    large = np.minimum(large, nb - 1)
    return ret + np.where(n < max_exact, n, large)


def gla_chunked(q, k, v, log_a):
    B, H, L, dk = q.shape
    dv = v.shape[-1]
    C = GLA_CHUNK
    N = L // C
    q = q.reshape(B, H, N, C, dk)
    k = k.reshape(B, H, N, C, dk)
    v = v.reshape(B, H, N, C, dv)
    b = jnp.cumsum(log_a.reshape(B, H, N, C, dk), axis=3)
    b_last = b[:, :, :, -1:, :]
    q_dec = q * jnp.exp(b)
    k_intra = k * jnp.exp(-b)
    k_state = k * jnp.exp(b_last - b)
    causal = jnp.asarray(np.tril(np.ones((C, C), dtype=bool)))
    A = jnp.where(causal, jnp.einsum('bhncd,bhnsd->bhncs', q_dec, k_intra), 0.0)
    o_intra = jnp.einsum('bhncs,bhnse->bhnce', A, v)
    dS = jnp.einsum('bhncd,bhnce->bhnde', k_state, v)
    decay = jnp.exp(b_last[:, :, :, 0, :])

    def step(S, inp):
        d, ds = inp
        return d[..., None] * S + ds, S

    S0 = jnp.zeros((B, H, dk, dv), jnp.float32)
    _, S_enter = lax.scan(step, S0, (jnp.moveaxis(decay, 2, 0), jnp.moveaxis(dS, 2, 0)))
    S_enter = jnp.moveaxis(S_enter, 0, 2)
    o_inter = jnp.einsum('bhncd,bhnde->bhnce', q_dec, S_enter)
    return (o_intra + o_inter).reshape(B, H, L, dv)


def banded_window_gqa(q, k, v, sink, rel_table):
    B, Hq, L, dh = q.shape
    Hkv = k.shape[1]
    G = Hq // Hkv
    W = SWA_BLOCK
    N = L // W
    qb = q.reshape(B, Hkv, G, N, W, dh)

    def band(t):
        tp = jnp.pad(t, ((0, 0), (0, 0), (W, W), (0, 0))).reshape(B, Hkv, N + 2, W, dh)
        return jnp.concatenate([tp[:, :, :N], tp[:, :, 1:N + 1], tp[:, :, 2:N + 2]], axis=3)

    kb, vb = band(k), band(v)
    c = np.arange(W)[:, None]
    s = np.arange(3 * W)[None, :]
    rel = s - W - c
    key_pos = (np.arange(N)[:, None, None] - 1) * W + s[None]
    mask = jnp.asarray((np.abs(rel)[None] <= SWA_WINDOW) & (key_pos >= 0) & (key_pos < L))
    bias = rel_table.astype(jnp.float32)[jnp.asarray(t5_buckets(rel))]
    bias = jnp.transpose(bias, (2, 0, 1)).reshape(Hkv, G, 1, W, 3 * W)

    logits = jnp.einsum('bkgncd,bknsd->bkgncs', qb, kb).astype(jnp.float32) * (dh ** -0.5) + bias
    logits = jnp.where(mask, logits, -1e30)
    sink_l = sink.astype(jnp.float32).reshape(Hkv, G, 1, 1, 1)
    m = jnp.maximum(jnp.max(logits, axis=-1, keepdims=True), sink_l)
    p = jnp.exp(logits - m)
    denom = jnp.sum(p, axis=-1, keepdims=True) + jnp.exp(sink_l - m)
    o = jnp.einsum('bkgncs,bknsd->bkgncd', (p / denom).astype(v.dtype), vb)
    return o.reshape(B, Hq, L, dh)


def hybrid_mixer(u, w_in, w_gu_f, b_g_f, w_gu_b, b_g_b, gla_norm, sink, rel_table, w_out):
    B, L, _ = u.shape
    proj = u @ w_in
    qa, ka, va, ga, za, qs, ks, vs = jnp.split(proj, SPLITS, axis=-1)

    def heads(t, h):
        return t.reshape(B, L, h, -1).transpose(0, 2, 1, 3)

    f32 = jnp.float32
    qh = heads(qa, GLA_HEADS).astype(f32) * (GLA_DK ** -0.5)
    kh = heads(ka, GLA_HEADS).astype(f32)
    vh = heads(va, GLA_HEADS).astype(f32)
    zf, zb = za[..., :GLA_GATE_RANK], za[..., GLA_GATE_RANK:]
    la_f = heads(jax.nn.log_sigmoid((zf @ w_gu_f + b_g_f).astype(f32)) / GLA_GATE_NORM, GLA_HEADS)
    la_b = heads(jax.nn.log_sigmoid((zb @ w_gu_b + b_g_b).astype(f32)) / GLA_GATE_NORM, GLA_HEADS)
    o_f = gla_chunked(qh, kh, vh, la_f)
    flip = lambda t: jnp.flip(t, axis=2)
    o_b = flip(gla_chunked(flip(qh), flip(kh), flip(vh), flip(la_b)))
    o_a = o_f + o_b
    o_a = o_a * lax.rsqrt(jnp.mean(o_a * o_a, axis=-1, keepdims=True) + NORM_EPS)
    o_a = o_a.transpose(0, 2, 1, 3) * gla_norm.astype(f32)
    o_a = (o_a.reshape(B, L, GLA_HEADS * GLA_DV) * jax.nn.silu(ga.astype(f32))).astype(u.dtype)

    o_s = banded_window_gqa(heads(qs, SWA_Q_HEADS), heads(ks, SWA_KV_HEADS), heads(vs, SWA_KV_HEADS),
                            sink, rel_table)
    o_s = o_s.transpose(0, 2, 1, 3).reshape(B, L, SWA_Q_HEADS * SWA_HEAD_DIM)

    return jnp.concatenate([o_a, o_s], axis=-1) @ w_out


def setup_inputs(seed: int = 0) -> dict:
    key = jax.random.key(seed)
    ks = jax.random.split(key, 20)
    nrm = lambda k, shape, scale: jax.random.normal(k, shape, jnp.float32) * scale
    gain = lambda k, shape: 1.0 + nrm(k, shape, 0.02)
    gla_w = GLA_HEADS * GLA_DK
    return {
        "x": nrm(ks[0], (BATCH, SEQ, D_MODEL), 1.0),
        "norm_mix_pre": gain(ks[1], (DEPTH, D_MODEL)),
        "w_in": nrm(ks[2], (DEPTH, D_MODEL, IN_COLS), D_MODEL ** -0.5),
        "w_gate_up_fwd": nrm(ks[3], (DEPTH, GLA_GATE_RANK, gla_w), GLA_GATE_RANK ** -0.5),
        "b_gate_fwd": nrm(ks[4], (DEPTH, gla_w), 0.1),
        "w_gate_up_bwd": nrm(ks[5], (DEPTH, GLA_GATE_RANK, gla_w), GLA_GATE_RANK ** -0.5),
        "b_gate_bwd": nrm(ks[6], (DEPTH, gla_w), 0.1),
        "gla_norm": gain(ks[7], (DEPTH, GLA_DV)),
        "swa_sink": nrm(ks[8], (DEPTH, SWA_Q_HEADS), 0.5),
        "rel_bias": nrm(ks[9], (REL_BUCKETS, SWA_Q_HEADS), 0.5),
        "w_out": nrm(ks[10], (DEPTH, MIX_WIDTH, D_MODEL), MIX_WIDTH ** -0.5),
        "norm_mix_post": gain(ks[11], (DEPTH, D_MODEL)),
        "norm_mlp_pre": gain(ks[12], (DEPTH, D_MODEL)),
        "w_up": nrm(ks[13], (DEPTH, D_MODEL, D_FF), D_MODEL ** -0.5),
        "w_down": nrm(ks[14], (DEPTH, D_FF, D_MODEL), D_FF ** -0.5),
        "norm_mlp_post": gain(ks[15], (DEPTH, D_MODEL)),
    }


def reference(x, norm_mix_pre, w_in, w_gate_up_fwd, b_gate_fwd, w_gate_up_bwd, b_gate_bwd,
              gla_norm, swa_sink, rel_bias, w_out, norm_mix_post, norm_mlp_pre, w_up, w_down,
              norm_mlp_post):
    h = x
    for l in range(DEPTH):
        u = rmsnorm(h, norm_mix_pre[l])
        mix = hybrid_mixer(u, w_in[l], w_gate_up_fwd[l], b_gate_fwd[l], w_gate_up_bwd[l], b_gate_bwd[l],
                           gla_norm[l], swa_sink[l], rel_bias, w_out[l])
        h = h + rmsnorm(mix, norm_mix_post[l])
        z = rmsnorm(h, norm_mlp_pre[l]) @ w_up[l]
        ff = jnp.square(jax.nn.relu(z)) @ w_down[l]
        h = h + rmsnorm(ff, norm_mlp_post[l])
    return h
```

```python
import functools
import math

import jax
import jax.numpy as jnp
import numpy as np
from jax import lax
from jax.experimental import pallas as pl
from jax.experimental.pallas import tpu as pltpu

F32 = jnp.float32
BF16 = jnp.bfloat16

D_MODEL = 1024
GLA_HEADS = 4
GLA_DK = 64
GLA_DV = 128
GLA_RANK = 16
GLA_GATE_NORM = 16.0
GLA_CHUNK = 64
SWA_DH = 64
SWA_Q_HEADS = 8
SWA_KV_HEADS = 2
SWA_BLOCK = 128
SWA_WINDOW = 128
REL_BUCKETS = 32
REL_MAX_DIST = 128
D_FF = 4 * D_MODEL
NORM_EPS = 1e-6
MASK_VALUE = -1e30

LANES = 128
HALF = LANES // 2

QK_GLA_W = GLA_HEADS * 2 * GLA_DK
V_GLA_W = GLA_HEADS * GLA_DV
G_GLA_W = GLA_HEADS * GLA_DV
Q_SWA_W = SWA_Q_HEADS * SWA_DH
KV_SWA_W = SWA_KV_HEADS * 2 * SWA_DH
Z_W = 2 * GLA_RANK
IN_COLS = QK_GLA_W + V_GLA_W + G_GLA_W + Q_SWA_W + KV_SWA_W + Z_W

TOKEN_BLOCK = 512
FF_BLOCK = 1024
VMEM_LIMIT = 56 * 1024 * 1024


def _rms(x):
    return x * lax.rsqrt(jnp.mean(x * x, axis=-1, keepdims=True) + NORM_EPS)


def _dot(a, b):
    return jnp.dot(a, b, preferred_element_type=F32)


def _dot_nt(a, b):
    return lax.dot_general(a, b, (((1,), (1,)), ((), ())), preferred_element_type=F32)


def _dot_tn(a, b):
    return lax.dot_general(a, b, (((0,), (0,)), ((), ())), preferred_element_type=F32)


def _const_spec(shape):
    nd = len(shape)
    return pl.BlockSpec(shape, lambda *_: (0,) * nd, pipeline_mode=pl.Buffered(1))


def _inproj_kernel(x_ref, g_ref, w_ref, qk_ref, v_ref, gg_ref, qs_ref, kv_ref, z_ref):
    u = (_rms(x_ref[...]) * g_ref[...]).astype(BF16)
    off = 0
    for ref, width in ((qk_ref, QK_GLA_W), (v_ref, V_GLA_W), (gg_ref, G_GLA_W),
                       (qs_ref, Q_SWA_W), (kv_ref, KV_SWA_W), (z_ref, Z_W)):
        ref[...] = _dot(u, w_ref[:, off:off + width]).astype(ref.dtype)
        off += width


def _inproj(x2, g, w):
    t = x2.shape[0]
    row = lambda width: pl.BlockSpec((TOKEN_BLOCK, width), lambda i: (i, 0))
    widths = (QK_GLA_W, V_GLA_W, G_GLA_W, Q_SWA_W, KV_SWA_W, Z_W)
    dtypes = (BF16, BF16, BF16, BF16, BF16, F32)
    return pl.pallas_call(
        _inproj_kernel,
        grid=(t // TOKEN_BLOCK,),
        in_specs=[row(D_MODEL), _const_spec((1, D_MODEL)), _const_spec((D_MODEL, IN_COLS))],
        out_specs=[row(wd) for wd in widths],
        out_shape=[jax.ShapeDtypeStruct((t, wd), dt) for wd, dt in zip(widths, dtypes)],
        compiler_params=pltpu.CompilerParams(
            dimension_semantics=("arbitrary",), vmem_limit_bytes=VMEM_LIMIT),
        name="inproj",
    )(x2, g, w)


def _gla_kernel(qk_ref, v_ref, g_ref, z_ref, wgu_ref, bg_ref, nrm_ref, o_ref,
                qdec_s, am_s, ds_s, dec_s, sc_s, *, seq):
    c = GLA_CHUNK
    nc = seq // c
    lane_c = lax.broadcasted_iota(jnp.int32, (c, LANES), 1)
    row_c = lax.broadcasted_iota(jnp.int32, (c, LANES), 0)
    fwd_c = lane_c < HALF
    tri = (fwd_c & (lane_c <= row_c)) | ((lane_c >= HALF) & ((lane_c - HALF) >= row_c))
    tri_bf = tri.astype(F32).astype(BF16)
    fwd_sq = lax.broadcasted_iota(jnp.int32, (LANES, LANES), 1) < HALF
    fwd_row = lax.broadcasted_iota(jnp.int32, (8, LANES), 1) < HALF
    wgu = wgu_ref[0]
    bg = bg_ref[0]
    zero_c = jnp.zeros((c, LANES), F32)

    def block_diag(t):
        return jnp.concatenate([jnp.where(fwd_c, t, zero_c), jnp.where(fwd_c, zero_c, t)], axis=0)

    def chunk_pass(n, carry):
        rows = pl.ds(pl.multiple_of(n * c, c), c)
        pre = _dot(z_ref[rows, :].astype(BF16), wgu) + bg
        la = (jnp.minimum(pre, 0.0) - jnp.log1p(jnp.exp(-jnp.abs(pre)))) * (1.0 / GLA_GATE_NORM)
        la_hi = la.astype(BF16)
        la_lo = (la - la_hi.astype(F32)).astype(BF16)
        b = (_dot(tri_bf, block_diag(la_hi.astype(F32)).astype(BF16))
             + _dot(tri_bf, block_diag(la_lo.astype(F32)).astype(BF16)))
        tot = jnp.sum(la, axis=0, keepdims=True)
        x = qk_ref[rows, :].astype(F32)
        xr = pltpu.roll(x, HALF, axis=1)
        qq = jnp.where(fwd_c, x, xr)
        kk = jnp.where(fwd_c, xr, x)
        qdec = (qq * jnp.exp(b)).astype(BF16)
        kin = kk * jnp.exp(-b)
        kst = (kk * jnp.exp(tot - b)).astype(BF16)
        a2 = _dot_nt(qdec, block_diag(kin).astype(BF16))
        qdec_s[rows, :] = qdec
        am_s[rows, :] = jnp.where(tri, a2, zero_c).astype(BF16)
        ds_s[n] = _dot_tn(v_ref[rows, :], kst)
        dec_s[n] = jnp.broadcast_to(jnp.exp(tot), (8, LANES))
        return carry

    lax.fori_loop(0, nc, chunk_pass, 0)

    def scan_step(i, s):
        j = nc - 1 - i
        sc_s[i] = s
        dec = jnp.where(fwd_row, dec_s[i], dec_s[j])[0:1]
        return dec * s + jnp.where(fwd_sq, ds_s[i], ds_s[j])

    lax.fori_loop(0, nc, scan_step, jnp.zeros((LANES, LANES), F32))

    nrm = nrm_ref[...]

    def out_pass(n, carry):
        rows = pl.ds(pl.multiple_of(n * c, c), c)
        st = jnp.where(fwd_sq, sc_s[n], sc_s[nc - 1 - n]).astype(BF16)
        vc = v_ref[rows, :]
        o = _dot_nt(qdec_s[rows, :], st) + _dot(am_s[rows, :], jnp.concatenate([vc, vc], axis=0))
        gate = g_ref[rows, :].astype(F32)
        o = (_rms(o) * nrm) * (gate / (1.0 + jnp.exp(-gate)))
        o_ref[rows, :] = o.astype(o_ref.dtype)
        return carry

    lax.fori_loop(0, nc, out_pass, 0)


def _gla(qk, v, g, z, wgu, bg, nrm, batch, seq):
    t = batch * seq
    nc = seq // GLA_CHUNK
    tile = pl.BlockSpec((seq, LANES), lambda b, h: (b, h))
    return pl.pallas_call(
        functools.partial(_gla_kernel, seq=seq),
        grid=(batch, GLA_HEADS),
        in_specs=[tile, tile, tile,
                  pl.BlockSpec((seq, Z_W), lambda b, h: (b, 0)),
                  pl.BlockSpec((1, Z_W, LANES), lambda b, h: (h, 0, 0)),
                  pl.BlockSpec((1, 1, LANES), lambda b, h: (h, 0, 0)),
                  pl.BlockSpec((1, LANES), lambda b, h: (0, 0))],
        out_specs=tile,
        out_shape=jax.ShapeDtypeStruct((t, V_GLA_W), BF16),
        scratch_shapes=[pltpu.VMEM((seq, LANES), BF16),
                        pltpu.VMEM((seq, LANES), BF16),
                        pltpu.VMEM((nc, LANES, LANES), F32),
                        pltpu.VMEM((nc, 8, LANES), F32),
                        pltpu.VMEM((nc, LANES, LANES), F32)],
        compiler_params=pltpu.CompilerParams(
            dimension_semantics=("arbitrary", "arbitrary"), vmem_limit_bytes=VMEM_LIMIT),
        name="gla",
    )(qk, v, g, z, wgu, bg, nrm)


def _t5_buckets(rel):
    nb = REL_BUCKETS // 2
    ret = (rel > 0).astype(np.int32) * nb
    n = np.abs(rel)
    max_exact = nb // 2
    large = max_exact + (np.log(np.maximum(n, 1).astype(np.float32) / max_exact)
                         / math.log(REL_MAX_DIST / max_exact) * (nb - max_exact)).astype(np.int32)
    large = np.minimum(large, nb - 1)
    return ret + np.where(n < max_exact, n, large)


def _bucket_maps():
    w = SWA_BLOCK
    cq = np.arange(w)[:, None]
    s = np.arange(3 * w)[None, :]
    rel = s - w - cq
    band = np.abs(rel) <= SWA_WINDOW
    buckets = _t5_buckets(rel).astype(np.int32)
    first = band & (s >= w)
    last = band & (s < 2 * w)
    return np.stack([np.where(m, buckets, -1) for m in (first, band, last)]).astype(np.int32)


def _bias_kernel(bm_ref, tab_ref, o_ref):
    pair = pl.program_id(1)
    bm = bm_ref[0]
    for hh in range(2):
        head = 2 * pair + hh
        acc = jnp.full(bm.shape, MASK_VALUE, F32)
        for bkt in range(REL_BUCKETS):
            acc = jnp.where(bm == bkt, tab_ref[bkt, head], acc)
        o_ref[0, 0, :, hh * 3 * SWA_BLOCK:(hh + 1) * 3 * SWA_BLOCK] = acc


def _bias_tables(rel_bias):
    w = SWA_BLOCK
    pairs = SWA_Q_HEADS // 2
    return pl.pallas_call(
        _bias_kernel,
        grid=(3, pairs),
        in_specs=[pl.BlockSpec((1, w, 3 * w), lambda v, p: (v, 0, 0)),
                  pl.BlockSpec(memory_space=pltpu.SMEM)],
        out_specs=pl.BlockSpec((1, 1, w, 6 * w), lambda v, p: (v, p, 0, 0)),
        out_shape=jax.ShapeDtypeStruct((3, pairs, w, 6 * w), F32),
        compiler_params=pltpu.CompilerParams(dimension_semantics=("arbitrary", "arbitrary")),
        name="swa_bias",
    )(jnp.asarray(_bucket_maps()), rel_bias.astype(F32))


def _swa_kernel(q_ref, kv_ref, bias_ref, sink_ref, o_ref, ka_s, kb_s, va_s, vb_s, *, seq):
    w = SWA_BLOCK
    nb = seq // w
    kvh = pl.program_id(1)
    pairs_per_kv = SWA_Q_HEADS // SWA_KV_HEADS // 2
    stage_rows = 512
    lo = lax.broadcasted_iota(jnp.int32, (stage_rows, LANES), 1) < HALF
    zpad = jnp.zeros((w, LANES), BF16)
    for s_ref in (ka_s, kb_s, va_s, vb_s):
        s_ref[0:w, :] = zpad
        s_ref[w + seq:2 * w + seq, :] = zpad

    def stage(i, carry):
        src = pl.ds(pl.multiple_of(i * stage_rows, stage_rows), stage_rows)
        dst = pl.ds(pl.multiple_of(i * stage_rows + w, w), stage_rows)
        kv = kv_ref[src, :].astype(F32)
        vk = pltpu.roll(kv, HALF, axis=1)
        zero = jnp.zeros_like(kv)
        ka_s[dst, :] = jnp.where(lo, kv, zero).astype(BF16)
        kb_s[dst, :] = jnp.where(lo, zero, vk).astype(BF16)
        va_s[dst, :] = jnp.where(lo, vk, zero).astype(BF16)
        vb_s[dst, :] = jnp.where(lo, zero, kv).astype(BF16)
        return carry

    lax.fori_loop(0, seq // stage_rows, stage, 0)

    lo_w = lax.broadcasted_iota(jnp.int32, (w, LANES), 1) < HALF

    def block(n, carry):
        variant = jnp.where(n == 0, 0, jnp.where(n == nb - 1, 2, 1))
        qrows = pl.ds(pl.multiple_of(n * w, w), w)
        band = pl.ds(pl.multiple_of(n * w, w), 3 * w)
        kt = jnp.concatenate([ka_s[band, :], kb_s[band, :]], axis=0)
        vb = jnp.concatenate([va_s[band, :], vb_s[band, :]], axis=0)
        for pair in range(pairs_per_kv):
            cols = slice(pair * LANES, (pair + 1) * LANES)
            logits = _dot_nt(q_ref[qrows, cols], kt) + bias_ref[variant, pair]
            probs, inv = [], []
            for hh in range(2):
                lg = logits[:, hh * 3 * w:(hh + 1) * 3 * w]
                sink = sink_ref[kvh * 2 * pairs_per_kv + 2 * pair + hh]
                m = jnp.maximum(jnp.max(lg, axis=-1, keepdims=True), sink)
                p = jnp.exp(lg - m)
                probs.append(p.astype(BF16))
                inv.append(1.0 / (jnp.sum(p, axis=-1, keepdims=True) + jnp.exp(sink - m)))
            o = _dot(jnp.concatenate(probs, axis=1), vb)
            o = o * jnp.where(lo_w, inv[0], inv[1])
            o_ref[qrows, cols] = o.astype(o_ref.dtype)
        return carry

    lax.fori_loop(0, nb, block, 0)


def _swa(q, kv, bias, sink, batch, seq):
    t = batch * seq
    w = SWA_BLOCK
    qw = Q_SWA_W // SWA_KV_HEADS
    pairs_per_kv = SWA_Q_HEADS // SWA_KV_HEADS // 2
    staged = pltpu.VMEM((seq + 2 * w, LANES), BF16)
    return pl.pallas_call(
        functools.partial(_swa_kernel, seq=seq),
        grid=(batch, SWA_KV_HEADS),
        in_specs=[pl.BlockSpec((seq, qw), lambda b, j: (b, j)),
                  pl.BlockSpec((seq, LANES), lambda b, j: (b, j)),
                  pl.BlockSpec((3, pairs_per_kv, w, 6 * w), lambda b, j: (0, j, 0, 0)),
                  pl.BlockSpec(memory_space=pltpu.SMEM)],
        out_specs=pl.BlockSpec((seq, qw), lambda b, j: (b, j)),
        out_shape=jax.ShapeDtypeStruct((t, Q_SWA_W), BF16),
        scratch_shapes=[staged, staged, staged, staged],
        compiler_params=pltpu.CompilerParams(
            dimension_semantics=("arbitrary", "arbitrary"), vmem_limit_bytes=VMEM_LIMIT),
        name="swa",
    )(q, kv, bias, sink)


def _ffn_kernel(x_ref, oa_ref, os_ref, wa_ref, ws_ref, gpost_ref, gpre_ref, wup_ref, wdn_ref,
                gout_ref, o_ref):
    mix = _dot(oa_ref[...], wa_ref[...]) + _dot(os_ref[...], ws_ref[...])
    h = x_ref[...] + _rms(mix) * gpost_ref[...]
    hn = (_rms(h) * gpre_ref[...]).astype(BF16)
    acc = jnp.zeros(h.shape, F32)
    for j in range(D_FF // FF_BLOCK):
        cols = slice(j * FF_BLOCK, (j + 1) * FF_BLOCK)
        zj = jnp.maximum(_dot(hn, wup_ref[:, cols]), 0.0)
        acc = acc + _dot((zj * zj).astype(BF16), wdn_ref[cols, :])
    o_ref[...] = h + _rms(acc) * gout_ref[...]


def _ffn(x2, oa, os_, wa, ws, gpost, gpre, wup, wdn, gout):
    t = x2.shape[0]
    row = lambda width: pl.BlockSpec((TOKEN_BLOCK, width), lambda i: (i, 0))
    return pl.pallas_call(
        _ffn_kernel,
        grid=(t // TOKEN_BLOCK,),
        in_specs=[row(D_MODEL), row(V_GLA_W), row(Q_SWA_W),
                  _const_spec((V_GLA_W, D_MODEL)), _const_spec((Q_SWA_W, D_MODEL)),
                  _const_spec((1, D_MODEL)), _const_spec((1, D_MODEL)),
                  _const_spec((D_MODEL, D_FF)), _const_spec((D_FF, D_MODEL)),
                  _const_spec((1, D_MODEL))],
        out_specs=row(D_MODEL),
        out_shape=jax.ShapeDtypeStruct((t, D_MODEL), F32),
        compiler_params=pltpu.CompilerParams(
            dimension_semantics=("arbitrary",), vmem_limit_bytes=VMEM_LIMIT),
        name="outproj_ffn",
    )(x2, oa, os_, wa, ws, gpost, gpre, wup, wdn, gout)


def _regroup_w_in(w_in):
    sizes = (GLA_HEADS * GLA_DK, GLA_HEADS * GLA_DK, V_GLA_W, G_GLA_W, Z_W,
             Q_SWA_W, SWA_KV_HEADS * SWA_DH, SWA_KV_HEADS * SWA_DH)
    starts = np.concatenate([[0], np.cumsum(sizes)[:-1]])
    qa, ka, va, ga, za, qs, ks, vs = (w_in[:, s:s + n] for s, n in zip(starts, sizes))
    cols = []
    for h in range(GLA_HEADS):
        sl = slice(h * GLA_DK, (h + 1) * GLA_DK)
        cols += [qa[:, sl] * GLA_DK ** -0.5, ka[:, sl]]
    cols += [va, ga, qs * SWA_DH ** -0.5]
    for j in range(SWA_KV_HEADS):
        sl = slice(j * SWA_DH, (j + 1) * SWA_DH)
        cols += [ks[:, sl], vs[:, sl]]
    cols += [za]
    return jnp.concatenate(cols, axis=1).astype(BF16)


def _layer(h, norm_mix_pre, w_in, w_gu_f, b_g_f, w_gu_b, b_g_b, gla_norm, swa_sink, bias_tab,
           w_out, norm_mix_post, norm_mlp_pre, w_up, w_down, norm_mlp_post):
    batch, seq, _ = h.shape
    x2 = h.reshape(batch * seq, D_MODEL)
    w_in_r = _regroup_w_in(w_in)
    row = lambda g: g.reshape(1, -1).astype(F32)

    qk, v, g, qs, kv, z = _inproj(x2, row(norm_mix_pre), w_in_r)

    wf = w_gu_f.reshape(GLA_RANK, GLA_HEADS, GLA_DK).transpose(1, 0, 2)
    wb = w_gu_b.reshape(GLA_RANK, GLA_HEADS, GLA_DK).transpose(1, 0, 2)
    zeros = jnp.zeros_like(wf)
    wgu = jnp.concatenate([jnp.concatenate([wf, zeros], axis=2),
                           jnp.concatenate([zeros, wb], axis=2)], axis=1).astype(BF16)
    bg = jnp.concatenate([b_g_f.reshape(GLA_HEADS, 1, GLA_DK),
                          b_g_b.reshape(GLA_HEADS, 1, GLA_DK)], axis=2).astype(F32)
    o_a = _gla(qk, v, g, z, wgu, bg, row(gla_norm), batch, seq)

    o_s = _swa(qs, kv, bias_tab, swa_sink.astype(F32), batch, seq)

    out = _ffn(x2, o_a, o_s, w_out[:V_GLA_W].astype(BF16), w_out[V_GLA_W:].astype(BF16),
               row(norm_mix_post), row(norm_mlp_pre), w_up.astype(BF16), w_down.astype(BF16),
               row(norm_mlp_post))
    return out.reshape(batch, seq, D_MODEL)


def kernel(x, norm_mix_pre, w_in, w_gate_up_fwd, b_gate_fwd, w_gate_up_bwd, b_gate_bwd, gla_norm,
           swa_sink, rel_bias, w_out, norm_mix_post, norm_mlp_pre, w_up, w_down, norm_mlp_post):
    bias_tab = _bias_tables(rel_bias)
    h = x
    for l in range(w_in.shape[0]):
        h = _layer(h, norm_mix_pre[l], w_in[l], w_gate_up_fwd[l], b_gate_fwd[l], w_gate_up_bwd[l],
                   b_gate_bwd[l], gla_norm[l], swa_sink[l], bias_tab, w_out[l], norm_mix_post[l],
                   norm_mlp_pre[l], w_up[l], w_down[l], norm_mlp_post[l])
    return h
```

```python
import functools
import math

import jax
import jax.numpy as jnp
import numpy as np
from jax import lax
from jax.experimental import pallas as pl
from jax.experimental.pallas import tpu as pltpu

F32 = jnp.float32
BF16 = jnp.bfloat16

D_MODEL = 1024
GLA_HEADS = 4
GLA_DK = 64
GLA_DV = 128
GLA_RANK = 16
GLA_GATE_NORM = 16.0
GLA_CHUNK = 64
SWA_DH = 64
SWA_Q_HEADS = 8
SWA_KV_HEADS = 2
SWA_BLOCK = 128
SWA_WINDOW = 128
REL_BUCKETS = 32
REL_MAX_DIST = 128
D_FF = 4 * D_MODEL
NORM_EPS = 1e-6
MASK_VALUE = -1e30

LANES = 128
HALF = LANES // 2

QK_GLA_W = GLA_HEADS * 2 * GLA_DK
V_GLA_W = GLA_HEADS * GLA_DV
G_GLA_W = GLA_HEADS * GLA_DV
Q_SWA_W = SWA_Q_HEADS * SWA_DH
KV_SWA_W = SWA_KV_HEADS * 2 * SWA_DH
Z_W = 2 * GLA_RANK
IN_COLS = QK_GLA_W + V_GLA_W + G_GLA_W + Q_SWA_W + KV_SWA_W + Z_W

GLA_UNROLL = 8
GLA_ROWS_A = 512
GLA_CUMSUM_ROWS = 256
SWA_UNROLL = 2
TOKEN_BLOCK = 512
FF_BLOCK = 1024
VMEM_LIMIT = 56 * 1024 * 1024


def _rms(x):
    return x * lax.rsqrt(jnp.mean(x * x, axis=-1, keepdims=True) + NORM_EPS)


def _dot(a, b):
    return jnp.dot(a, b, preferred_element_type=F32)


def _dot_nt(a, b):
    return lax.dot_general(a, b, (((1,), (1,)), ((), ())), preferred_element_type=F32)


def _dot_tn(a, b):
    return lax.dot_general(a, b, (((0,), (0,)), ((), ())), preferred_element_type=F32)


def _const_spec(shape):
    nd = len(shape)
    return pl.BlockSpec(shape, lambda *_: (0,) * nd, pipeline_mode=pl.Buffered(1))


def _inproj_kernel(x_ref, g_ref, w_ref, qk_ref, v_ref, gg_ref, qs_ref, kv_ref, z_ref):
    u = (_rms(x_ref[...]) * g_ref[...]).astype(BF16)
    off = 0
    for ref, width in ((qk_ref, QK_GLA_W), (v_ref, V_GLA_W), (gg_ref, G_GLA_W),
                       (qs_ref, Q_SWA_W), (kv_ref, KV_SWA_W), (z_ref, Z_W)):
        ref[...] = _dot(u, w_ref[:, off:off + width]).astype(ref.dtype)
        off += width


def _inproj(x2, g, w):
    t = x2.shape[0]
    row = lambda width: pl.BlockSpec((TOKEN_BLOCK, width), lambda i: (i, 0))
    widths = (QK_GLA_W, V_GLA_W, G_GLA_W, Q_SWA_W, KV_SWA_W, Z_W)
    dtypes = (BF16, BF16, BF16, BF16, BF16, F32)
    return pl.pallas_call(
        _inproj_kernel,
        grid=(t // TOKEN_BLOCK,),
        in_specs=[row(D_MODEL), _const_spec((1, D_MODEL)), _const_spec((D_MODEL, IN_COLS))],
        out_specs=[row(wd) for wd in widths],
        out_shape=[jax.ShapeDtypeStruct((t, wd), dt) for wd, dt in zip(widths, dtypes)],
        compiler_params=pltpu.CompilerParams(
            dimension_semantics=("arbitrary",), vmem_limit_bytes=VMEM_LIMIT),
        name="inproj",
    )(x2, g, w)


def _cumsum_matrix():
    g = GLA_CUMSUM_ROWS
    r = np.arange(g)[:, None]
    s = np.arange(g)[None, :]
    same = (r // GLA_CHUNK) == (s // GLA_CHUNK)
    return np.concatenate([same & (s <= r), same & (s >= r)], axis=0).astype(np.float32)


def _gla_kernel(qk_ref, v_ref, g_ref, z_ref, wgu_ref, bg_ref, nrm_ref, cm_ref, o_ref,
                la_s, qdec_s, kina_s, kinb_s, kst_s, am_s, ds_s, dec_s, sc_s, *, seq):
    c = GLA_CHUNK
    nc = seq // c
    grp = GLA_CUMSUM_ROWS
    cpg = grp // c

    wgu = wgu_ref[0]
    bg = bg_ref[0]

    def decay_logs(i, carry):
        rows = pl.ds(pl.multiple_of(i * GLA_ROWS_A, GLA_ROWS_A), GLA_ROWS_A)
        pre = _dot(z_ref[rows, :].astype(BF16), wgu) + bg
        la_s[rows, :] = ((jnp.minimum(pre, 0.0) - jnp.log1p(jnp.exp(-jnp.abs(pre))))
                         * (1.0 / GLA_GATE_NORM))
        return carry

    lax.fori_loop(0, seq // GLA_ROWS_A, decay_logs, 0)

    fwd_g = lax.broadcasted_iota(jnp.int32, (grp, LANES), 1) < HALF
    zero_g = jnp.zeros((grp, LANES), F32)

    def decayed_qk(i, carry):
        rows = pl.ds(pl.multiple_of(i * grp, grp), grp)
        la = la_s[rows, :]
        la_hi = la.astype(BF16)
        la_lo = (la - la_hi.astype(F32)).astype(BF16)
        cs = _dot(cm_ref[...], jnp.concatenate([la_hi, la_lo], axis=1))
        cs = cs[:, :LANES] + cs[:, LANES:]
        b = jnp.where(fwd_g, cs[:grp], cs[grp:])
        tot = jnp.sum(la.reshape(cpg, c, LANES), axis=1, keepdims=True)
        rest = jnp.exp(tot - b.reshape(cpg, c, LANES)).reshape(grp, LANES)
        x = qk_ref[rows, :].astype(F32)
        xr = pltpu.roll(x, HALF, axis=1)
        qq = jnp.where(fwd_g, x, xr)
        kk = jnp.where(fwd_g, xr, x)
        kin = kk * jnp.exp(-b)
        qdec_s[rows, :] = (qq * jnp.exp(b)).astype(BF16)
        kina_s[rows, :] = jnp.where(fwd_g, kin, zero_g).astype(BF16)
        kinb_s[rows, :] = jnp.where(fwd_g, zero_g, kin).astype(BF16)
        kst_s[rows, :] = (kk * rest).astype(BF16)
        dec_s[pl.ds(pl.multiple_of(i * cpg, cpg), cpg)] = jnp.broadcast_to(
            jnp.exp(tot), (cpg, 8, LANES))
        return carry

    lax.fori_loop(0, seq // grp, decayed_qk, 0)

    lane_c = lax.broadcasted_iota(jnp.int32, (c, LANES), 1)
    row_c = lax.broadcasted_iota(jnp.int32, (c, LANES), 0)
    tri = ((lane_c < HALF) & (lane_c <= row_c)) | ((lane_c >= HALF) & ((lane_c - HALF) >= row_c))
    zero_c = jnp.zeros((c, LANES), F32)

    def chunk_products(n, carry):
        rows = pl.ds(pl.multiple_of(n * c, c), c)
        kin_bd = jnp.concatenate([kina_s[rows, :], kinb_s[rows, :]], axis=0)
        a2 = _dot_nt(qdec_s[rows, :], kin_bd)
        am_s[rows, :] = jnp.where(tri, a2, zero_c).astype(BF16)
        ds_s[n] = _dot_tn(v_ref[rows, :], kst_s[rows, :])
        return carry

    lax.fori_loop(0, nc, chunk_products, 0, unroll=GLA_UNROLL)

    fwd_sq = lax.broadcasted_iota(jnp.int32, (LANES, LANES), 1) < HALF
    fwd_row = lax.broadcasted_iota(jnp.int32, (8, LANES), 1) < HALF

    def scan_step(i, s):
        j = nc - 1 - i
        sc_s[i] = s
        dec = jnp.where(fwd_row, dec_s[i], dec_s[j])[0:1]
        return dec * s + jnp.where(fwd_sq, ds_s[i], ds_s[j])

    lax.fori_loop(0, nc, scan_step, jnp.zeros((LANES, LANES), F32), unroll=GLA_UNROLL)

    nrm = nrm_ref[...]

    def out_pass(n, carry):
        rows = pl.ds(pl.multiple_of(n * c, c), c)
        st = jnp.where(fwd_sq, sc_s[n], sc_s[nc - 1 - n]).astype(BF16)
        vc = v_ref[rows, :]
        o = _dot_nt(qdec_s[rows, :], st) + _dot(am_s[rows, :], jnp.concatenate([vc, vc], axis=0))
        gate = g_ref[rows, :].astype(F32)
        o = (_rms(o) * nrm) * (gate / (1.0 + jnp.exp(-gate)))
        o_ref[rows, :] = o.astype(o_ref.dtype)
        return carry

    lax.fori_loop(0, nc, out_pass, 0, unroll=GLA_UNROLL)


def _gla(qk, v, g, z, wgu, bg, nrm, batch, seq):
    t = batch * seq
    nc = seq // GLA_CHUNK
    tile = pl.BlockSpec((seq, LANES), lambda b, h: (b, h))
    tok_bf = pltpu.VMEM((seq, LANES), BF16)
    cm = jnp.asarray(_cumsum_matrix(), dtype=BF16)
    return pl.pallas_call(
        functools.partial(_gla_kernel, seq=seq),
        grid=(batch, GLA_HEADS),
        in_specs=[tile, tile, tile,
                  pl.BlockSpec((seq, Z_W), lambda b, h: (b, 0)),
                  pl.BlockSpec((1, Z_W, LANES), lambda b, h: (h, 0, 0)),
                  pl.BlockSpec((1, 1, LANES), lambda b, h: (h, 0, 0)),
                  pl.BlockSpec((1, LANES), lambda b, h: (0, 0)),
                  _const_spec(cm.shape)],
        out_specs=tile,
        out_shape=jax.ShapeDtypeStruct((t, V_GLA_W), BF16),
        scratch_shapes=[pltpu.VMEM((seq, LANES), F32),
                        tok_bf, tok_bf, tok_bf, tok_bf, tok_bf,
                        pltpu.VMEM((nc, LANES, LANES), F32),
                        pltpu.VMEM((nc, 8, LANES), F32),
                        pltpu.VMEM((nc, LANES, LANES), F32)],
        compiler_params=pltpu.CompilerParams(
            dimension_semantics=("arbitrary", "arbitrary"), vmem_limit_bytes=VMEM_LIMIT),
        name="gla",
    )(qk, v, g, z, wgu, bg, nrm, cm)


def _t5_buckets(rel):
    nb = REL_BUCKETS // 2
    ret = (rel > 0).astype(np.int32) * nb
    n = np.abs(rel)
    max_exact = nb // 2
    large = max_exact + (np.log(np.maximum(n, 1).astype(np.float32) / max_exact)
                         / math.log(REL_MAX_DIST / max_exact) * (nb - max_exact)).astype(np.int32)
    large = np.minimum(large, nb - 1)
    return ret + np.where(n < max_exact, n, large)


def _bucket_maps():
    w = SWA_BLOCK
    cq = np.arange(w)[:, None]
    s = np.arange(3 * w)[None, :]
    rel = s - w - cq
    band = np.abs(rel) <= SWA_WINDOW
    buckets = _t5_buckets(rel).astype(np.int32)
    first = band & (s >= w)
    last = band & (s < 2 * w)
    return np.stack([np.where(m, buckets, -1) for m in (first, band, last)]).astype(np.int32)


def _bias_kernel(bm_ref, tab_ref, o_ref):
    pair = pl.program_id(1)
    bm = bm_ref[0]
    for hh in range(2):
        head = 2 * pair + hh
        acc = jnp.full(bm.shape, MASK_VALUE, F32)
        for bkt in range(REL_BUCKETS):
            acc = jnp.where(bm == bkt, tab_ref[bkt, head], acc)
        o_ref[0, 0, :, hh * 3 * SWA_BLOCK:(hh + 1) * 3 * SWA_BLOCK] = acc


def _bias_tables(rel_bias):
    w = SWA_BLOCK
    pairs = SWA_Q_HEADS // 2
    return pl.pallas_call(
        _bias_kernel,
        grid=(3, pairs),
        in_specs=[pl.BlockSpec((1, w, 3 * w), lambda v, p: (v, 0, 0)),
                  pl.BlockSpec(memory_space=pltpu.SMEM)],
        out_specs=pl.BlockSpec((1, 1, w, 6 * w), lambda v, p: (v, p, 0, 0)),
        out_shape=jax.ShapeDtypeStruct((3, pairs, w, 6 * w), F32),
        compiler_params=pltpu.CompilerParams(dimension_semantics=("arbitrary", "arbitrary")),
        name="swa_bias",
    )(jnp.asarray(_bucket_maps()), rel_bias.astype(F32))


def _swa_kernel(q_ref, kv_ref, bias_ref, sink_ref, o_ref, ka_s, kb_s, va_s, vb_s, *, seq):
    w = SWA_BLOCK
    nb = seq // w
    kvh = pl.program_id(1)
    pairs_per_kv = SWA_Q_HEADS // SWA_KV_HEADS // 2
    stage_rows = 512
    lo = lax.broadcasted_iota(jnp.int32, (stage_rows, LANES), 1) < HALF
    zpad = jnp.zeros((w, LANES), BF16)
    for s_ref in (ka_s, kb_s, va_s, vb_s):
        s_ref[0:w, :] = zpad
        s_ref[w + seq:2 * w + seq, :] = zpad

    def stage(i, carry):
        src = pl.ds(pl.multiple_of(i * stage_rows, stage_rows), stage_rows)
        dst = pl.ds(pl.multiple_of(i * stage_rows + w, w), stage_rows)
        kv = kv_ref[src, :].astype(F32)
        vk = pltpu.roll(kv, HALF, axis=1)
        zero = jnp.zeros_like(kv)
        ka_s[dst, :] = jnp.where(lo, kv, zero).astype(BF16)
        kb_s[dst, :] = jnp.where(lo, zero, vk).astype(BF16)
        va_s[dst, :] = jnp.where(lo, vk, zero).astype(BF16)
        vb_s[dst, :] = jnp.where(lo, zero, kv).astype(BF16)
        return carry

    lax.fori_loop(0, seq // stage_rows, stage, 0)

    lo_w = lax.broadcasted_iota(jnp.int32, (w, LANES), 1) < HALF

    def block(n, carry):
        variant = jnp.where(n == 0, 0, jnp.where(n == nb - 1, 2, 1))
        qrows = pl.ds(pl.multiple_of(n * w, w), w)
        band = pl.ds(pl.multiple_of(n * w, w), 3 * w)
        kt = jnp.concatenate([ka_s[band, :], kb_s[band, :]], axis=0)
        vb = jnp.concatenate([va_s[band, :], vb_s[band, :]], axis=0)
        for pair in range(pairs_per_kv):
            cols = slice(pair * LANES, (pair + 1) * LANES)
            logits = _dot_nt(q_ref[qrows, cols], kt) + bias_ref[variant, pair]
            probs, inv = [], []
            for hh in range(2):
                lg = logits[:, hh * 3 * w:(hh + 1) * 3 * w]
                sink = sink_ref[kvh * 2 * pairs_per_kv + 2 * pair + hh]
                m = jnp.maximum(jnp.max(lg, axis=-1, keepdims=True), sink)
                p = jnp.exp(lg - m)
                probs.append(p.astype(BF16))
                inv.append(1.0 / (jnp.sum(p, axis=-1, keepdims=True) + jnp.exp(sink - m)))
            o = _dot(jnp.concatenate(probs, axis=1), vb)
            o = o * jnp.where(lo_w, inv[0], inv[1])
            o_ref[qrows, cols] = o.astype(o_ref.dtype)
        return carry

    lax.fori_loop(0, nb, block, 0, unroll=SWA_UNROLL)


def _swa(q, kv, bias, sink, batch, seq):
    t = batch * seq
    w = SWA_BLOCK
    qw = Q_SWA_W // SWA_KV_HEADS
    pairs_per_kv = SWA_Q_HEADS // SWA_KV_HEADS // 2
    staged = pltpu.VMEM((seq + 2 * w, LANES), BF16)
    return pl.pallas_call(
        functools.partial(_swa_kernel, seq=seq),
        grid=(batch, SWA_KV_HEADS),
        in_specs=[pl.BlockSpec((seq, qw), lambda b, j: (b, j)),
                  pl.BlockSpec((seq, LANES), lambda b, j: (b, j)),
                  pl.BlockSpec((3, pairs_per_kv, w, 6 * w), lambda b, j: (0, j, 0, 0)),
                  pl.BlockSpec(memory_space=pltpu.SMEM)],
        out_specs=pl.BlockSpec((seq, qw), lambda b, j: (b, j)),
        out_shape=jax.ShapeDtypeStruct((t, Q_SWA_W), BF16),
        scratch_shapes=[staged, staged, staged, staged],
        compiler_params=pltpu.CompilerParams(
            dimension_semantics=("arbitrary", "arbitrary"), vmem_limit_bytes=VMEM_LIMIT),
        name="swa",
    )(q, kv, bias, sink)


def _ffn_kernel(x_ref, oa_ref, os_ref, wa_ref, ws_ref, gpost_ref, gpre_ref, wup_ref, wdn_ref,
                gout_ref, o_ref):
    mix = _dot(oa_ref[...], wa_ref[...]) + _dot(os_ref[...], ws_ref[...])
    h = x_ref[...] + _rms(mix) * gpost_ref[...]
    hn = (_rms(h) * gpre_ref[...]).astype(BF16)
    acc = jnp.zeros(h.shape, F32)
    for j in range(D_FF // FF_BLOCK):
        cols = slice(j * FF_BLOCK, (j + 1) * FF_BLOCK)
        zj = jnp.maximum(_dot(hn, wup_ref[:, cols]), 0.0)
        acc = acc + _dot((zj * zj).astype(BF16), wdn_ref[cols, :])
    o_ref[...] = h + _rms(acc) * gout_ref[...]


def _ffn(x2, oa, os_, wa, ws, gpost, gpre, wup, wdn, gout):
    t = x2.shape[0]
    row = lambda width: pl.BlockSpec((TOKEN_BLOCK, width), lambda i: (i, 0))
    return pl.pallas_call(
        _ffn_kernel,
        grid=(t // TOKEN_BLOCK,),
        in_specs=[row(D_MODEL), row(V_GLA_W), row(Q_SWA_W),
                  _const_spec((V_GLA_W, D_MODEL)), _const_spec((Q_SWA_W, D_MODEL)),
                  _const_spec((1, D_MODEL)), _const_spec((1, D_MODEL)),
                  _const_spec((D_MODEL, D_FF)), _const_spec((D_FF, D_MODEL)),
                  _const_spec((1, D_MODEL))],
        out_specs=row(D_MODEL),
        out_shape=jax.ShapeDtypeStruct((t, D_MODEL), F32),
        compiler_params=pltpu.CompilerParams(
            dimension_semantics=("arbitrary",), vmem_limit_bytes=VMEM_LIMIT),
        name="outproj_ffn",
    )(x2, oa, os_, wa, ws, gpost, gpre, wup, wdn, gout)


def _regroup_w_in(w_in):
    sizes = (GLA_HEADS * GLA_DK, GLA_HEADS * GLA_DK, V_GLA_W, G_GLA_W, Z_W,
             Q_SWA_W, SWA_KV_HEADS * SWA_DH, SWA_KV_HEADS * SWA_DH)
    starts = np.concatenate([[0], np.cumsum(sizes)[:-1]])
    qa, ka, va, ga, za, qs, ks, vs = (w_in[:, s:s + n] for s, n in zip(starts, sizes))
    cols = []
    for h in range(GLA_HEADS):
        sl = slice(h * GLA_DK, (h + 1) * GLA_DK)
        cols += [qa[:, sl] * GLA_DK ** -0.5, ka[:, sl]]
    cols += [va, ga, qs * SWA_DH ** -0.5]
    for j in range(SWA_KV_HEADS):
        sl = slice(j * SWA_DH, (j + 1) * SWA_DH)
        cols += [ks[:, sl], vs[:, sl]]
    cols += [za]
    return jnp.concatenate(cols, axis=1).astype(BF16)


def _layer(h, norm_mix_pre, w_in, w_gu_f, b_g_f, w_gu_b, b_g_b, gla_norm, swa_sink, bias_tab,
           w_out, norm_mix_post, norm_mlp_pre, w_up, w_down, norm_mlp_post):
    batch, seq, _ = h.shape
    x2 = h.reshape(batch * seq, D_MODEL)
    w_in_r = _regroup_w_in(w_in)
    row = lambda g: g.reshape(1, -1).astype(F32)

    qk, v, g, qs, kv, z = _inproj(x2, row(norm_mix_pre), w_in_r)

    wf = w_gu_f.reshape(GLA_RANK, GLA_HEADS, GLA_DK).transpose(1, 0, 2)
    wb = w_gu_b.reshape(GLA_RANK, GLA_HEADS, GLA_DK).transpose(1, 0, 2)
    zeros = jnp.zeros_like(wf)
    wgu = jnp.concatenate([jnp.concatenate([wf, zeros], axis=2),
                           jnp.concatenate([zeros, wb], axis=2)], axis=1).astype(BF16)
    bg = jnp.concatenate([b_g_f.reshape(GLA_HEADS, 1, GLA_DK),
                          b_g_b.reshape(GLA_HEADS, 1, GLA_DK)], axis=2).astype(F32)
    o_a = _gla(qk, v, g, z, wgu, bg, row(gla_norm), batch, seq)

    o_s = _swa(qs, kv, bias_tab, swa_sink.astype(F32), batch, seq)

    out = _ffn(x2, o_a, o_s, w_out[:V_GLA_W].astype(BF16), w_out[V_GLA_W:].astype(BF16),
               row(norm_mix_post), row(norm_mlp_pre), w_up.astype(BF16), w_down.astype(BF16),
               row(norm_mlp_post))
    return out.reshape(batch, seq, D_MODEL)


def kernel(x, norm_mix_pre, w_in, w_gate_up_fwd, b_gate_fwd, w_gate_up_bwd, b_gate_bwd, gla_norm,
           swa_sink, rel_bias, w_out, norm_mix_post, norm_mlp_pre, w_up, w_down, norm_mlp_post):
    bias_tab = _bias_tables(rel_bias)
    h = x
    for l in range(w_in.shape[0]):
        h = _layer(h, norm_mix_pre[l], w_in[l], w_gate_up_fwd[l], b_gate_fwd[l], w_gate_up_bwd[l],
                   b_gate_bwd[l], gla_norm[l], swa_sink[l], bias_tab, w_out[l], norm_mix_post[l],
                   norm_mlp_pre[l], w_up[l], w_down[l], norm_mlp_post[l])
    return h
```

```python
import functools
import math

import jax
import jax.numpy as jnp
import numpy as np
from jax import lax
from jax.experimental import pallas as pl
from jax.experimental.pallas import tpu as pltpu

F32 = jnp.float32
BF16 = jnp.bfloat16

D_MODEL = 1024
GLA_HEADS = 4
GLA_DK = 64
GLA_DV = 128
GLA_RANK = 16
GLA_GATE_NORM = 16.0
GLA_CHUNK = 64
SWA_DH = 64
SWA_Q_HEADS = 8
SWA_KV_HEADS = 2
SWA_BLOCK = 128
SWA_WINDOW = 128
REL_BUCKETS = 32
REL_MAX_DIST = 128
D_FF = 4 * D_MODEL
NORM_EPS = 1e-6
MASK_VALUE = -1e30

LANES = 128
HALF = LANES // 2

QK_GLA_W = GLA_HEADS * 2 * GLA_DK
V_GLA_W = GLA_HEADS * GLA_DV
G_GLA_W = GLA_HEADS * GLA_DV
Q_SWA_W = SWA_Q_HEADS * SWA_DH
KV_SWA_W = SWA_KV_HEADS * 2 * SWA_DH
Z_W = 2 * GLA_RANK
IN_COLS = QK_GLA_W + V_GLA_W + G_GLA_W + Q_SWA_W + KV_SWA_W + Z_W

GLA_UNROLL = 8
GLA_ROWS_A = 512
GLA_CUMSUM_ROWS = 256
TOKEN_BLOCK = 512
FF_BLOCK = 1024
VMEM_LIMIT = 56 * 1024 * 1024


def _rms(x):
    return x * lax.rsqrt(jnp.mean(x * x, axis=-1, keepdims=True) + NORM_EPS)


def _dot(a, b):
    return jnp.dot(a, b, preferred_element_type=F32)


def _dot_nt(a, b):
    return lax.dot_general(a, b, (((1,), (1,)), ((), ())), preferred_element_type=F32)


def _dot_tn(a, b):
    return lax.dot_general(a, b, (((0,), (0,)), ((), ())), preferred_element_type=F32)


def _const_spec(shape):
    nd = len(shape)
    return pl.BlockSpec(shape, lambda *_: (0,) * nd, pipeline_mode=pl.Buffered(1))


def _inproj_kernel(x_ref, g_ref, w_ref, qk_ref, v_ref, gg_ref, qs_ref, kv_ref, z_ref):
    u = (_rms(x_ref[...]) * g_ref[...]).astype(BF16)
    off = 0
    for ref, width in ((qk_ref, QK_GLA_W), (v_ref, V_GLA_W), (gg_ref, G_GLA_W),
                       (qs_ref, Q_SWA_W), (kv_ref, KV_SWA_W), (z_ref, Z_W)):
        ref[...] = _dot(u, w_ref[:, off:off + width]).astype(ref.dtype)
        off += width


def _inproj(x2, g, w):
    t = x2.shape[0]
    row = lambda width: pl.BlockSpec((TOKEN_BLOCK, width), lambda i: (i, 0))
    widths = (QK_GLA_W, V_GLA_W, G_GLA_W, Q_SWA_W, KV_SWA_W, Z_W)
    dtypes = (BF16, BF16, BF16, BF16, BF16, F32)
    return pl.pallas_call(
        _inproj_kernel,
        grid=(t // TOKEN_BLOCK,),
        in_specs=[row(D_MODEL), _const_spec((1, D_MODEL)), _const_spec((D_MODEL, IN_COLS))],
        out_specs=[row(wd) for wd in widths],
        out_shape=[jax.ShapeDtypeStruct((t, wd), dt) for wd, dt in zip(widths, dtypes)],
        compiler_params=pltpu.CompilerParams(
            dimension_semantics=("arbitrary",), vmem_limit_bytes=VMEM_LIMIT),
        name="inproj",
    )(x2, g, w)


def _cumsum_matrix():
    g = GLA_CUMSUM_ROWS
    r = np.arange(g)[:, None]
    s = np.arange(g)[None, :]
    same = (r // GLA_CHUNK) == (s // GLA_CHUNK)
    return np.concatenate([same & (s <= r), same & (s >= r)], axis=0).astype(np.float32)


def _gla_kernel(qk_ref, v_ref, g_ref, z_ref, wgu_ref, bg_ref, nrm_ref, cm_ref, o_ref,
                la_s, qdec_s, kina_s, kinb_s, kst_s, am_s, ds_s, dec_s, sc_s, *, seq):
    c = GLA_CHUNK
    nc = seq // c
    grp = GLA_CUMSUM_ROWS
    cpg = grp // c

    wgu = wgu_ref[0]
    bg = bg_ref[0]

    def decay_logs(i, carry):
        rows = pl.ds(pl.multiple_of(i * GLA_ROWS_A, GLA_ROWS_A), GLA_ROWS_A)
        pre = _dot(z_ref[rows, :].astype(BF16), wgu) + bg
        la_s[rows, :] = ((jnp.minimum(pre, 0.0) - jnp.log1p(jnp.exp(-jnp.abs(pre))))
                         * (1.0 / GLA_GATE_NORM))
        return carry

    lax.fori_loop(0, seq // GLA_ROWS_A, decay_logs, 0)

    fwd_g = lax.broadcasted_iota(jnp.int32, (grp, LANES), 1) < HALF
    zero_g = jnp.zeros((grp, LANES), F32)

    def decayed_qk(i, carry):
        rows = pl.ds(pl.multiple_of(i * grp, grp), grp)
        la = la_s[rows, :]
        la_hi = la.astype(BF16)
        la_lo = (la - la_hi.astype(F32)).astype(BF16)
        cs = _dot(cm_ref[...], jnp.concatenate([la_hi, la_lo], axis=1))
        cs = cs[:, :LANES] + cs[:, LANES:]
        b = jnp.where(fwd_g, cs[:grp], cs[grp:])
        tot = jnp.sum(la.reshape(cpg, c, LANES), axis=1, keepdims=True)
        rest = jnp.exp(tot - b.reshape(cpg, c, LANES)).reshape(grp, LANES)
        x = qk_ref[rows, :].astype(F32)
        xr = pltpu.roll(x, HALF, axis=1)
        qq = jnp.where(fwd_g, x, xr)
        kk = jnp.where(fwd_g, xr, x)
        kin = kk * jnp.exp(-b)
        qdec_s[rows, :] = (qq * jnp.exp(b)).astype(BF16)
        kina_s[rows, :] = jnp.where(fwd_g, kin, zero_g).astype(BF16)
        kinb_s[rows, :] = jnp.where(fwd_g, zero_g, kin).astype(BF16)
        kst_s[rows, :] = (kk * rest).astype(BF16)
        dec_s[pl.ds(pl.multiple_of(i * cpg, cpg), cpg)] = jnp.broadcast_to(
            jnp.exp(tot), (cpg, 8, LANES))
        return carry

    lax.fori_loop(0, seq // grp, decayed_qk, 0)

    lane_c = lax.broadcasted_iota(jnp.int32, (c, LANES), 1)
    row_c = lax.broadcasted_iota(jnp.int32, (c, LANES), 0)
    tri = ((lane_c < HALF) & (lane_c <= row_c)) | ((lane_c >= HALF) & ((lane_c - HALF) >= row_c))
    zero_c = jnp.zeros((c, LANES), F32)

    def chunk_products(n, carry):
        rows = pl.ds(pl.multiple_of(n * c, c), c)
        kin_bd = jnp.concatenate([kina_s[rows, :], kinb_s[rows, :]], axis=0)
        a2 = _dot_nt(qdec_s[rows, :], kin_bd)
        am_s[rows, :] = jnp.where(tri, a2, zero_c).astype(BF16)
        ds_s[n] = _dot_tn(v_ref[rows, :], kst_s[rows, :])
        return carry

    lax.fori_loop(0, nc, chunk_products, 0, unroll=GLA_UNROLL)

    fwd_sq = lax.broadcasted_iota(jnp.int32, (LANES, LANES), 1) < HALF
    fwd_row = lax.broadcasted_iota(jnp.int32, (8, LANES), 1) < HALF

    def scan_step(i, s):
        j = nc - 1 - i
        sc_s[i] = s
        dec = jnp.where(fwd_row, dec_s[i], dec_s[j])[0:1]
        return dec * s + jnp.where(fwd_sq, ds_s[i], ds_s[j])

    lax.fori_loop(0, nc, scan_step, jnp.zeros((LANES, LANES), F32), unroll=GLA_UNROLL)

    nrm = nrm_ref[...]

    def out_pass(n, carry):
        rows = pl.ds(pl.multiple_of(n * c, c), c)
        st = jnp.where(fwd_sq, sc_s[n], sc_s[nc - 1 - n]).astype(BF16)
        vc = v_ref[rows, :]
        o = _dot_nt(qdec_s[rows, :], st) + _dot(am_s[rows, :], jnp.concatenate([vc, vc], axis=0))
        gate = g_ref[rows, :].astype(F32)
        o = (_rms(o) * nrm) * (gate / (1.0 + jnp.exp(-gate)))
        o_ref[rows, :] = o.astype(o_ref.dtype)
        return carry

    lax.fori_loop(0, nc, out_pass, 0, unroll=GLA_UNROLL)


def _gla(qk, v, g, z, wgu, bg, nrm, batch, seq):
    t = batch * seq
    nc = seq // GLA_CHUNK
    tile = pl.BlockSpec((seq, LANES), lambda b, h: (b, h))
    tok_bf = pltpu.VMEM((seq, LANES), BF16)
    cm = jnp.asarray(_cumsum_matrix(), dtype=BF16)
    return pl.pallas_call(
        functools.partial(_gla_kernel, seq=seq),
        grid=(batch, GLA_HEADS),
        in_specs=[tile, tile, tile,
                  pl.BlockSpec((seq, Z_W), lambda b, h: (b, 0)),
                  pl.BlockSpec((1, Z_W, LANES), lambda b, h: (h, 0, 0)),
                  pl.BlockSpec((1, 1, LANES), lambda b, h: (h, 0, 0)),
                  pl.BlockSpec((1, LANES), lambda b, h: (0, 0)),
                  _const_spec(cm.shape)],
        out_specs=tile,
        out_shape=jax.ShapeDtypeStruct((t, V_GLA_W), BF16),
        scratch_shapes=[pltpu.VMEM((seq, LANES), F32),
                        tok_bf, tok_bf, tok_bf, tok_bf, tok_bf,
                        pltpu.VMEM((nc, LANES, LANES), F32),
                        pltpu.VMEM((nc, 8, LANES), F32),
                        pltpu.VMEM((nc, LANES, LANES), F32)],
        compiler_params=pltpu.CompilerParams(
            dimension_semantics=("arbitrary", "arbitrary"), vmem_limit_bytes=VMEM_LIMIT),
        name="gla",
    )(qk, v, g, z, wgu, bg, nrm, cm)


def _t5_buckets(rel):
    nb = REL_BUCKETS // 2
    ret = (rel > 0).astype(np.int32) * nb
    n = np.abs(rel)
    max_exact = nb // 2
    large = max_exact + (np.log(np.maximum(n, 1).astype(np.float32) / max_exact)
                         / math.log(REL_MAX_DIST / max_exact) * (nb - max_exact)).astype(np.int32)
    large = np.minimum(large, nb - 1)
    return ret + np.where(n < max_exact, n, large)


def _bucket_maps():
    w = SWA_BLOCK
    cq = np.arange(w)[:, None]
    s = np.arange(3 * w)[None, :]
    rel = s - w - cq
    band = np.abs(rel) <= SWA_WINDOW
    buckets = _t5_buckets(rel).astype(np.int32)
    first = band & (s >= w)
    last = band & (s < 2 * w)
    return np.stack([np.where(m, buckets, -1) for m in (first, band, last)]).astype(np.int32)


def _bias_kernel(bm_ref, tab_ref, o_ref):
    pair = pl.program_id(1)
    bm = bm_ref[0]
    for hh in range(2):
        head = 2 * pair + hh
        acc = jnp.full(bm.shape, MASK_VALUE, F32)
        for bkt in range(REL_BUCKETS):
            acc = jnp.where(bm == bkt, tab_ref[bkt, head], acc)
        o_ref[0, 0, :, hh * 3 * SWA_BLOCK:(hh + 1) * 3 * SWA_BLOCK] = acc


def _bias_tables(rel_bias):
    w = SWA_BLOCK
    pairs = SWA_Q_HEADS // 2
    return pl.pallas_call(
        _bias_kernel,
        grid=(3, pairs),
        in_specs=[pl.BlockSpec((1, w, 3 * w), lambda v, p: (v, 0, 0)),
                  pl.BlockSpec(memory_space=pltpu.SMEM)],
        out_specs=pl.BlockSpec((1, 1, w, 6 * w), lambda v, p: (v, p, 0, 0)),
        out_shape=jax.ShapeDtypeStruct((3, pairs, w, 6 * w), F32),
        compiler_params=pltpu.CompilerParams(dimension_semantics=("arbitrary", "arbitrary")),
        name="swa_bias",
    )(jnp.asarray(_bucket_maps()), rel_bias.astype(F32))


def _swa_kernel(q_ref, kv_ref, bias_ref, sink_ref, o_ref, ka_s, kb_s, va_s, vb_s,
                lg_s, p_s, st_s, *, seq):
    w = SWA_BLOCK
    nb = seq // w
    kvh = pl.program_id(1)
    pairs = SWA_Q_HEADS // SWA_KV_HEADS // 2
    stage_rows = 512
    lo = lax.broadcasted_iota(jnp.int32, (stage_rows, LANES), 1) < HALF
    for s_ref in (ka_s, kb_s, va_s, vb_s):
        zpad = jnp.zeros((w, s_ref.shape[1]), BF16)
        s_ref[0:w, :] = zpad
        s_ref[w + seq:2 * w + seq, :] = zpad

    def stage(i, carry):
        src = pl.ds(pl.multiple_of(i * stage_rows, stage_rows), stage_rows)
        dst = pl.ds(pl.multiple_of(i * stage_rows + w, w), stage_rows)
        kv = kv_ref[src, :].astype(F32)
        vk = pltpu.roll(kv, HALF, axis=1)
        zero = jnp.zeros_like(kv)
        one = jnp.ones_like(kv)
        ka_s[dst, :] = jnp.where(lo, kv, zero).astype(BF16)
        kb_s[dst, :] = jnp.where(lo, zero, vk).astype(BF16)
        va_s[dst, 0:LANES] = jnp.where(lo, vk, zero).astype(BF16)
        va_s[dst, LANES:2 * LANES] = jnp.where(lo, one, zero).astype(BF16)
        vb_s[dst, 0:LANES] = jnp.where(lo, zero, kv).astype(BF16)
        vb_s[dst, LANES:2 * LANES] = jnp.where(lo, zero, one).astype(BF16)
        return carry

    lax.fori_loop(0, seq // stage_rows, stage, 0)

    lo_w = lax.broadcasted_iota(jnp.int32, (w, LANES), 1) < HALF

    def logits_stage(n, variant):
        qrows = pl.ds(pl.multiple_of(n * w, w), w)
        band = pl.ds(pl.multiple_of(n * w, w), 3 * w)
        kt = jnp.concatenate([ka_s[band, :], kb_s[band, :]], axis=0)
        for pair in range(pairs):
            cols = slice(pair * LANES, (pair + 1) * LANES)
            lg_s[pair] = _dot_nt(q_ref[qrows, cols], kt) + bias_ref[variant, pair]

    def softmax_stage():
        for pair in range(pairs):
            sink_terms = []
            for hh in range(2):
                hcols = slice(hh * 3 * w, (hh + 1) * 3 * w)
                lg = lg_s[pair, :, hcols]
                sink = sink_ref[kvh * 2 * pairs + 2 * pair + hh]
                m = jnp.maximum(jnp.max(lg, axis=-1, keepdims=True), sink)
                p_s[pair, :, hcols] = jnp.exp(lg - m).astype(BF16)
                sink_terms.append(jnp.exp(sink - m))
            st_s[pair] = jnp.where(lo_w, sink_terms[0], sink_terms[1])

    def output_stage(n):
        qrows = pl.ds(pl.multiple_of(n * w, w), w)
        band = pl.ds(pl.multiple_of(n * w, w), 3 * w)
        vx = jnp.concatenate([va_s[band, :], vb_s[band, :]], axis=0)
        for pair in range(pairs):
            ox = _dot(p_s[pair], vx)
            o = ox[:, :LANES] / (ox[:, LANES:] + st_s[pair])
            o_ref[qrows, pair * LANES:(pair + 1) * LANES] = o.astype(o_ref.dtype)

    logits_stage(0, 0)
    softmax_stage()
    logits_stage(1, 1)

    def steady(n, carry):
        output_stage(n - 1)
        softmax_stage()
        logits_stage(n + 1, 1)
        return carry

    lax.fori_loop(1, nb - 2, steady, 0, unroll=2)
    output_stage(nb - 3)
    softmax_stage()
    logits_stage(nb - 1, 2)
    output_stage(nb - 2)
    softmax_stage()
    output_stage(nb - 1)


def _swa(q, kv, bias, sink, batch, seq):
    t = batch * seq
    w = SWA_BLOCK
    qw = Q_SWA_W // SWA_KV_HEADS
    pairs = SWA_Q_HEADS // SWA_KV_HEADS // 2
    staged_k = pltpu.VMEM((seq + 2 * w, LANES), BF16)
    staged_v = pltpu.VMEM((seq + 2 * w, 2 * LANES), BF16)
    return pl.pallas_call(
        functools.partial(_swa_kernel, seq=seq),
        grid=(batch, SWA_KV_HEADS),
        in_specs=[pl.BlockSpec((seq, qw), lambda b, j: (b, j)),
                  pl.BlockSpec((seq, LANES), lambda b, j: (b, j)),
                  pl.BlockSpec((3, pairs, w, 6 * w), lambda b, j: (0, j, 0, 0)),
                  pl.BlockSpec(memory_space=pltpu.SMEM)],
        out_specs=pl.BlockSpec((seq, qw), lambda b, j: (b, j)),
        out_shape=jax.ShapeDtypeStruct((t, Q_SWA_W), BF16),
        scratch_shapes=[staged_k, staged_k, staged_v, staged_v,
                        pltpu.VMEM((pairs, w, 6 * w), F32),
                        pltpu.VMEM((pairs, w, 6 * w), BF16),
                        pltpu.VMEM((pairs, w, LANES), F32)],
        compiler_params=pltpu.CompilerParams(
            dimension_semantics=("arbitrary", "arbitrary"), vmem_limit_bytes=VMEM_LIMIT),
        name="swa",
    )(q, kv, bias, sink)


def _ffn_kernel(x_ref, oa_ref, os_ref, wa_ref, ws_ref, gpost_ref, gpre_ref, wup_ref, wdn_ref,
                gout_ref, o_ref):
    mix = _dot(oa_ref[...], wa_ref[...]) + _dot(os_ref[...], ws_ref[...])
    h = x_ref[...] + _rms(mix) * gpost_ref[...]
    hn = (_rms(h) * gpre_ref[...]).astype(BF16)
    acc = jnp.zeros(h.shape, F32)
    for j in range(D_FF // FF_BLOCK):
        cols = slice(j * FF_BLOCK, (j + 1) * FF_BLOCK)
        zj = jnp.maximum(_dot(hn, wup_ref[:, cols]), 0.0)
        acc = acc + _dot((zj * zj).astype(BF16), wdn_ref[cols, :])
    o_ref[...] = h + _rms(acc) * gout_ref[...]


def _ffn(x2, oa, os_, wa, ws, gpost, gpre, wup, wdn, gout):
    t = x2.shape[0]
    row = lambda width: pl.BlockSpec((TOKEN_BLOCK, width), lambda i: (i, 0))
    return pl.pallas_call(
        _ffn_kernel,
        grid=(t // TOKEN_BLOCK,),
        in_specs=[row(D_MODEL), row(V_GLA_W), row(Q_SWA_W),
                  _const_spec((V_GLA_W, D_MODEL)), _const_spec((Q_SWA_W, D_MODEL)),
                  _const_spec((1, D_MODEL)), _const_spec((1, D_MODEL)),
                  _const_spec((D_MODEL, D_FF)), _const_spec((D_FF, D_MODEL)),
                  _const_spec((1, D_MODEL))],
        out_specs=row(D_MODEL),
        out_shape=jax.ShapeDtypeStruct((t, D_MODEL), F32),
        compiler_params=pltpu.CompilerParams(
            dimension_semantics=("arbitrary",), vmem_limit_bytes=VMEM_LIMIT),
        name="outproj_ffn",
    )(x2, oa, os_, wa, ws, gpost, gpre, wup, wdn, gout)


def _regroup_w_in(w_in):
    sizes = (GLA_HEADS * GLA_DK, GLA_HEADS * GLA_DK, V_GLA_W, G_GLA_W, Z_W,
             Q_SWA_W, SWA_KV_HEADS * SWA_DH, SWA_KV_HEADS * SWA_DH)
    starts = np.concatenate([[0], np.cumsum(sizes)[:-1]])
    qa, ka, va, ga, za, qs, ks, vs = (w_in[:, s:s + n] for s, n in zip(starts, sizes))
    cols = []
    for h in range(GLA_HEADS):
        sl = slice(h * GLA_DK, (h + 1) * GLA_DK)
        cols += [qa[:, sl] * GLA_DK ** -0.5, ka[:, sl]]
    cols += [va, ga, qs * SWA_DH ** -0.5]
    for j in range(SWA_KV_HEADS):
        sl = slice(j * SWA_DH, (j + 1) * SWA_DH)
        cols += [ks[:, sl], vs[:, sl]]
    cols += [za]
    return jnp.concatenate(cols, axis=1).astype(BF16)


def _layer(h, norm_mix_pre, w_in, w_gu_f, b_g_f, w_gu_b, b_g_b, gla_norm, swa_sink, bias_tab,
           w_out, norm_mix_post, norm_mlp_pre, w_up, w_down, norm_mlp_post):
    batch, seq, _ = h.shape
    x2 = h.reshape(batch * seq, D_MODEL)
    w_in_r = _regroup_w_in(w_in)
    row = lambda g: g.reshape(1, -1).astype(F32)

    qk, v, g, qs, kv, z = _inproj(x2, row(norm_mix_pre), w_in_r)

    wf = w_gu_f.reshape(GLA_RANK, GLA_HEADS, GLA_DK).transpose(1, 0, 2)
    wb = w_gu_b.reshape(GLA_RANK, GLA_HEADS, GLA_DK).transpose(1, 0, 2)
    zeros = jnp.zeros_like(wf)
    wgu = jnp.concatenate([jnp.concatenate([wf, zeros], axis=2),
                           jnp.concatenate([zeros, wb], axis=2)], axis=1).astype(BF16)
    bg = jnp.concatenate([b_g_f.reshape(GLA_HEADS, 1, GLA_DK),
                          b_g_b.reshape(GLA_HEADS, 1, GLA_DK)], axis=2).astype(F32)
    o_a = _gla(qk, v, g, z, wgu, bg, row(gla_norm), batch, seq)

    o_s = _swa(qs, kv, bias_tab, swa_sink.astype(F32), batch, seq)

    out = _ffn(x2, o_a, o_s, w_out[:V_GLA_W].astype(BF16), w_out[V_GLA_W:].astype(BF16),
               row(norm_mix_post), row(norm_mlp_pre), w_up.astype(BF16), w_down.astype(BF16),
               row(norm_mlp_post))
    return out.reshape(batch, seq, D_MODEL)


def kernel(x, norm_mix_pre, w_in, w_gate_up_fwd, b_gate_fwd, w_gate_up_bwd, b_gate_bwd, gla_norm,
           swa_sink, rel_bias, w_out, norm_mix_post, norm_mlp_pre, w_up, w_down, norm_mlp_post):
    bias_tab = _bias_tables(rel_bias)
    h = x
    for l in range(w_in.shape[0]):
        h = _layer(h, norm_mix_pre[l], w_in[l], w_gate_up_fwd[l], b_gate_fwd[l], w_gate_up_bwd[l],
                   b_gate_bwd[l], gla_norm[l], swa_sink[l], bias_tab, w_out[l], norm_mix_post[l],
                   norm_mlp_pre[l], w_up[l], w_down[l], norm_mlp_post[l])
    return h
```

```python
import functools
import math

import jax
import jax.numpy as jnp
import numpy as np
from jax import lax
from jax.experimental import pallas as pl
from jax.experimental.pallas import tpu as pltpu

F32 = jnp.float32
BF16 = jnp.bfloat16

D_MODEL = 1024
GLA_HEADS = 4
GLA_DK = 64
GLA_DV = 128
GLA_RANK = 16
GLA_GATE_NORM = 16.0
GLA_CHUNK = 64
SWA_DH = 64
SWA_Q_HEADS = 8
SWA_KV_HEADS = 2
SWA_BLOCK = 128
SWA_WINDOW = 128
REL_BUCKETS = 32
REL_MAX_DIST = 128
D_FF = 4 * D_MODEL
NORM_EPS = 1e-6
MASK_VALUE = -1e30

LANES = 128
HALF = LANES // 2

QK_GLA_W = GLA_HEADS * 2 * GLA_DK
V_GLA_W = GLA_HEADS * GLA_DV
G_GLA_W = GLA_HEADS * GLA_DV
Q_SWA_W = SWA_Q_HEADS * SWA_DH
KV_SWA_W = SWA_KV_HEADS * 2 * SWA_DH
Z_W = 2 * GLA_RANK
IN_COLS = QK_GLA_W + V_GLA_W + G_GLA_W + Q_SWA_W + KV_SWA_W + Z_W

GLA_UNROLL = 8
GLA_GROUP = 256
TOKEN_BLOCK = 512
FF_BLOCK = 1024
VMEM_LIMIT = 56 * 1024 * 1024


def _rms(x):
    return x * lax.rsqrt(jnp.mean(x * x, axis=-1, keepdims=True) + NORM_EPS)


def _dot(a, b):
    return jnp.dot(a, b, preferred_element_type=F32)


def _dot_nt(a, b):
    return lax.dot_general(a, b, (((1,), (1,)), ((), ())), preferred_element_type=F32)


def _dot_tn(a, b):
    return lax.dot_general(a, b, (((0,), (0,)), ((), ())), preferred_element_type=F32)


def _const_spec(shape):
    nd = len(shape)
    return pl.BlockSpec(shape, lambda *_: (0,) * nd, pipeline_mode=pl.Buffered(1))


def _rows(item, size):
    start = item * size
    if not isinstance(start, int):
        start = pl.multiple_of(start, size)
    return pl.ds(start, size)


def _software_pipeline(n_items, stages, unroll=1):
    depth = len(stages)

    def trip(t, static):
        for k in reversed(range(depth)):
            if not static or 0 <= t - k < n_items:
                stages[k](t - k)

    for t in range(depth - 1):
        trip(t, True)

    def steady(t, carry):
        trip(t, False)
        return carry

    lax.fori_loop(depth - 1, n_items, steady, 0, unroll=unroll)
    for t in range(n_items, n_items + depth - 1):
        trip(t, True)


def _inproj_kernel(x_ref, g_ref, w_ref, qk_ref, v_ref, gg_ref, qs_ref, kv_ref, z_ref):
    u = (_rms(x_ref[...]) * g_ref[...]).astype(BF16)
    off = 0
    for ref, width in ((qk_ref, QK_GLA_W), (v_ref, V_GLA_W), (gg_ref, G_GLA_W),
                       (qs_ref, Q_SWA_W), (kv_ref, KV_SWA_W), (z_ref, Z_W)):
        ref[...] = _dot(u, w_ref[:, off:off + width]).astype(ref.dtype)
        off += width


def _inproj(x2, g, w):
    t = x2.shape[0]
    row = lambda width: pl.BlockSpec((TOKEN_BLOCK, width), lambda i: (i, 0))
    widths = (QK_GLA_W, V_GLA_W, G_GLA_W, Q_SWA_W, KV_SWA_W, Z_W)
    dtypes = (BF16, BF16, BF16, BF16, BF16, F32)
    return pl.pallas_call(
        _inproj_kernel,
        grid=(t // TOKEN_BLOCK,),
        in_specs=[row(D_MODEL), _const_spec((1, D_MODEL)), _const_spec((D_MODEL, IN_COLS))],
        out_specs=[row(wd) for wd in widths],
        out_shape=[jax.ShapeDtypeStruct((t, wd), dt) for wd, dt in zip(widths, dtypes)],
        compiler_params=pltpu.CompilerParams(
            dimension_semantics=("arbitrary",), vmem_limit_bytes=VMEM_LIMIT),
        name="inproj",
    )(x2, g, w)


def _cumsum_matrix():
    r = np.arange(GLA_GROUP)[:, None]
    s = np.arange(GLA_GROUP)[None, :]
    return (((r // GLA_CHUNK) == (s // GLA_CHUNK)) & (s <= r)).astype(np.float32)


def _gla_kernel(qk_ref, v_ref, g_ref, z_ref, wgu_ref, bg_ref, nrm_ref, cm_ref, o_ref,
                la_s, hl_s, cs_s, qdec_s, kina_s, kinb_s, kst_s, am_s, ds_s, dec_s, sc_s, *, seq):
    c = GLA_CHUNK
    nc = seq // c
    grp = GLA_GROUP
    cpg = grp // c
    n_groups = seq // grp
    wgu = wgu_ref[0]
    bg = bg_ref[0]
    fwd_g = lax.broadcasted_iota(jnp.int32, (grp, LANES), 1) < HALF
    fwd_3 = lax.broadcasted_iota(jnp.int32, (cpg, c, LANES), 2) < HALF
    zero_g = jnp.zeros((grp, LANES), F32)
    lane_c = lax.broadcasted_iota(jnp.int32, (c, LANES), 1)
    row_c = lax.broadcasted_iota(jnp.int32, (c, LANES), 0)
    tri = ((lane_c < HALF) & (lane_c <= row_c)) | ((lane_c >= HALF) & ((lane_c - HALF) >= row_c))
    zero_c = jnp.zeros((c, LANES), F32)

    def gate_preact(i):
        rows = _rows(i, grp)
        la_s[rows, :] = _dot(z_ref[rows, :].astype(BF16), wgu)

    def decay_logs(i):
        rows = _rows(i, grp)
        pre = la_s[rows, :] + bg
        la = ((jnp.minimum(pre, 0.0) - jnp.log(1.0 + jnp.exp(-jnp.abs(pre))))
              * (1.0 / GLA_GATE_NORM))
        la_hi = la.astype(BF16)
        la_s[rows, :] = la
        hl_s[rows, 0:LANES] = la_hi
        hl_s[rows, LANES:2 * LANES] = (la - la_hi.astype(F32)).astype(BF16)

    def chunk_cumsum(i):
        rows = _rows(i, grp)
        cs_s[rows, :] = _dot(cm_ref[...], hl_s[rows, :])

    def decayed_qk(i):
        rows = _rows(i, grp)
        cs = cs_s[rows, :]
        la3 = la_s[rows, :].reshape(cpg, c, LANES)
        cum3 = (cs[:, :LANES] + cs[:, LANES:]).reshape(cpg, c, LANES)
        tot = jnp.sum(la3, axis=1, keepdims=True)
        b3 = jnp.where(fwd_3, cum3, tot - cum3 + la3)
        b = b3.reshape(grp, LANES)
        rest = jnp.exp(tot - b3).reshape(grp, LANES)
        x = qk_ref[rows, :].astype(F32)
        xr = pltpu.roll(x, HALF, axis=1)
        qq = jnp.where(fwd_g, x, xr)
        kk = jnp.where(fwd_g, xr, x)
        kin = kk * jnp.exp(-b)
        qdec_s[rows, :] = (qq * jnp.exp(b)).astype(BF16)
        kina_s[rows, :] = jnp.where(fwd_g, kin, zero_g).astype(BF16)
        kinb_s[rows, :] = jnp.where(fwd_g, zero_g, kin).astype(BF16)
        kst_s[rows, :] = (kk * rest).astype(BF16)
        dec_s[_rows(i, cpg)] = jnp.broadcast_to(jnp.exp(tot), (cpg, 8, LANES))

    def chunk_products(i):
        for cc in range(cpg):
            n = i * cpg + cc
            rows = _rows(n, c)
            kin_bd = jnp.concatenate([kina_s[rows, :], kinb_s[rows, :]], axis=0)
            a2 = _dot_nt(qdec_s[rows, :], kin_bd)
            am_s[rows, :] = jnp.where(tri, a2, zero_c).astype(BF16)
            ds_s[n] = _dot_tn(v_ref[rows, :], kst_s[rows, :])

    _software_pipeline(n_groups, [gate_preact, decay_logs, chunk_cumsum, decayed_qk,
                                  chunk_products], unroll=4)

    fwd_sq = lax.broadcasted_iota(jnp.int32, (LANES, LANES), 1) < HALF
    fwd_row = lax.broadcasted_iota(jnp.int32, (8, LANES), 1) < HALF

    def scan_step(i, s):
        j = nc - 1 - i
        sb = s.astype(BF16)
        sc_s[i, :, 0:HALF] = sb[:, 0:HALF]
        sc_s[j, :, HALF:LANES] = sb[:, HALF:LANES]
        dec = jnp.where(fwd_row, dec_s[i], dec_s[j])[0:1]
        return dec * s + jnp.where(fwd_sq, ds_s[i], ds_s[j])

    lax.fori_loop(0, nc, scan_step, jnp.zeros((LANES, LANES), F32), unroll=GLA_UNROLL)

    nrm = nrm_ref[...]

    def mix(i):
        for cc in range(cpg):
            n = i * cpg + cc
            rows = _rows(n, c)
            vc = v_ref[rows, :]
            la_s[rows, :] = (_dot_nt(qdec_s[rows, :], sc_s[n])
                             + _dot(am_s[rows, :], jnp.concatenate([vc, vc], axis=0)))

    def norm_gate(i):
        rows = _rows(i, grp)
        gate = g_ref[rows, :].astype(F32)
        o = (_rms(la_s[rows, :]) * nrm) * (gate / (1.0 + jnp.exp(-gate)))
        o_ref[rows, :] = o.astype(o_ref.dtype)

    _software_pipeline(n_groups, [mix, norm_gate], unroll=5)


def _gla(qk, v, g, z, wgu, bg, nrm, batch, seq):
    t = batch * seq
    nc = seq // GLA_CHUNK
    tile = pl.BlockSpec((seq, LANES), lambda b, h: (b, h))
    tok_bf = pltpu.VMEM((seq, LANES), BF16)
    cm = jnp.asarray(_cumsum_matrix(), dtype=BF16)
    return pl.pallas_call(
        functools.partial(_gla_kernel, seq=seq),
        grid=(batch, GLA_HEADS),
        in_specs=[tile, tile, tile,
                  pl.BlockSpec((seq, Z_W), lambda b, h: (b, 0)),
                  pl.BlockSpec((1, Z_W, LANES), lambda b, h: (h, 0, 0)),
                  pl.BlockSpec((1, 1, LANES), lambda b, h: (h, 0, 0)),
                  pl.BlockSpec((1, LANES), lambda b, h: (0, 0)),
                  _const_spec(cm.shape)],
        out_specs=tile,
        out_shape=jax.ShapeDtypeStruct((t, V_GLA_W), BF16),
        scratch_shapes=[pltpu.VMEM((seq, LANES), F32),
                        pltpu.VMEM((seq, 2 * LANES), BF16),
                        pltpu.VMEM((seq, 2 * LANES), F32),
                        tok_bf, tok_bf, tok_bf, tok_bf, tok_bf,
                        pltpu.VMEM((nc, LANES, LANES), F32),
                        pltpu.VMEM((nc, 8, LANES), F32),
                        pltpu.VMEM((nc, LANES, LANES), BF16)],
        compiler_params=pltpu.CompilerParams(
            dimension_semantics=("arbitrary", "arbitrary"), vmem_limit_bytes=VMEM_LIMIT),
        name="gla",
    )(qk, v, g, z, wgu, bg, nrm, cm)


def _t5_buckets(rel):
    nb = REL_BUCKETS // 2
    ret = (rel > 0).astype(np.int32) * nb
    n = np.abs(rel)
    max_exact = nb // 2
    large = max_exact + (np.log(np.maximum(n, 1).astype(np.float32) / max_exact)
                         / math.log(REL_MAX_DIST / max_exact) * (nb - max_exact)).astype(np.int32)
    large = np.minimum(large, nb - 1)
    return ret + np.where(n < max_exact, n, large)


def _bucket_maps():
    w = SWA_BLOCK
    cq = np.arange(w)[:, None]
    s = np.arange(3 * w)[None, :]
    rel = s - w - cq
    band = np.abs(rel) <= SWA_WINDOW
    buckets = _t5_buckets(rel).astype(np.int32)
    first = band & (s >= w)
    last = band & (s < 2 * w)
    return np.stack([np.where(m, buckets, -1) for m in (first, band, last)]).astype(np.int32)


def _bias_kernel(bm_ref, tab_ref, o_ref):
    pair = pl.program_id(1)
    bm = bm_ref[0]
    for hh in range(2):
        head = 2 * pair + hh
        acc = jnp.full(bm.shape, MASK_VALUE, F32)
        for bkt in range(REL_BUCKETS):
            acc = jnp.where(bm == bkt, tab_ref[bkt, head], acc)
        o_ref[0, 0, :, hh * 3 * SWA_BLOCK:(hh + 1) * 3 * SWA_BLOCK] = acc


def _bias_tables(rel_bias):
    w = SWA_BLOCK
    pairs = SWA_Q_HEADS // 2
    return pl.pallas_call(
        _bias_kernel,
        grid=(3, pairs),
        in_specs=[pl.BlockSpec((1, w, 3 * w), lambda v, p: (v, 0, 0)),
                  pl.BlockSpec(memory_space=pltpu.SMEM)],
        out_specs=pl.BlockSpec((1, 1, w, 6 * w), lambda v, p: (v, p, 0, 0)),
        out_shape=jax.ShapeDtypeStruct((3, pairs, w, 6 * w), F32),
        compiler_params=pltpu.CompilerParams(dimension_semantics=("arbitrary", "arbitrary")),
        name="swa_bias",
    )(jnp.asarray(_bucket_maps()), rel_bias.astype(F32))


def _swa_kernel(q_ref, kv_ref, bias_ref, sink_ref, o_ref, ka_s, kb_s, va_s, vb_s,
                lg_s, p_s, st_s, *, seq):
    w = SWA_BLOCK
    nb = seq // w
    kvh = pl.program_id(1)
    pairs = SWA_Q_HEADS // SWA_KV_HEADS // 2
    stage_rows = 512
    lo = lax.broadcasted_iota(jnp.int32, (stage_rows, LANES), 1) < HALF
    for s_ref in (ka_s, kb_s, va_s, vb_s):
        zpad = jnp.zeros((w, s_ref.shape[1]), BF16)
        s_ref[0:w, :] = zpad
        s_ref[w + seq:2 * w + seq, :] = zpad

    def stage(i, carry):
        src = pl.ds(pl.multiple_of(i * stage_rows, stage_rows), stage_rows)
        dst = pl.ds(pl.multiple_of(i * stage_rows + w, w), stage_rows)
        kv = kv_ref[src, :].astype(F32)
        vk = pltpu.roll(kv, HALF, axis=1)
        zero = jnp.zeros_like(kv)
        one = jnp.ones_like(kv)
        ka_s[dst, :] = jnp.where(lo, kv, zero).astype(BF16)
        kb_s[dst, :] = jnp.where(lo, zero, vk).astype(BF16)
        va_s[dst, 0:LANES] = jnp.where(lo, vk, zero).astype(BF16)
        va_s[dst, LANES:2 * LANES] = jnp.where(lo, one, zero).astype(BF16)
        vb_s[dst, 0:LANES] = jnp.where(lo, zero, kv).astype(BF16)
        vb_s[dst, LANES:2 * LANES] = jnp.where(lo, zero, one).astype(BF16)
        return carry

    lax.fori_loop(0, seq // stage_rows, stage, 0)

    lo_w = lax.broadcasted_iota(jnp.int32, (w, LANES), 1) < HALF

    def logits_stage(n, variant):
        qrows = pl.ds(pl.multiple_of(n * w, w), w)
        band = pl.ds(pl.multiple_of(n * w, w), 3 * w)
        kt = jnp.concatenate([ka_s[band, :], kb_s[band, :]], axis=0)
        for pair in range(pairs):
            cols = slice(pair * LANES, (pair + 1) * LANES)
            lg_s[pair] = _dot_nt(q_ref[qrows, cols], kt) + bias_ref[variant, pair]

    def softmax_stage():
        for pair in range(pairs):
            sink_terms = []
            for hh in range(2):
                hcols = slice(hh * 3 * w, (hh + 1) * 3 * w)
                lg = lg_s[pair, :, hcols]
                sink = sink_ref[kvh * 2 * pairs + 2 * pair + hh]
                m = jnp.maximum(jnp.max(lg, axis=-1, keepdims=True), sink)
                p_s[pair, :, hcols] = jnp.exp(lg - m).astype(BF16)
                sink_terms.append(jnp.exp(sink - m))
            st_s[pair] = jnp.where(lo_w, sink_terms[0], sink_terms[1])

    def output_stage(n):
        qrows = pl.ds(pl.multiple_of(n * w, w), w)
        band = pl.ds(pl.multiple_of(n * w, w), 3 * w)
        vx = jnp.concatenate([va_s[band, :], vb_s[band, :]], axis=0)
        for pair in range(pairs):
            ox = _dot(p_s[pair], vx)
            o = ox[:, :LANES] / (ox[:, LANES:] + st_s[pair])
            o_ref[qrows, pair * LANES:(pair + 1) * LANES] = o.astype(o_ref.dtype)

    logits_stage(0, 0)
    softmax_stage()
    logits_stage(1, 1)

    def steady(n, carry):
        output_stage(n - 1)
        softmax_stage()
        logits_stage(n + 1, 1)
        return carry

    lax.fori_loop(1, nb - 2, steady, 0, unroll=2)
    output_stage(nb - 3)
    softmax_stage()
    logits_stage(nb - 1, 2)
    output_stage(nb - 2)
    softmax_stage()
    output_stage(nb - 1)


def _swa(q, kv, bias, sink, batch, seq):
    t = batch * seq
    w = SWA_BLOCK
    qw = Q_SWA_W // SWA_KV_HEADS
    pairs = SWA_Q_HEADS // SWA_KV_HEADS // 2
    staged_k = pltpu.VMEM((seq + 2 * w, LANES), BF16)
    staged_v = pltpu.VMEM((seq + 2 * w, 2 * LANES), BF16)
    return pl.pallas_call(
        functools.partial(_swa_kernel, seq=seq),
        grid=(batch, SWA_KV_HEADS),
        in_specs=[pl.BlockSpec((seq, qw), lambda b, j: (b, j)),
                  pl.BlockSpec((seq, LANES), lambda b, j: (b, j)),
                  pl.BlockSpec((3, pairs, w, 6 * w), lambda b, j: (0, j, 0, 0)),
                  pl.BlockSpec(memory_space=pltpu.SMEM)],
        out_specs=pl.BlockSpec((seq, qw), lambda b, j: (b, j)),
        out_shape=jax.ShapeDtypeStruct((t, Q_SWA_W), BF16),
        scratch_shapes=[staged_k, staged_k, staged_v, staged_v,
                        pltpu.VMEM((pairs, w, 6 * w), F32),
                        pltpu.VMEM((pairs, w, 6 * w), BF16),
                        pltpu.VMEM((pairs, w, LANES), F32)],
        compiler_params=pltpu.CompilerParams(
            dimension_semantics=("arbitrary", "arbitrary"), vmem_limit_bytes=VMEM_LIMIT),
        name="swa",
    )(q, kv, bias, sink)


def _ffn_kernel(x_ref, oa_ref, os_ref, wa_ref, ws_ref, gpost_ref, gpre_ref, wup_ref, wdn_ref,
                gout_ref, o_ref):
    mix = _dot(oa_ref[...], wa_ref[...]) + _dot(os_ref[...], ws_ref[...])
    h = x_ref[...] + _rms(mix) * gpost_ref[...]
    hn = (_rms(h) * gpre_ref[...]).astype(BF16)
    acc = jnp.zeros(h.shape, F32)
    for j in range(D_FF // FF_BLOCK):
        cols = slice(j * FF_BLOCK, (j + 1) * FF_BLOCK)
        zj = jnp.maximum(_dot(hn, wup_ref[:, cols]), 0.0)
        acc = acc + _dot((zj * zj).astype(BF16), wdn_ref[cols, :])
    o_ref[...] = h + _rms(acc) * gout_ref[...]


def _ffn(x2, oa, os_, wa, ws, gpost, gpre, wup, wdn, gout):
    t = x2.shape[0]
    row = lambda width: pl.BlockSpec((TOKEN_BLOCK, width), lambda i: (i, 0))
    return pl.pallas_call(
        _ffn_kernel,
        grid=(t // TOKEN_BLOCK,),
        in_specs=[row(D_MODEL), row(V_GLA_W), row(Q_SWA_W),
                  _const_spec((V_GLA_W, D_MODEL)), _const_spec((Q_SWA_W, D_MODEL)),
                  _const_spec((1, D_MODEL)), _const_spec((1, D_MODEL)),
                  _const_spec((D_MODEL, D_FF)), _const_spec((D_FF, D_MODEL)),
                  _const_spec((1, D_MODEL))],
        out_specs=row(D_MODEL),
        out_shape=jax.ShapeDtypeStruct((t, D_MODEL), F32),
        compiler_params=pltpu.CompilerParams(
            dimension_semantics=("arbitrary",), vmem_limit_bytes=VMEM_LIMIT),
        name="outproj_ffn",
    )(x2, oa, os_, wa, ws, gpost, gpre, wup, wdn, gout)


def _regroup_w_in(w_in):
    sizes = (GLA_HEADS * GLA_DK, GLA_HEADS * GLA_DK, V_GLA_W, G_GLA_W, Z_W,
             Q_SWA_W, SWA_KV_HEADS * SWA_DH, SWA_KV_HEADS * SWA_DH)
    starts = np.concatenate([[0], np.cumsum(sizes)[:-1]])
    qa, ka, va, ga, za, qs, ks, vs = (w_in[:, s:s + n] for s, n in zip(starts, sizes))
    cols = []
    for h in range(GLA_HEADS):
        sl = slice(h * GLA_DK, (h + 1) * GLA_DK)
        cols += [qa[:, sl] * GLA_DK ** -0.5, ka[:, sl]]
    cols += [va, ga, qs * SWA_DH ** -0.5]
    for j in range(SWA_KV_HEADS):
        sl = slice(j * SWA_DH, (j + 1) * SWA_DH)
        cols += [ks[:, sl], vs[:, sl]]
    cols += [za]
    return jnp.concatenate(cols, axis=1).astype(BF16)


def _layer(h, norm_mix_pre, w_in, w_gu_f, b_g_f, w_gu_b, b_g_b, gla_norm, swa_sink, bias_tab,
           w_out, norm_mix_post, norm_mlp_pre, w_up, w_down, norm_mlp_post):
    batch, seq, _ = h.shape
    x2 = h.reshape(batch * seq, D_MODEL)
    w_in_r = _regroup_w_in(w_in)
    row = lambda g: g.reshape(1, -1).astype(F32)

    qk, v, g, qs, kv, z = _inproj(x2, row(norm_mix_pre), w_in_r)

    wf = w_gu_f.reshape(GLA_RANK, GLA_HEADS, GLA_DK).transpose(1, 0, 2)
    wb = w_gu_b.reshape(GLA_RANK, GLA_HEADS, GLA_DK).transpose(1, 0, 2)
    zeros = jnp.zeros_like(wf)
    wgu = jnp.concatenate([jnp.concatenate([wf, zeros], axis=2),
                           jnp.concatenate([zeros, wb], axis=2)], axis=1).astype(BF16)
    bg = jnp.concatenate([b_g_f.reshape(GLA_HEADS, 1, GLA_DK),
                          b_g_b.reshape(GLA_HEADS, 1, GLA_DK)], axis=2).astype(F32)
    o_a = _gla(qk, v, g, z, wgu, bg, row(gla_norm), batch, seq)

    o_s = _swa(qs, kv, bias_tab, swa_sink.astype(F32), batch, seq)

    out = _ffn(x2, o_a, o_s, w_out[:V_GLA_W].astype(BF16), w_out[V_GLA_W:].astype(BF16),
               row(norm_mix_post), row(norm_mlp_pre), w_up.astype(BF16), w_down.astype(BF16),
               row(norm_mlp_post))
    return out.reshape(batch, seq, D_MODEL)


def kernel(x, norm_mix_pre, w_in, w_gate_up_fwd, b_gate_fwd, w_gate_up_bwd, b_gate_bwd, gla_norm,
           swa_sink, rel_bias, w_out, norm_mix_post, norm_mlp_pre, w_up, w_down, norm_mlp_post):
    bias_tab = _bias_tables(rel_bias)
    h = x
    for l in range(w_in.shape[0]):
        h = _layer(h, norm_mix_pre[l], w_in[l], w_gate_up_fwd[l], b_gate_fwd[l], w_gate_up_bwd[l],
                   b_gate_bwd[l], gla_norm[l], swa_sink[l], bias_tab, w_out[l], norm_mix_post[l],
                   norm_mlp_pre[l], w_up[l], w_down[l], norm_mlp_post[l])
    return h
```

```python
import functools
import math

import jax
import jax.numpy as jnp
import numpy as np
from jax import lax
from jax.experimental import pallas as pl
from jax.experimental.pallas import tpu as pltpu

F32 = jnp.float32
BF16 = jnp.bfloat16

D_MODEL = 1024
GLA_HEADS = 4
GLA_DK = 64
GLA_DV = 128
GLA_RANK = 16
GLA_GATE_NORM = 16.0
GLA_CHUNK = 64
SWA_DH = 64
SWA_Q_HEADS = 8
SWA_KV_HEADS = 2
SWA_BLOCK = 128
SWA_WINDOW = 128
REL_BUCKETS = 32
REL_MAX_DIST = 128
D_FF = 4 * D_MODEL
NORM_EPS = 1e-6
MASK_VALUE = -1e30

LANES = 128
HALF = LANES // 2

QK_GLA_W = GLA_HEADS * GLA_DK
V_GLA_W = GLA_HEADS * GLA_DV
G_GLA_W = GLA_HEADS * GLA_DV
Q_SWA_W = SWA_Q_HEADS * SWA_DH
KV_SWA_W = SWA_KV_HEADS * SWA_DH
Z_W = 2 * GLA_RANK
IN_WIDTHS = (QK_GLA_W, QK_GLA_W, V_GLA_W, G_GLA_W, Q_SWA_W, KV_SWA_W, KV_SWA_W, Z_W)
IN_COLS = sum(IN_WIDTHS)
IN_DOT_GROUPS = ((0, 1), (2,), (3,), (4,), (5, 6), (7,))

GLA_UNROLL = 8
GLA_GROUP = 256
TOKEN_BLOCK = 512
FF_BLOCK = 1024
FFN_SLABS = 2
VMEM_LIMIT = 56 * 1024 * 1024


def _rms(x):
    return x * lax.rsqrt(jnp.mean(x * x, axis=-1, keepdims=True) + NORM_EPS)


def _dot(a, b):
    return jnp.dot(a, b, preferred_element_type=F32)


def _dot_nt(a, b):
    return lax.dot_general(a, b, (((1,), (1,)), ((), ())), preferred_element_type=F32)


def _dot_tn(a, b):
    return lax.dot_general(a, b, (((0,), (0,)), ((), ())), preferred_element_type=F32)


def _const_spec(shape):
    nd = len(shape)
    return pl.BlockSpec(shape, lambda *_: (0,) * nd, pipeline_mode=pl.Buffered(1))


def _rows(item, size):
    start = item * size
    if not isinstance(start, int):
        start = pl.multiple_of(start, size)
    return pl.ds(start, size)


def _software_pipeline(n_items, stages, unroll=1):
    depth = len(stages)

    def trip(t, static):
        for k in reversed(range(depth)):
            if not static or 0 <= t - k < n_items:
                stages[k](t - k)

    for t in range(depth - 1):
        trip(t, True)

    def steady(t, carry):
        trip(t, False)
        return carry

    lax.fori_loop(depth - 1, n_items, steady, 0, unroll=unroll)
    for t in range(n_items, n_items + depth - 1):
        trip(t, True)


def _inproj_kernel(x_ref, g_ref, w_ref, *out_refs):
    u = (_rms(x_ref[...]) * g_ref[...]).astype(BF16)
    off = 0
    for group in IN_DOT_GROUPS:
        width = sum(IN_WIDTHS[i] for i in group)
        y = _dot(u, w_ref[:, off:off + width])
        off += width
        col = 0
        for i in group:
            out_refs[i][...] = y[:, col:col + IN_WIDTHS[i]].astype(out_refs[i].dtype)
            col += IN_WIDTHS[i]


def _inproj(x2, g, w):
    t = x2.shape[0]
    row = lambda width: pl.BlockSpec((TOKEN_BLOCK, width), lambda i: (i, 0))
    widths = IN_WIDTHS
    dtypes = (BF16,) * (len(IN_WIDTHS) - 1) + (F32,)
    return pl.pallas_call(
        _inproj_kernel,
        grid=(t // TOKEN_BLOCK,),
        in_specs=[row(D_MODEL), _const_spec((1, D_MODEL)), _const_spec((D_MODEL, IN_COLS))],
        out_specs=[row(wd) for wd in widths],
        out_shape=[jax.ShapeDtypeStruct((t, wd), dt) for wd, dt in zip(widths, dtypes)],
        compiler_params=pltpu.CompilerParams(
            dimension_semantics=("arbitrary",), vmem_limit_bytes=VMEM_LIMIT),
        name="inproj",
    )(x2, g, w)


def _cumsum_matrix():
    r = np.arange(GLA_GROUP)[:, None]
    s = np.arange(GLA_GROUP)[None, :]
    return (((r // GLA_CHUNK) == (s // GLA_CHUNK)) & (s <= r)).astype(np.float32)


def _gla_kernel(q_ref, k_ref, v_ref, g_ref, z_ref, wgu_ref, bg_ref, nrm_ref, cm_ref, o_ref,
                la_s, hl_s, cs_s, qdec_s, kina_s, kinb_s, kst_s, am_s, ds_s, dec_s, sc_s, *, seq):
    c = GLA_CHUNK
    nc = seq // c
    grp = GLA_GROUP
    cpg = grp // c
    n_groups = seq // grp
    wgu = wgu_ref[0]
    bg = bg_ref[0]
    lane_g = lax.broadcasted_iota(jnp.int32, (grp, LANES), 1)
    fwd_g = lane_g < HALF
    own_g = (lane_g // HALF) == (pl.program_id(1) % 2)
    fwd_3 = lax.broadcasted_iota(jnp.int32, (cpg, c, LANES), 2) < HALF
    zero_g = jnp.zeros((grp, LANES), F32)
    lane_c = lax.broadcasted_iota(jnp.int32, (c, LANES), 1)
    row_c = lax.broadcasted_iota(jnp.int32, (c, LANES), 0)
    tri = ((lane_c < HALF) & (lane_c <= row_c)) | ((lane_c >= HALF) & ((lane_c - HALF) >= row_c))
    zero_c = jnp.zeros((c, LANES), F32)

    def gate_preact(i):
        rows = _rows(i, grp)
        la_s[rows, :] = _dot(z_ref[rows, :].astype(BF16), wgu)

    def decay_logs(i):
        rows = _rows(i, grp)
        pre = la_s[rows, :] + bg
        la = ((jnp.minimum(pre, 0.0) - jnp.log(1.0 + jnp.exp(-jnp.abs(pre))))
              * (1.0 / GLA_GATE_NORM))
        la_hi = la.astype(BF16)
        la_s[rows, :] = la
        hl_s[rows, 0:LANES] = la_hi
        hl_s[rows, LANES:2 * LANES] = (la - la_hi.astype(F32)).astype(BF16)

    def chunk_cumsum(i):
        rows = _rows(i, grp)
        cs_s[rows, :] = _dot(cm_ref[...], hl_s[rows, :])

    def decayed_qk(i):
        rows = _rows(i, grp)
        cs = cs_s[rows, :]
        la3 = la_s[rows, :].reshape(cpg, c, LANES)
        cum3 = (cs[:, :LANES] + cs[:, LANES:]).reshape(cpg, c, LANES)
        tot = jnp.sum(la3, axis=1, keepdims=True)
        b3 = jnp.where(fwd_3, cum3, tot - cum3 + la3)
        b = b3.reshape(grp, LANES)
        rest = jnp.exp(tot - b3).reshape(grp, LANES)
        q2 = q_ref[rows, :].astype(F32)
        k2 = k_ref[rows, :].astype(F32)
        qq = jnp.where(own_g, q2, pltpu.roll(q2, HALF, axis=1))
        kk = jnp.where(own_g, k2, pltpu.roll(k2, HALF, axis=1))
        kin = kk * jnp.exp(-b)
        qdec_s[rows, :] = (qq * jnp.exp(b)).astype(BF16)
        kina_s[rows, :] = jnp.where(fwd_g, kin, zero_g).astype(BF16)
        kinb_s[rows, :] = jnp.where(fwd_g, zero_g, kin).astype(BF16)
        kst_s[rows, :] = (kk * rest).astype(BF16)
        dec_s[_rows(i, cpg)] = jnp.broadcast_to(jnp.exp(tot), (cpg, 8, LANES))

    def chunk_products(i):
        for cc in range(cpg):
            n = i * cpg + cc
            rows = _rows(n, c)
            kin_bd = jnp.concatenate([kina_s[rows, :], kinb_s[rows, :]], axis=0)
            a2 = _dot_nt(qdec_s[rows, :], kin_bd)
            am_s[rows, :] = jnp.where(tri, a2, zero_c).astype(BF16)
            ds_s[n] = _dot_tn(v_ref[rows, :], kst_s[rows, :])

    _software_pipeline(n_groups, [gate_preact, decay_logs, chunk_cumsum, decayed_qk,
                                  chunk_products], unroll=4)

    fwd_sq = lax.broadcasted_iota(jnp.int32, (LANES, LANES), 1) < HALF
    fwd_row = lax.broadcasted_iota(jnp.int32, (8, LANES), 1) < HALF

    def scan_step(i, s):
        j = nc - 1 - i
        sb = s.astype(BF16)
        sc_s[i, :, 0:HALF] = sb[:, 0:HALF]
        sc_s[j, :, HALF:LANES] = sb[:, HALF:LANES]
        dec = jnp.where(fwd_row, dec_s[i], dec_s[j])[0:1]
        return dec * s + jnp.where(fwd_sq, ds_s[i], ds_s[j])

    lax.fori_loop(0, nc, scan_step, jnp.zeros((LANES, LANES), F32), unroll=GLA_UNROLL)

    nrm = nrm_ref[...]

    def mix(i):
        for cc in range(cpg):
            n = i * cpg + cc
            rows = _rows(n, c)
            vc = v_ref[rows, :]
            la_s[rows, :] = (_dot_nt(qdec_s[rows, :], sc_s[n])
                             + _dot(am_s[rows, :], jnp.concatenate([vc, vc], axis=0)))

    def norm_gate(i):
        rows = _rows(i, grp)
        gate = g_ref[rows, :].astype(F32)
        o = (_rms(la_s[rows, :]) * nrm) * (gate / (1.0 + jnp.exp(-gate)))
        o_ref[rows, :] = o.astype(o_ref.dtype)

    _software_pipeline(n_groups, [mix, norm_gate], unroll=5)


def _gla(q, k, v, g, z, wgu, bg, nrm, batch, seq):
    t = batch * seq
    nc = seq // GLA_CHUNK
    tile = pl.BlockSpec((seq, LANES), lambda b, h: (b, h))
    pair_tile = pl.BlockSpec((seq, LANES), lambda b, h: (b, h // 2))
    tok_bf = pltpu.VMEM((seq, LANES), BF16)
    cm = jnp.asarray(_cumsum_matrix(), dtype=BF16)
    return pl.pallas_call(
        functools.partial(_gla_kernel, seq=seq),
        grid=(batch, GLA_HEADS),
        in_specs=[pair_tile, pair_tile, tile, tile,
                  pl.BlockSpec((seq, Z_W), lambda b, h: (b, 0)),
                  pl.BlockSpec((1, Z_W, LANES), lambda b, h: (h, 0, 0)),
                  pl.BlockSpec((1, 1, LANES), lambda b, h: (h, 0, 0)),
                  pl.BlockSpec((1, LANES), lambda b, h: (0, 0)),
                  _const_spec(cm.shape)],
        out_specs=tile,
        out_shape=jax.ShapeDtypeStruct((t, V_GLA_W), BF16),
        scratch_shapes=[pltpu.VMEM((seq, LANES), F32),
                        pltpu.VMEM((seq, 2 * LANES), BF16),
                        pltpu.VMEM((seq, 2 * LANES), F32),
                        tok_bf, tok_bf, tok_bf, tok_bf, tok_bf,
                        pltpu.VMEM((nc, LANES, LANES), F32),
                        pltpu.VMEM((nc, 8, LANES), F32),
                        pltpu.VMEM((nc, LANES, LANES), BF16)],
        compiler_params=pltpu.CompilerParams(
            dimension_semantics=("arbitrary", "arbitrary"), vmem_limit_bytes=VMEM_LIMIT),
        name="gla",
    )(q, k, v, g, z, wgu, bg, nrm, cm)


def _t5_buckets(rel):
    nb = REL_BUCKETS // 2
    ret = (rel > 0).astype(np.int32) * nb
    n = np.abs(rel)
    max_exact = nb // 2
    large = max_exact + (np.log(np.maximum(n, 1).astype(np.float32) / max_exact)
                         / math.log(REL_MAX_DIST / max_exact) * (nb - max_exact)).astype(np.int32)
    large = np.minimum(large, nb - 1)
    return ret + np.where(n < max_exact, n, large)


def _bucket_map():
    w = SWA_BLOCK
    cq = np.arange(w)[:, None]
    s = np.arange(3 * w)[None, :]
    rel = s - w - cq
    return np.where(np.abs(rel) <= SWA_WINDOW, _t5_buckets(rel), -1).astype(np.int32)


def _bias_kernel(bm_ref, tab_ref, o_ref):
    w = SWA_BLOCK
    pair = pl.program_id(0)
    bm = bm_ref[...]
    key = lax.broadcasted_iota(jnp.int32, bm.shape, 1)
    masked = jnp.full(bm.shape, MASK_VALUE, F32)
    for hh in range(2):
        head = 2 * pair + hh
        acc = masked
        for bkt in range(REL_BUCKETS):
            acc = jnp.where(bm == bkt, tab_ref[bkt, head], acc)
        hcols = slice(hh * 3 * w, (hh + 1) * 3 * w)
        o_ref[0, 0, :, hcols] = jnp.where(key >= w, acc, masked)
        o_ref[1, 0, :, hcols] = acc
        o_ref[2, 0, :, hcols] = jnp.where(key < 2 * w, acc, masked)


def _bias_tables(rel_bias):
    w = SWA_BLOCK
    pairs = SWA_Q_HEADS // 2
    return pl.pallas_call(
        _bias_kernel,
        grid=(pairs,),
        in_specs=[pl.BlockSpec((w, 3 * w), lambda p: (0, 0)),
                  pl.BlockSpec(memory_space=pltpu.SMEM)],
        out_specs=pl.BlockSpec((3, 1, w, 6 * w), lambda p: (0, p, 0, 0)),
        out_shape=jax.ShapeDtypeStruct((3, pairs, w, 6 * w), F32),
        compiler_params=pltpu.CompilerParams(dimension_semantics=("arbitrary",)),
        name="swa_bias",
    )(jnp.asarray(_bucket_map()), rel_bias.astype(F32))


def _swa_kernel(q_ref, k_ref, v_ref, bias_ref, sink_ref, o_ref, ka_s, kb_s, va_s, vb_s,
                lg_s, p_s, st_s, *, seq):
    w = SWA_BLOCK
    nb = seq // w
    kvh = pl.program_id(1)
    pairs = SWA_Q_HEADS // SWA_KV_HEADS // 2
    stage_rows = 512
    lane_s = lax.broadcasted_iota(jnp.int32, (stage_rows, LANES), 1)
    lo = lane_s < HALF
    own = (lane_s // HALF) == kvh
    for s_ref in (ka_s, kb_s, va_s, vb_s):
        zpad = jnp.zeros((w, s_ref.shape[1]), BF16)
        s_ref[0:w, :] = zpad
        s_ref[w + seq:2 * w + seq, :] = zpad

    def stage(i, carry):
        src = pl.ds(pl.multiple_of(i * stage_rows, stage_rows), stage_rows)
        dst = pl.ds(pl.multiple_of(i * stage_rows + w, w), stage_rows)
        k2 = k_ref[src, :].astype(F32)
        v2 = v_ref[src, :].astype(F32)
        kk = jnp.where(own, k2, pltpu.roll(k2, HALF, axis=1))
        vv = jnp.where(own, v2, pltpu.roll(v2, HALF, axis=1))
        zero = jnp.zeros_like(kk)
        one = jnp.ones_like(kk)
        ka_s[dst, :] = jnp.where(lo, kk, zero).astype(BF16)
        kb_s[dst, :] = jnp.where(lo, zero, kk).astype(BF16)
        va_s[dst, 0:LANES] = jnp.where(lo, vv, zero).astype(BF16)
        va_s[dst, LANES:2 * LANES] = jnp.where(lo, one, zero).astype(BF16)
        vb_s[dst, 0:LANES] = jnp.where(lo, zero, vv).astype(BF16)
        vb_s[dst, LANES:2 * LANES] = jnp.where(lo, zero, one).astype(BF16)
        return carry

    lax.fori_loop(0, seq // stage_rows, stage, 0)

    lo_w = lax.broadcasted_iota(jnp.int32, (w, LANES), 1) < HALF

    def logits_stage(n, variant):
        qrows = pl.ds(pl.multiple_of(n * w, w), w)
        band = pl.ds(pl.multiple_of(n * w, w), 3 * w)
        kt = jnp.concatenate([ka_s[band, :], kb_s[band, :]], axis=0)
        for pair in range(pairs):
            cols = slice(pair * LANES, (pair + 1) * LANES)
            lg_s[pair] = _dot_nt(q_ref[qrows, cols], kt) + bias_ref[variant, pair]

    def softmax_stage():
        for pair in range(pairs):
            sink_terms = []
            for hh in range(2):
                hcols = slice(hh * 3 * w, (hh + 1) * 3 * w)
                lg = lg_s[pair, :, hcols]
                sink = sink_ref[kvh * 2 * pairs + 2 * pair + hh]
                m = jnp.maximum(jnp.max(lg, axis=-1, keepdims=True), sink)
                p_s[pair, :, hcols] = jnp.exp(lg - m).astype(BF16)
                sink_terms.append(jnp.exp(sink - m))
            st_s[pair] = jnp.where(lo_w, sink_terms[0], sink_terms[1])

    def output_stage(n):
        qrows = pl.ds(pl.multiple_of(n * w, w), w)
        band = pl.ds(pl.multiple_of(n * w, w), 3 * w)
        vx = jnp.concatenate([va_s[band, :], vb_s[band, :]], axis=0)
        for pair in range(pairs):
            ox = _dot(p_s[pair], vx)
            o = ox[:, :LANES] / (ox[:, LANES:] + st_s[pair])
            o_ref[qrows, pair * LANES:(pair + 1) * LANES] = o.astype(o_ref.dtype)

    logits_stage(0, 0)
    softmax_stage()
    logits_stage(1, 1)

    def steady(n, carry):
        output_stage(n - 1)
        softmax_stage()
        logits_stage(n + 1, 1)
        return carry

    lax.fori_loop(1, nb - 2, steady, 0, unroll=2)
    output_stage(nb - 3)
    softmax_stage()
    logits_stage(nb - 1, 2)
    output_stage(nb - 2)
    softmax_stage()
    output_stage(nb - 1)


def _swa(q, k, v, bias, sink, batch, seq):
    t = batch * seq
    w = SWA_BLOCK
    qw = Q_SWA_W // SWA_KV_HEADS
    pairs = SWA_Q_HEADS // SWA_KV_HEADS // 2
    staged_k = pltpu.VMEM((seq + 2 * w, LANES), BF16)
    staged_v = pltpu.VMEM((seq + 2 * w, 2 * LANES), BF16)
    return pl.pallas_call(
        functools.partial(_swa_kernel, seq=seq),
        grid=(batch, SWA_KV_HEADS),
        in_specs=[pl.BlockSpec((seq, qw), lambda b, j: (b, j)),
                  pl.BlockSpec((seq, LANES), lambda b, j: (b, 0)),
                  pl.BlockSpec((seq, LANES), lambda b, j: (b, 0)),
                  pl.BlockSpec((3, pairs, w, 6 * w), lambda b, j: (0, j, 0, 0)),
                  pl.BlockSpec(memory_space=pltpu.SMEM)],
        out_specs=pl.BlockSpec((seq, qw), lambda b, j: (b, j)),
        out_shape=jax.ShapeDtypeStruct((t, Q_SWA_W), BF16),
        scratch_shapes=[staged_k, staged_k, staged_v, staged_v,
                        pltpu.VMEM((pairs, w, 6 * w), F32),
                        pltpu.VMEM((pairs, w, 6 * w), BF16),
                        pltpu.VMEM((pairs, w, LANES), F32)],
        compiler_params=pltpu.CompilerParams(
            dimension_semantics=("arbitrary", "arbitrary"), vmem_limit_bytes=VMEM_LIMIT),
        name="swa",
    )(q, k, v, bias, sink)


def _ffn_kernel(x_ref, oa_ref, os_ref, wa_ref, ws_ref, gpost_ref, gpre_ref, wup_ref, wdn_ref,
                gout_ref, o_ref):
    slab = TOKEN_BLOCK // FFN_SLABS
    slabs = [slice(s * slab, (s + 1) * slab) for s in range(FFN_SLABS)]
    mixes = [_dot(oa_ref[rows, :], wa_ref[...]) + _dot(os_ref[rows, :], ws_ref[...])
             for rows in slabs]
    for rows, mix in zip(slabs, mixes):
        h = x_ref[rows, :] + _rms(mix) * gpost_ref[...]
        hn = (_rms(h) * gpre_ref[...]).astype(BF16)
        acc = jnp.zeros(h.shape, F32)
        for j in range(D_FF // FF_BLOCK):
            cols = slice(j * FF_BLOCK, (j + 1) * FF_BLOCK)
            zj = jnp.maximum(_dot(hn, wup_ref[:, cols]), 0.0)
            acc = acc + _dot((zj * zj).astype(BF16), wdn_ref[cols, :])
        o_ref[rows, :] = h + _rms(acc) * gout_ref[...]


def _ffn(x2, oa, os_, wa, ws, gpost, gpre, wup, wdn, gout):
    t = x2.shape[0]
    row = lambda width: pl.BlockSpec((TOKEN_BLOCK, width), lambda i: (i, 0))
    return pl.pallas_call(
        _ffn_kernel,
        grid=(t // TOKEN_BLOCK,),
        in_specs=[row(D_MODEL), row(V_GLA_W), row(Q_SWA_W),
                  _const_spec((V_GLA_W, D_MODEL)), _const_spec((Q_SWA_W, D_MODEL)),
                  _const_spec((1, D_MODEL)), _const_spec((1, D_MODEL)),
                  _const_spec((D_MODEL, D_FF)), _const_spec((D_FF, D_MODEL)),
                  _const_spec((1, D_MODEL))],
        out_specs=row(D_MODEL),
        out_shape=jax.ShapeDtypeStruct((t, D_MODEL), F32),
        compiler_params=pltpu.CompilerParams(
            dimension_semantics=("arbitrary",), vmem_limit_bytes=VMEM_LIMIT),
        name="outproj_ffn",
    )(x2, oa, os_, wa, ws, gpost, gpre, wup, wdn, gout)


def _regroup_w_in(w_in):
    sizes = (GLA_HEADS * GLA_DK, GLA_HEADS * GLA_DK, V_GLA_W, G_GLA_W, Z_W,
             Q_SWA_W, SWA_KV_HEADS * SWA_DH, SWA_KV_HEADS * SWA_DH)
    starts = np.concatenate([[0], np.cumsum(sizes)[:-1]])
    qa, ka, va, ga, za, qs, ks, vs = (w_in[:, s:s + n] for s, n in zip(starts, sizes))
    cols = [qa * GLA_DK ** -0.5, ka, va, ga, qs * SWA_DH ** -0.5, ks, vs, za]
    return jnp.concatenate([col.astype(BF16) for col in cols], axis=1)


def _layer(h, norm_mix_pre, w_in, w_gu_f, b_g_f, w_gu_b, b_g_b, gla_norm, swa_sink, bias_tab,
           w_out, norm_mix_post, norm_mlp_pre, w_up, w_down, norm_mlp_post):
    batch, seq, _ = h.shape
    x2 = h.reshape(batch * seq, D_MODEL)
    w_in_r = _regroup_w_in(w_in)
    row = lambda g: g.reshape(1, -1).astype(F32)

    qa, ka, va, ga, qs, ks, vs, z = _inproj(x2, row(norm_mix_pre), w_in_r)

    wf = w_gu_f.reshape(GLA_RANK, GLA_HEADS, GLA_DK).transpose(1, 0, 2)
    wb = w_gu_b.reshape(GLA_RANK, GLA_HEADS, GLA_DK).transpose(1, 0, 2)
    zeros = jnp.zeros_like(wf)
    wgu = jnp.concatenate([jnp.concatenate([wf, zeros], axis=2),
                           jnp.concatenate([zeros, wb], axis=2)], axis=1).astype(BF16)
    bg = jnp.concatenate([b_g_f.reshape(GLA_HEADS, 1, GLA_DK),
                          b_g_b.reshape(GLA_HEADS, 1, GLA_DK)], axis=2).astype(F32)
    o_a = _gla(qa, ka, va, ga, z, wgu, bg, row(gla_norm), batch, seq)

    o_s = _swa(qs, ks, vs, bias_tab, swa_sink.astype(F32), batch, seq)

    out = _ffn(x2, o_a, o_s, w_out[:V_GLA_W].astype(BF16), w_out[V_GLA_W:].astype(BF16),
               row(norm_mix_post), row(norm_mlp_pre), w_up.astype(BF16), w_down.astype(BF16),
               row(norm_mlp_post))
    return out.reshape(batch, seq, D_MODEL)


def kernel(x, norm_mix_pre, w_in, w_gate_up_fwd, b_gate_fwd, w_gate_up_bwd, b_gate_bwd, gla_norm,
           swa_sink, rel_bias, w_out, norm_mix_post, norm_mlp_pre, w_up, w_down, norm_mlp_post):
    bias_tab = _bias_tables(rel_bias)
    h = x
    for l in range(w_in.shape[0]):
        h = _layer(h, norm_mix_pre[l], w_in[l], w_gate_up_fwd[l], b_gate_fwd[l], w_gate_up_bwd[l],
                   b_gate_bwd[l], gla_norm[l], swa_sink[l], bias_tab, w_out[l], norm_mix_post[l],
                   norm_mlp_pre[l], w_up[l], w_down[l], norm_mlp_post[l])
    return h
```

```python
import functools
import math

import jax
import jax.numpy as jnp
import numpy as np
from jax import lax
from jax.experimental import pallas as pl
from jax.experimental.pallas import tpu as pltpu

F32 = jnp.float32
BF16 = jnp.bfloat16

D_MODEL = 1024
GLA_HEADS = 4
GLA_DK = 64
GLA_DV = 128
GLA_RANK = 16
GLA_GATE_NORM = 16.0
GLA_CHUNK = 64
SWA_DH = 64
SWA_Q_HEADS = 8
SWA_KV_HEADS = 2
SWA_BLOCK = 128
SWA_WINDOW = 128
REL_BUCKETS = 32
REL_MAX_DIST = 128
D_FF = 4 * D_MODEL
NORM_EPS = 1e-6
MASK_VALUE = -1e30

LANES = 128
HALF = LANES // 2

QK_GLA_W = GLA_HEADS * GLA_DK
V_GLA_W = GLA_HEADS * GLA_DV
G_GLA_W = GLA_HEADS * GLA_DV
Q_SWA_W = SWA_Q_HEADS * SWA_DH
KV_SWA_W = SWA_KV_HEADS * SWA_DH
Z_W = 2 * GLA_RANK
IN_WIDTHS = (QK_GLA_W, QK_GLA_W, V_GLA_W, G_GLA_W, Q_SWA_W, KV_SWA_W, KV_SWA_W, Z_W)
IN_COLS = sum(IN_WIDTHS)
IN_DOT_GROUPS = ((0, 1), (2,), (3,), (4,), (5, 6), (7,))

GLA_UNROLL = 8
GLA_GROUP = 256
GLA_PREP_UNROLL = 6
GLA_OUT_UNROLL = 15
TOKEN_BLOCK = 512
FF_BLOCK = 1024
FFN_SLABS = 2
VMEM_LIMIT = 56 * 1024 * 1024


def _rms(x):
    return x * lax.rsqrt(jnp.mean(x * x, axis=-1, keepdims=True) + NORM_EPS)


def _dot(a, b):
    return jnp.dot(a, b, preferred_element_type=F32)


def _dot_nt(a, b):
    return lax.dot_general(a, b, (((1,), (1,)), ((), ())), preferred_element_type=F32)


def _dot_tn(a, b):
    return lax.dot_general(a, b, (((0,), (0,)), ((), ())), preferred_element_type=F32)


def _const_spec(shape):
    nd = len(shape)
    return pl.BlockSpec(shape, lambda *_: (0,) * nd, pipeline_mode=pl.Buffered(1))


def _rows(item, size):
    start = item * size
    if not isinstance(start, int):
        start = pl.multiple_of(start, size)
    return pl.ds(start, size)


def _software_pipeline(n_items, stages, unroll=1):
    depth = len(stages)

    def trip(t, static):
        for k in reversed(range(depth)):
            if not static or 0 <= t - k < n_items:
                stages[k](t - k)

    for t in range(depth - 1):
        trip(t, True)

    def steady(t, carry):
        trip(t, False)
        return carry

    lax.fori_loop(depth - 1, n_items, steady, 0, unroll=unroll)
    for t in range(n_items, n_items + depth - 1):
        trip(t, True)


def _inproj_kernel(x_ref, g_ref, w_ref, *out_refs):
    u = (_rms(x_ref[...]) * g_ref[...]).astype(BF16)
    off = 0
    for group in IN_DOT_GROUPS:
        width = sum(IN_WIDTHS[i] for i in group)
        y = _dot(u, w_ref[:, off:off + width])
        off += width
        col = 0
        for i in group:
            out_refs[i][...] = y[:, col:col + IN_WIDTHS[i]].astype(out_refs[i].dtype)
            col += IN_WIDTHS[i]


def _inproj(x2, g, w):
    t = x2.shape[0]
    row = lambda width: pl.BlockSpec((TOKEN_BLOCK, width), lambda i: (i, 0))
    widths = IN_WIDTHS
    dtypes = (BF16,) * (len(IN_WIDTHS) - 1) + (F32,)
    return pl.pallas_call(
        _inproj_kernel,
        grid=(t // TOKEN_BLOCK,),
        in_specs=[row(D_MODEL), _const_spec((1, D_MODEL)), _const_spec((D_MODEL, IN_COLS))],
        out_specs=[row(wd) for wd in widths],
        out_shape=[jax.ShapeDtypeStruct((t, wd), dt) for wd, dt in zip(widths, dtypes)],
        compiler_params=pltpu.CompilerParams(
            dimension_semantics=("arbitrary",), vmem_limit_bytes=VMEM_LIMIT),
        name="inproj",
    )(x2, g, w)


def _cumsum_matrix():
    r = np.arange(GLA_GROUP)[:, None]
    s = np.arange(GLA_GROUP)[None, :]
    return (((r // GLA_CHUNK) == (s // GLA_CHUNK)) & (s <= r)).astype(np.float32)


def _gla_kernel(q_ref, k_ref, v_ref, g_ref, z_ref, wgu_ref, bg_ref, nrm_ref, cm_ref, o_ref,
                la_s, hl_s, cs_s, qdec_s, kina_s, kinb_s, kst_s, am_s, ds_s, dec_s, sc_s, *, seq):
    c = GLA_CHUNK
    nc = seq // c
    grp = GLA_GROUP
    cpg = grp // c
    n_groups = seq // grp
    wgu = wgu_ref[0]
    bg = bg_ref[0]
    fwd_p = lax.broadcasted_iota(jnp.int16, (grp, LANES), 1) < HALF
    own_p = fwd_p == (pl.program_id(1) % 2 == 0)
    zero_p = jnp.zeros((grp, LANES), BF16)

    def own_head_twice(tile):
        return jnp.where(own_p, tile, pltpu.roll(tile, HALF, axis=1)).astype(F32)

    fwd_3 = lax.broadcasted_iota(jnp.int32, (cpg, c, LANES), 2) < HALF
    lane_c = lax.broadcasted_iota(jnp.int32, (c, LANES), 1)
    row_c = lax.broadcasted_iota(jnp.int32, (c, LANES), 0)
    tri = ((lane_c < HALF) & (lane_c <= row_c)) | ((lane_c >= HALF) & ((lane_c - HALF) >= row_c))
    zero_c = jnp.zeros((c, LANES), F32)

    def gate_preact(i):
        rows = _rows(i, grp)
        la_s[rows, :] = _dot(z_ref[rows, :].astype(BF16), wgu)

    def decay_logs(i):
        rows = _rows(i, grp)
        pre = la_s[rows, :] + bg
        la = ((jnp.minimum(pre, 0.0) - jnp.log(1.0 + jnp.exp(-jnp.abs(pre))))
              * (1.0 / GLA_GATE_NORM))
        la_hi = la.astype(BF16)
        la_s[rows, :] = la
        hl_s[rows, 0:LANES] = la_hi
        hl_s[rows, LANES:2 * LANES] = (la - la_hi.astype(F32)).astype(BF16)

    def chunk_cumsum(i):
        rows = _rows(i, grp)
        cs_s[rows, :] = _dot(cm_ref[...], hl_s[rows, :])

    def decayed_qk(i):
        rows = _rows(i, grp)
        cs = cs_s[rows, :]
        la3 = la_s[rows, :].reshape(cpg, c, LANES)
        cum3 = (cs[:, :LANES] + cs[:, LANES:]).reshape(cpg, c, LANES)
        tot = jnp.sum(la3, axis=1, keepdims=True)
        b3 = jnp.where(fwd_3, cum3, tot - cum3 + la3)
        b = b3.reshape(grp, LANES)
        rest = jnp.exp(tot - b3).reshape(grp, LANES)
        qq = own_head_twice(q_ref[rows, :])
        kk = own_head_twice(k_ref[rows, :])
        kin = (kk * jnp.exp(-b)).astype(BF16)
        qdec_s[rows, :] = (qq * jnp.exp(b)).astype(BF16)
        kina_s[rows, :] = jnp.where(fwd_p, kin, zero_p)
        kinb_s[rows, :] = jnp.where(fwd_p, zero_p, kin)
        kst_s[rows, :] = (kk * rest).astype(BF16)
        dec_s[_rows(i, cpg)] = jnp.broadcast_to(jnp.exp(tot), (cpg, 8, LANES))

    def chunk_products(i):
        for cc in range(cpg):
            n = i * cpg + cc
            rows = _rows(n, c)
            kin_bd = jnp.concatenate([kina_s[rows, :], kinb_s[rows, :]], axis=0)
            a2 = _dot_nt(qdec_s[rows, :], kin_bd)
            am_s[rows, :] = jnp.where(tri, a2, zero_c).astype(BF16)
            ds_s[n] = _dot_tn(v_ref[rows, :], kst_s[rows, :])

    _software_pipeline(n_groups, [gate_preact, decay_logs, chunk_cumsum, decayed_qk,
                                  chunk_products], unroll=GLA_PREP_UNROLL)

    fwd_sq = lax.broadcasted_iota(jnp.int32, (LANES, LANES), 1) < HALF
    fwd_row = lax.broadcasted_iota(jnp.int32, (8, LANES), 1) < HALF

    def scan_step(i, s):
        j = nc - 1 - i
        sb = s.astype(BF16)
        sc_s[i, :, 0:HALF] = sb[:, 0:HALF]
        sc_s[j, :, HALF:LANES] = sb[:, HALF:LANES]
        dec = jnp.where(fwd_row, dec_s[i], dec_s[j])[0:1]
        return dec * s + jnp.where(fwd_sq, ds_s[i], ds_s[j])

    lax.fori_loop(0, nc, scan_step, jnp.zeros((LANES, LANES), F32), unroll=GLA_UNROLL)

    nrm = nrm_ref[...]

    def mix(i):
        for cc in range(cpg):
            n = i * cpg + cc
            rows = _rows(n, c)
            vc = v_ref[rows, :]
            la_s[rows, :] = (_dot_nt(qdec_s[rows, :], sc_s[n])
                             + _dot(am_s[rows, :], jnp.concatenate([vc, vc], axis=0)))

    def norm_gate(i):
        rows = _rows(i, grp)
        gate = g_ref[rows, :].astype(F32)
        o = (_rms(la_s[rows, :]) * nrm) * (gate / (1.0 + jnp.exp(-gate)))
        o_ref[rows, :] = o.astype(o_ref.dtype)

    _software_pipeline(n_groups, [mix, norm_gate], unroll=GLA_OUT_UNROLL)


def _gla(q, k, v, g, z, wgu, bg, nrm, batch, seq):
    t = batch * seq
    nc = seq // GLA_CHUNK
    tile = pl.BlockSpec((seq, LANES), lambda b, h: (b, h))
    pair_tile = pl.BlockSpec((seq, LANES), lambda b, h: (b, h // 2))
    tok_bf = pltpu.VMEM((seq, LANES), BF16)
    cm = jnp.asarray(_cumsum_matrix(), dtype=BF16)
    return pl.pallas_call(
        functools.partial(_gla_kernel, seq=seq),
        grid=(batch, GLA_HEADS),
        in_specs=[pair_tile, pair_tile, tile, tile,
                  pl.BlockSpec((seq, Z_W), lambda b, h: (b, 0)),
                  pl.BlockSpec((1, Z_W, LANES), lambda b, h: (h, 0, 0)),
                  pl.BlockSpec((1, 1, LANES), lambda b, h: (h, 0, 0)),
                  pl.BlockSpec((1, LANES), lambda b, h: (0, 0)),
                  _const_spec(cm.shape)],
        out_specs=tile,
        out_shape=jax.ShapeDtypeStruct((t, V_GLA_W), BF16),
        scratch_shapes=[pltpu.VMEM((seq, LANES), F32),
                        pltpu.VMEM((seq, 2 * LANES), BF16),
                        pltpu.VMEM((seq, 2 * LANES), F32),
                        tok_bf, tok_bf, tok_bf, tok_bf, tok_bf,
                        pltpu.VMEM((nc, LANES, LANES), F32),
                        pltpu.VMEM((nc, 8, LANES), F32),
                        pltpu.VMEM((nc, LANES, LANES), BF16)],
        compiler_params=pltpu.CompilerParams(
            dimension_semantics=("arbitrary", "arbitrary"), vmem_limit_bytes=VMEM_LIMIT),
        name="gla",
    )(q, k, v, g, z, wgu, bg, nrm, cm)


def _t5_buckets(rel):
    nb = REL_BUCKETS // 2
    ret = (rel > 0).astype(np.int32) * nb
    n = np.abs(rel)
    max_exact = nb // 2
    large = max_exact + (np.log(np.maximum(n, 1).astype(np.float32) / max_exact)
                         / math.log(REL_MAX_DIST / max_exact) * (nb - max_exact)).astype(np.int32)
    large = np.minimum(large, nb - 1)
    return ret + np.where(n < max_exact, n, large)


def _bucket_map():
    w = SWA_BLOCK
    cq = np.arange(w)[:, None]
    s = np.arange(3 * w)[None, :]
    rel = s - w - cq
    return np.where(np.abs(rel) <= SWA_WINDOW, _t5_buckets(rel), -1).astype(np.int32)


def _bias_kernel(bm_ref, tab_ref, o_ref):
    w = SWA_BLOCK
    pair = pl.program_id(0)
    bm = bm_ref[...]
    key = lax.broadcasted_iota(jnp.int32, bm.shape, 1)
    masked = jnp.full(bm.shape, MASK_VALUE, F32)
    for hh in range(2):
        head = 2 * pair + hh
        acc = masked
        for bkt in range(REL_BUCKETS):
            acc = jnp.where(bm == bkt, tab_ref[bkt, head], acc)
        hcols = slice(hh * 3 * w, (hh + 1) * 3 * w)
        o_ref[0, 0, :, hcols] = jnp.where(key >= w, acc, masked)
        o_ref[1, 0, :, hcols] = acc
        o_ref[2, 0, :, hcols] = jnp.where(key < 2 * w, acc, masked)


def _bias_tables(rel_bias):
    w = SWA_BLOCK
    pairs = SWA_Q_HEADS // 2
    return pl.pallas_call(
        _bias_kernel,
        grid=(pairs,),
        in_specs=[pl.BlockSpec((w, 3 * w), lambda p: (0, 0)),
                  pl.BlockSpec(memory_space=pltpu.SMEM)],
        out_specs=pl.BlockSpec((3, 1, w, 6 * w), lambda p: (0, p, 0, 0)),
        out_shape=jax.ShapeDtypeStruct((3, pairs, w, 6 * w), F32),
        compiler_params=pltpu.CompilerParams(dimension_semantics=("arbitrary",)),
        name="swa_bias",
    )(jnp.asarray(_bucket_map()), rel_bias.astype(F32))


def _swa_kernel(q_ref, k_ref, v_ref, bias_ref, sink_ref, o_ref, ka_s, kb_s, va_s, vb_s,
                lg_s, p_s, st_s, *, seq):
    w = SWA_BLOCK
    nb = seq // w
    kvh = pl.program_id(1)
    pairs = SWA_Q_HEADS // SWA_KV_HEADS // 2
    stage_rows = 512
    lane_s = lax.broadcasted_iota(jnp.int16, (stage_rows, LANES), 1)
    lo = lane_s < HALF
    own = lo == (kvh == 0)
    for s_ref in (ka_s, kb_s, va_s, vb_s):
        zpad = jnp.zeros((w, s_ref.shape[1]), BF16)
        s_ref[0:w, :] = zpad
        s_ref[w + seq:2 * w + seq, :] = zpad

    def stage(i, carry):
        src = pl.ds(pl.multiple_of(i * stage_rows, stage_rows), stage_rows)
        dst = pl.ds(pl.multiple_of(i * stage_rows + w, w), stage_rows)
        k2 = k_ref[src, :]
        v2 = v_ref[src, :]
        kk = jnp.where(own, k2, pltpu.roll(k2, HALF, axis=1))
        vv = jnp.where(own, v2, pltpu.roll(v2, HALF, axis=1))
        zero = jnp.zeros_like(kk)
        one = jnp.ones_like(kk)
        ka_s[dst, :] = jnp.where(lo, kk, zero)
        kb_s[dst, :] = jnp.where(lo, zero, kk)
        va_s[dst, 0:LANES] = jnp.where(lo, vv, zero)
        va_s[dst, LANES:2 * LANES] = jnp.where(lo, one, zero)
        vb_s[dst, 0:LANES] = jnp.where(lo, zero, vv)
        vb_s[dst, LANES:2 * LANES] = jnp.where(lo, zero, one)
        return carry

    lax.fori_loop(0, seq // stage_rows, stage, 0)

    lo_w = lax.broadcasted_iota(jnp.int32, (w, LANES), 1) < HALF

    def logits_stage(n, variant):
        qrows = pl.ds(pl.multiple_of(n * w, w), w)
        band = pl.ds(pl.multiple_of(n * w, w), 3 * w)
        kt = jnp.concatenate([ka_s[band, :], kb_s[band, :]], axis=0)
        for pair in range(pairs):
            cols = slice(pair * LANES, (pair + 1) * LANES)
            lg_s[pair] = _dot_nt(q_ref[qrows, cols], kt) + bias_ref[variant, pair]

    def softmax_stage():
        for pair in range(pairs):
            sink_terms = []
            for hh in range(2):
                hcols = slice(hh * 3 * w, (hh + 1) * 3 * w)
                lg = lg_s[pair, :, hcols]
                sink = sink_ref[kvh * 2 * pairs + 2 * pair + hh]
                m = jnp.maximum(jnp.max(lg, axis=-1, keepdims=True), sink)
                p_s[pair, :, hcols] = jnp.exp(lg - m).astype(BF16)
                sink_terms.append(jnp.exp(sink - m))
            st_s[pair] = jnp.where(lo_w, sink_terms[0], sink_terms[1])

    def output_stage(n):
        qrows = pl.ds(pl.multiple_of(n * w, w), w)
        band = pl.ds(pl.multiple_of(n * w, w), 3 * w)
        vx = jnp.concatenate([va_s[band, :], vb_s[band, :]], axis=0)
        for pair in range(pairs):
            ox = _dot(p_s[pair], vx)
            o = ox[:, :LANES] / (ox[:, LANES:] + st_s[pair])
            o_ref[qrows, pair * LANES:(pair + 1) * LANES] = o.astype(o_ref.dtype)

    logits_stage(0, 0)
    softmax_stage()
    logits_stage(1, 1)

    def steady(n, carry):
        output_stage(n - 1)
        softmax_stage()
        logits_stage(n + 1, 1)
        return carry

    lax.fori_loop(1, nb - 2, steady, 0, unroll=2)
    output_stage(nb - 3)
    softmax_stage()
    logits_stage(nb - 1, 2)
    output_stage(nb - 2)
    softmax_stage()
    output_stage(nb - 1)


def _swa(q, k, v, bias, sink, batch, seq):
    t = batch * seq
    w = SWA_BLOCK
    qw = Q_SWA_W // SWA_KV_HEADS
    pairs = SWA_Q_HEADS // SWA_KV_HEADS // 2
    staged_k = pltpu.VMEM((seq + 2 * w, LANES), BF16)
    staged_v = pltpu.VMEM((seq + 2 * w, 2 * LANES), BF16)
    return pl.pallas_call(
        functools.partial(_swa_kernel, seq=seq),
        grid=(batch, SWA_KV_HEADS),
        in_specs=[pl.BlockSpec((seq, qw), lambda b, j: (b, j)),
                  pl.BlockSpec((seq, LANES), lambda b, j: (b, 0)),
                  pl.BlockSpec((seq, LANES), lambda b, j: (b, 0)),
                  pl.BlockSpec((3, pairs, w, 6 * w), lambda b, j: (0, j, 0, 0)),
                  pl.BlockSpec(memory_space=pltpu.SMEM)],
        out_specs=pl.BlockSpec((seq, qw), lambda b, j: (b, j)),
        out_shape=jax.ShapeDtypeStruct((t, Q_SWA_W), BF16),
        scratch_shapes=[staged_k, staged_k, staged_v, staged_v,
                        pltpu.VMEM((pairs, w, 6 * w), F32),
                        pltpu.VMEM((pairs, w, 6 * w), BF16),
                        pltpu.VMEM((pairs, w, LANES), F32)],
        compiler_params=pltpu.CompilerParams(
            dimension_semantics=("arbitrary", "arbitrary"), vmem_limit_bytes=VMEM_LIMIT),
        name="swa",
    )(q, k, v, bias, sink)


def _ffn_kernel(x_ref, oa_ref, os_ref, wa_ref, ws_ref, gpost_ref, gpre_ref, wup_ref, wdn_ref,
                gout_ref, o_ref):
    slab = TOKEN_BLOCK // FFN_SLABS
    slabs = [slice(s * slab, (s + 1) * slab) for s in range(FFN_SLABS)]
    mixes = [_dot(oa_ref[rows, :], wa_ref[...]) + _dot(os_ref[rows, :], ws_ref[...])
             for rows in slabs]
    for rows, mix in zip(slabs, mixes):
        h = x_ref[rows, :] + _rms(mix) * gpost_ref[...]
        hn = (_rms(h) * gpre_ref[...]).astype(BF16)
        acc = jnp.zeros(h.shape, F32)
        for j in range(D_FF // FF_BLOCK):
            cols = slice(j * FF_BLOCK, (j + 1) * FF_BLOCK)
            zj = jnp.maximum(_dot(hn, wup_ref[:, cols]), 0.0)
            acc = acc + _dot((zj * zj).astype(BF16), wdn_ref[cols, :])
        o_ref[rows, :] = h + _rms(acc) * gout_ref[...]


def _ffn(x2, oa, os_, wa, ws, gpost, gpre, wup, wdn, gout):
    t = x2.shape[0]
    row = lambda width: pl.BlockSpec((TOKEN_BLOCK, width), lambda i: (i, 0))
    return pl.pallas_call(
        _ffn_kernel,
        grid=(t // TOKEN_BLOCK,),
        in_specs=[row(D_MODEL), row(V_GLA_W), row(Q_SWA_W),
                  _const_spec((V_GLA_W, D_MODEL)), _const_spec((Q_SWA_W, D_MODEL)),
                  _const_spec((1, D_MODEL)), _const_spec((1, D_MODEL)),
                  _const_spec((D_MODEL, D_FF)), _const_spec((D_FF, D_MODEL)),
                  _const_spec((1, D_MODEL))],
        out_specs=row(D_MODEL),
        out_shape=jax.ShapeDtypeStruct((t, D_MODEL), F32),
        compiler_params=pltpu.CompilerParams(
            dimension_semantics=("arbitrary",), vmem_limit_bytes=VMEM_LIMIT),
        name="outproj_ffn",
    )(x2, oa, os_, wa, ws, gpost, gpre, wup, wdn, gout)


def _regroup_w_in(w_in):
    sizes = (GLA_HEADS * GLA_DK, GLA_HEADS * GLA_DK, V_GLA_W, G_GLA_W, Z_W,
             Q_SWA_W, SWA_KV_HEADS * SWA_DH, SWA_KV_HEADS * SWA_DH)
    starts = np.concatenate([[0], np.cumsum(sizes)[:-1]])
    qa, ka, va, ga, za, qs, ks, vs = (w_in[:, s:s + n] for s, n in zip(starts, sizes))
    cols = [qa * GLA_DK ** -0.5, ka, va, ga, qs * SWA_DH ** -0.5, ks, vs, za]
    return jnp.concatenate([col.astype(BF16) for col in cols], axis=1)


def _layer(h, norm_mix_pre, w_in, w_gu_f, b_g_f, w_gu_b, b_g_b, gla_norm, swa_sink, bias_tab,
           w_out, norm_mix_post, norm_mlp_pre, w_up, w_down, norm_mlp_post):
    batch, seq, _ = h.shape
    x2 = h.reshape(batch * seq, D_MODEL)
    w_in_r = _regroup_w_in(w_in)
    row = lambda g: g.reshape(1, -1).astype(F32)

    qa, ka, va, ga, qs, ks, vs, z = _inproj(x2, row(norm_mix_pre), w_in_r)

    wf = w_gu_f.reshape(GLA_RANK, GLA_HEADS, GLA_DK).transpose(1, 0, 2)
    wb = w_gu_b.reshape(GLA_RANK, GLA_HEADS, GLA_DK).transpose(1, 0, 2)
    zeros = jnp.zeros_like(wf)
    wgu = jnp.concatenate([jnp.concatenate([wf, zeros], axis=2),
                           jnp.concatenate([zeros, wb], axis=2)], axis=1).astype(BF16)
    bg = jnp.concatenate([b_g_f.reshape(GLA_HEADS, 1, GLA_DK),
                          b_g_b.reshape(GLA_HEADS, 1, GLA_DK)], axis=2).astype(F32)
    o_a = _gla(qa, ka, va, ga, z, wgu, bg, row(gla_norm), batch, seq)

    o_s = _swa(qs, ks, vs, bias_tab, swa_sink.astype(F32), batch, seq)

    out = _ffn(x2, o_a, o_s, w_out[:V_GLA_W].astype(BF16), w_out[V_GLA_W:].astype(BF16),
               row(norm_mix_post), row(norm_mlp_pre), w_up.astype(BF16), w_down.astype(BF16),
               row(norm_mlp_post))
    return out.reshape(batch, seq, D_MODEL)


def kernel(x, norm_mix_pre, w_in, w_gate_up_fwd, b_gate_fwd, w_gate_up_bwd, b_gate_bwd, gla_norm,
           swa_sink, rel_bias, w_out, norm_mix_post, norm_mlp_pre, w_up, w_down, norm_mlp_post):
    bias_tab = _bias_tables(rel_bias)
    h = x
    for l in range(w_in.shape[0]):
        h = _layer(h, norm_mix_pre[l], w_in[l], w_gate_up_fwd[l], b_gate_fwd[l], w_gate_up_bwd[l],
                   b_gate_bwd[l], gla_norm[l], swa_sink[l], bias_tab, w_out[l], norm_mix_post[l],
                   norm_mlp_pre[l], w_up[l], w_down[l], norm_mlp_post[l])
    return h
```

```python
import functools
import math

import jax
import jax.numpy as jnp
import numpy as np
from jax import lax
from jax.experimental import pallas as pl
from jax.experimental.pallas import tpu as pltpu

F32 = jnp.float32
BF16 = jnp.bfloat16

D_MODEL = 1024
GLA_HEADS = 4
GLA_DK = 64
GLA_DV = 128
GLA_RANK = 16
GLA_GATE_NORM = 16.0
GLA_CHUNK = 64
SWA_DH = 64
SWA_Q_HEADS = 8
SWA_KV_HEADS = 2
SWA_BLOCK = 128
SWA_WINDOW = 128
REL_BUCKETS = 32
REL_MAX_DIST = 128
D_FF = 4 * D_MODEL
NORM_EPS = 1e-6
MASK_VALUE = -1e30

LANES = 128
HALF = LANES // 2

QK_GLA_W = GLA_HEADS * GLA_DK
V_GLA_W = GLA_HEADS * GLA_DV
G_GLA_W = GLA_HEADS * GLA_DV
Q_SWA_W = SWA_Q_HEADS * SWA_DH
KV_SWA_W = SWA_KV_HEADS * SWA_DH
Z_W = 2 * GLA_RANK
IN_WIDTHS = (QK_GLA_W, QK_GLA_W, V_GLA_W, G_GLA_W, Q_SWA_W, KV_SWA_W, KV_SWA_W, Z_W)
IN_COLS = sum(IN_WIDTHS)
IN_DOT_GROUPS = ((0, 1), (2,), (3,), (4,), (5, 6), (7,))

GLA_UNROLL = 8
GLA_GROUP = 256
GLA_PREP_UNROLL = 6
GLA_OUT_UNROLL = 15
TOKEN_BLOCK = 1024
FFN_TOKEN_BLOCK = 1024
FF_BLOCK = 1024
FFN_SLABS = 2
VMEM_LIMIT = 56 * 1024 * 1024


def _rms(x):
    return x * lax.rsqrt(jnp.mean(x * x, axis=-1, keepdims=True) + NORM_EPS)


def _dot(a, b):
    return jnp.dot(a, b, preferred_element_type=F32)


def _dot_nt(a, b):
    return lax.dot_general(a, b, (((1,), (1,)), ((), ())), preferred_element_type=F32)


def _dot_tn(a, b):
    return lax.dot_general(a, b, (((0,), (0,)), ((), ())), preferred_element_type=F32)


def _const_spec(shape):
    nd = len(shape)
    return pl.BlockSpec(shape, lambda *_: (0,) * nd, pipeline_mode=pl.Buffered(1))


def _rows(item, size):
    start = item * size
    if not isinstance(start, int):
        start = pl.multiple_of(start, size)
    return pl.ds(start, size)


def _software_pipeline(n_items, stages, unroll=1):
    depth = len(stages)

    def trip(t, static):
        for k in reversed(range(depth)):
            if not static or 0 <= t - k < n_items:
                stages[k](t - k)

    for t in range(depth - 1):
        trip(t, True)

    def steady(t, carry):
        trip(t, False)
        return carry

    lax.fori_loop(depth - 1, n_items, steady, 0, unroll=unroll)
    for t in range(n_items, n_items + depth - 1):
        trip(t, True)


def _stage_w_in(w_ref, w_s):
    z_src = 2 * QK_GLA_W + V_GLA_W + G_GLA_W
    tail = z_src + Z_W
    rows_per_step = 128

    def body(i, carry):
        rows = _rows(i, rows_per_step)
        w_s[rows, 0:QK_GLA_W] = (w_ref[rows, 0:QK_GLA_W] * GLA_DK ** -0.5).astype(BF16)
        w_s[rows, QK_GLA_W:z_src] = w_ref[rows, QK_GLA_W:z_src].astype(BF16)
        w_s[rows, z_src:z_src + Q_SWA_W] = (
            w_ref[rows, tail:tail + Q_SWA_W] * SWA_DH ** -0.5).astype(BF16)
        w_s[rows, z_src + Q_SWA_W:IN_COLS - Z_W] = w_ref[rows, tail + Q_SWA_W:IN_COLS].astype(BF16)
        w_s[rows, IN_COLS - Z_W:IN_COLS] = w_ref[rows, z_src:tail].astype(BF16)
        return carry

    lax.fori_loop(0, D_MODEL // rows_per_step, body, 0)


def _inproj_kernel(x_ref, g_ref, w_ref, *refs):
    out_refs, w_s = refs[:-1], refs[-1]

    @pl.when(pl.program_id(0) == 0)
    def _():
        _stage_w_in(w_ref, w_s)

    u = (_rms(x_ref[...]) * g_ref[...]).astype(BF16)
    off = 0
    for group in IN_DOT_GROUPS:
        width = sum(IN_WIDTHS[i] for i in group)
        y = _dot(u, w_s[:, off:off + width])
        off += width
        col = 0
        for i in group:
            out_refs[i][...] = y[:, col:col + IN_WIDTHS[i]].astype(out_refs[i].dtype)
            col += IN_WIDTHS[i]


def _inproj(x2, g, w):
    t = x2.shape[0]
    row = lambda width: pl.BlockSpec((TOKEN_BLOCK, width), lambda i: (i, 0))
    widths = IN_WIDTHS
    dtypes = (BF16,) * (len(IN_WIDTHS) - 1) + (F32,)
    return pl.pallas_call(
        _inproj_kernel,
        grid=(t // TOKEN_BLOCK,),
        in_specs=[row(D_MODEL), _const_spec((1, D_MODEL)), _const_spec((D_MODEL, IN_COLS))],
        out_specs=[row(wd) for wd in widths],
        out_shape=[jax.ShapeDtypeStruct((t, wd), dt) for wd, dt in zip(widths, dtypes)],
        scratch_shapes=[pltpu.VMEM((D_MODEL, IN_COLS), BF16)],
        compiler_params=pltpu.CompilerParams(
            dimension_semantics=("arbitrary",), vmem_limit_bytes=VMEM_LIMIT),
        name="inproj",
    )(x2, g, w)


def _cumsum_matrix():
    r = np.arange(GLA_GROUP)[:, None]
    s = np.arange(GLA_GROUP)[None, :]
    return (((r // GLA_CHUNK) == (s // GLA_CHUNK)) & (s <= r)).astype(np.float32)


def _gla_kernel(q_ref, k_ref, v_ref, g_ref, z_ref, wgu_ref, bg_ref, nrm_ref, cm_ref, o_ref,
                la_s, hl_s, cs_s, qdec_s, kina_s, kinb_s, kst_s, am_s, ds_s, dec_s, sc_s, *, seq):
    c = GLA_CHUNK
    nc = seq // c
    grp = GLA_GROUP
    cpg = grp // c
    n_groups = seq // grp
    wgu = wgu_ref[0]
    bg = bg_ref[0]
    fwd_p = lax.broadcasted_iota(jnp.int16, (grp, LANES), 1) < HALF
    own_p = fwd_p == (pl.program_id(1) % 2 == 0)
    zero_p = jnp.zeros((grp, LANES), BF16)

    def own_head_twice(tile):
        return jnp.where(own_p, tile, pltpu.roll(tile, HALF, axis=1)).astype(F32)

    fwd_3 = lax.broadcasted_iota(jnp.int32, (cpg, c, LANES), 2) < HALF
    lane_c = lax.broadcasted_iota(jnp.int32, (c, LANES), 1)
    row_c = lax.broadcasted_iota(jnp.int32, (c, LANES), 0)
    tri = ((lane_c < HALF) & (lane_c <= row_c)) | ((lane_c >= HALF) & ((lane_c - HALF) >= row_c))
    zero_c = jnp.zeros((c, LANES), F32)

    def gate_preact(i):
        rows = _rows(i, grp)
        la_s[rows, :] = _dot(z_ref[rows, :].astype(BF16), wgu)

    def decay_logs(i):
        rows = _rows(i, grp)
        pre = la_s[rows, :] + bg
        la = ((jnp.minimum(pre, 0.0) - jnp.log(1.0 + jnp.exp(-jnp.abs(pre))))
              * (1.0 / GLA_GATE_NORM))
        la_hi = la.astype(BF16)
        la_s[rows, :] = la
        hl_s[rows, 0:LANES] = la_hi
        hl_s[rows, LANES:2 * LANES] = (la - la_hi.astype(F32)).astype(BF16)

    def chunk_cumsum(i):
        rows = _rows(i, grp)
        cs_s[rows, :] = _dot(cm_ref[...], hl_s[rows, :])

    def decayed_qk(i):
        rows = _rows(i, grp)
        cs = cs_s[rows, :]
        la3 = la_s[rows, :].reshape(cpg, c, LANES)
        cum3 = (cs[:, :LANES] + cs[:, LANES:]).reshape(cpg, c, LANES)
        tot = jnp.sum(la3, axis=1, keepdims=True)
        b3 = jnp.where(fwd_3, cum3, tot - cum3 + la3)
        b = b3.reshape(grp, LANES)
        rest = jnp.exp(tot - b3).reshape(grp, LANES)
        qq = own_head_twice(q_ref[rows, :])
        kk = own_head_twice(k_ref[rows, :])
        kin = (kk * jnp.exp(-b)).astype(BF16)
        qdec_s[rows, :] = (qq * jnp.exp(b)).astype(BF16)
        kina_s[rows, :] = jnp.where(fwd_p, kin, zero_p)
        kinb_s[rows, :] = jnp.where(fwd_p, zero_p, kin)
        kst_s[rows, :] = (kk * rest).astype(BF16)
        dec_s[_rows(i, cpg)] = jnp.broadcast_to(jnp.exp(tot), (cpg, 8, LANES))

    def chunk_products(i):
        for cc in range(cpg):
            n = i * cpg + cc
            rows = _rows(n, c)
            kin_bd = jnp.concatenate([kina_s[rows, :], kinb_s[rows, :]], axis=0)
            a2 = _dot_nt(qdec_s[rows, :], kin_bd)
            am_s[rows, :] = jnp.where(tri, a2, zero_c).astype(BF16)
            ds_s[n] = _dot_tn(v_ref[rows, :], kst_s[rows, :])

    _software_pipeline(n_groups, [gate_preact, decay_logs, chunk_cumsum, decayed_qk,
                                  chunk_products], unroll=GLA_PREP_UNROLL)

    fwd_sq = lax.broadcasted_iota(jnp.int32, (LANES, LANES), 1) < HALF
    fwd_row = lax.broadcasted_iota(jnp.int32, (8, LANES), 1) < HALF

    def scan_step(i, s):
        j = nc - 1 - i
        sb = s.astype(BF16)
        sc_s[i, :, 0:HALF] = sb[:, 0:HALF]
        sc_s[j, :, HALF:LANES] = sb[:, HALF:LANES]
        dec = jnp.where(fwd_row, dec_s[i], dec_s[j])[0:1]
        return dec * s + jnp.where(fwd_sq, ds_s[i], ds_s[j])

    lax.fori_loop(0, nc, scan_step, jnp.zeros((LANES, LANES), F32), unroll=GLA_UNROLL)

    nrm = nrm_ref[...]

    def mix(i):
        for cc in range(cpg):
            n = i * cpg + cc
            rows = _rows(n, c)
            vc = v_ref[rows, :]
            la_s[rows, :] = (_dot_nt(qdec_s[rows, :], sc_s[n])
                             + _dot(am_s[rows, :], jnp.concatenate([vc, vc], axis=0)))

    def norm_gate(i):
        rows = _rows(i, grp)
        gate = g_ref[rows, :].astype(F32)
        o = (_rms(la_s[rows, :]) * nrm) * (gate / (1.0 + jnp.exp(-gate)))
        o_ref[rows, :] = o.astype(o_ref.dtype)

    _software_pipeline(n_groups, [mix, norm_gate], unroll=GLA_OUT_UNROLL)


def _gla(q, k, v, g, z, wgu, bg, nrm, batch, seq):
    t = batch * seq
    nc = seq // GLA_CHUNK
    tile = pl.BlockSpec((seq, LANES), lambda b, h: (b, h))
    pair_tile = pl.BlockSpec((seq, LANES), lambda b, h: (b, h // 2))
    tok_bf = pltpu.VMEM((seq, LANES), BF16)
    cm = jnp.asarray(_cumsum_matrix(), dtype=BF16)
    return pl.pallas_call(
        functools.partial(_gla_kernel, seq=seq),
        grid=(batch, GLA_HEADS),
        in_specs=[pair_tile, pair_tile, tile, tile,
                  pl.BlockSpec((seq, Z_W), lambda b, h: (b, 0)),
                  pl.BlockSpec((1, Z_W, LANES), lambda b, h: (h, 0, 0)),
                  pl.BlockSpec((1, 1, LANES), lambda b, h: (h, 0, 0)),
                  pl.BlockSpec((1, LANES), lambda b, h: (0, 0)),
                  _const_spec(cm.shape)],
        out_specs=tile,
        out_shape=jax.ShapeDtypeStruct((t, V_GLA_W), BF16),
        scratch_shapes=[pltpu.VMEM((seq, LANES), F32),
                        pltpu.VMEM((seq, 2 * LANES), BF16),
                        pltpu.VMEM((seq, 2 * LANES), F32),
                        tok_bf, tok_bf, tok_bf, tok_bf, tok_bf,
                        pltpu.VMEM((nc, LANES, LANES), F32),
                        pltpu.VMEM((nc, 8, LANES), F32),
                        pltpu.VMEM((nc, LANES, LANES), BF16)],
        compiler_params=pltpu.CompilerParams(
            dimension_semantics=("arbitrary", "arbitrary"), vmem_limit_bytes=VMEM_LIMIT),
        name="gla",
    )(q, k, v, g, z, wgu, bg, nrm, cm)


def _t5_buckets(rel):
    nb = REL_BUCKETS // 2
    ret = (rel > 0).astype(np.int32) * nb
    n = np.abs(rel)
    max_exact = nb // 2
    large = max_exact + (np.log(np.maximum(n, 1).astype(np.float32) / max_exact)
                         / math.log(REL_MAX_DIST / max_exact) * (nb - max_exact)).astype(np.int32)
    large = np.minimum(large, nb - 1)
    return ret + np.where(n < max_exact, n, large)


def _bucket_map():
    w = SWA_BLOCK
    cq = np.arange(w)[:, None]
    s = np.arange(3 * w)[None, :]
    rel = s - w - cq
    return np.where(np.abs(rel) <= SWA_WINDOW, _t5_buckets(rel), -1).astype(np.int32)


def _bias_kernel(bm_ref, tab_ref, o_ref):
    w = SWA_BLOCK
    pair = pl.program_id(0)
    bm = bm_ref[...]
    key = lax.broadcasted_iota(jnp.int32, bm.shape, 1)
    masked = jnp.full(bm.shape, MASK_VALUE, F32)
    for hh in range(2):
        head = 2 * pair + hh
        acc = masked
        for bkt in range(REL_BUCKETS):
            acc = jnp.where(bm == bkt, tab_ref[bkt, head], acc)
        hcols = slice(hh * 3 * w, (hh + 1) * 3 * w)
        o_ref[0, 0, :, hcols] = jnp.where(key >= w, acc, masked)
        o_ref[1, 0, :, hcols] = acc
        o_ref[2, 0, :, hcols] = jnp.where(key < 2 * w, acc, masked)


def _bias_tables(rel_bias):
    w = SWA_BLOCK
    pairs = SWA_Q_HEADS // 2
    return pl.pallas_call(
        _bias_kernel,
        grid=(pairs,),
        in_specs=[pl.BlockSpec((w, 3 * w), lambda p: (0, 0)),
                  pl.BlockSpec(memory_space=pltpu.SMEM)],
        out_specs=pl.BlockSpec((3, 1, w, 6 * w), lambda p: (0, p, 0, 0)),
        out_shape=jax.ShapeDtypeStruct((3, pairs, w, 6 * w), F32),
        compiler_params=pltpu.CompilerParams(dimension_semantics=("arbitrary",)),
        name="swa_bias",
    )(jnp.asarray(_bucket_map()), rel_bias.astype(F32))


def _swa_kernel(q_ref, k_ref, v_ref, bias_ref, sink_ref, o_ref, ka_s, kb_s, va_s, vb_s,
                lg_s, p_s, st_s, *, seq):
    w = SWA_BLOCK
    nb = seq // w
    kvh = pl.program_id(1)
    pairs = SWA_Q_HEADS // SWA_KV_HEADS // 2
    stage_rows = 512
    lane_s = lax.broadcasted_iota(jnp.int16, (stage_rows, LANES), 1)
    lo = lane_s < HALF
    own = lo == (kvh == 0)
    for s_ref in (ka_s, kb_s, va_s, vb_s):
        zpad = jnp.zeros((w, s_ref.shape[1]), BF16)
        s_ref[0:w, :] = zpad
        s_ref[w + seq:2 * w + seq, :] = zpad

    def stage(i, carry):
        src = pl.ds(pl.multiple_of(i * stage_rows, stage_rows), stage_rows)
        dst = pl.ds(pl.multiple_of(i * stage_rows + w, w), stage_rows)
        k2 = k_ref[src, :]
        v2 = v_ref[src, :]
        kk = jnp.where(own, k2, pltpu.roll(k2, HALF, axis=1))
        vv = jnp.where(own, v2, pltpu.roll(v2, HALF, axis=1))
        zero = jnp.zeros_like(kk)
        one = jnp.ones_like(kk)
        ka_s[dst, :] = jnp.where(lo, kk, zero)
        kb_s[dst, :] = jnp.where(lo, zero, kk)
        va_s[dst, 0:LANES] = jnp.where(lo, vv, zero)
        va_s[dst, LANES:2 * LANES] = jnp.where(lo, one, zero)
        vb_s[dst, 0:LANES] = jnp.where(lo, zero, vv)
        vb_s[dst, LANES:2 * LANES] = jnp.where(lo, zero, one)
        return carry

    lax.fori_loop(0, seq // stage_rows, stage, 0)

    lo_w = lax.broadcasted_iota(jnp.int32, (w, LANES), 1) < HALF

    def logits_stage(n, variant):
        qrows = pl.ds(pl.multiple_of(n * w, w), w)
        band = pl.ds(pl.multiple_of(n * w, w), 3 * w)
        kt = jnp.concatenate([ka_s[band, :], kb_s[band, :]], axis=0)
        for pair in range(pairs):
            cols = slice(pair * LANES, (pair + 1) * LANES)
            lg_s[pair] = _dot_nt(q_ref[qrows, cols], kt) + bias_ref[variant, pair]

    def softmax_stage():
        for pair in range(pairs):
            sink_terms = []
            for hh in range(2):
                hcols = slice(hh * 3 * w, (hh + 1) * 3 * w)
                lg = lg_s[pair, :, hcols]
                sink = sink_ref[kvh * 2 * pairs + 2 * pair + hh]
                m = jnp.maximum(jnp.max(lg, axis=-1, keepdims=True), sink)
                p_s[pair, :, hcols] = jnp.exp(lg - m).astype(BF16)
                sink_terms.append(jnp.exp(sink - m))
            st_s[pair] = jnp.where(lo_w, sink_terms[0], sink_terms[1])

    def output_stage(n):
        qrows = pl.ds(pl.multiple_of(n * w, w), w)
        band = pl.ds(pl.multiple_of(n * w, w), 3 * w)
        vx = jnp.concatenate([va_s[band, :], vb_s[band, :]], axis=0)
        for pair in range(pairs):
            ox = _dot(p_s[pair], vx)
            o = ox[:, :LANES] / (ox[:, LANES:] + st_s[pair])
            o_ref[qrows, pair * LANES:(pair + 1) * LANES] = o.astype(o_ref.dtype)

    logits_stage(0, 0)
    softmax_stage()
    logits_stage(1, 1)

    def steady(n, carry):
        output_stage(n - 1)
        softmax_stage()
        logits_stage(n + 1, 1)
        return carry

    lax.fori_loop(1, nb - 2, steady, 0, unroll=2)
    output_stage(nb - 3)
    softmax_stage()
    logits_stage(nb - 1, 2)
    output_stage(nb - 2)
    softmax_stage()
    output_stage(nb - 1)


def _swa(q, k, v, bias, sink, batch, seq):
    t = batch * seq
    w = SWA_BLOCK
    qw = Q_SWA_W // SWA_KV_HEADS
    pairs = SWA_Q_HEADS // SWA_KV_HEADS // 2
    staged_k = pltpu.VMEM((seq + 2 * w, LANES), BF16)
    staged_v = pltpu.VMEM((seq + 2 * w, 2 * LANES), BF16)
    return pl.pallas_call(
        functools.partial(_swa_kernel, seq=seq),
        grid=(batch, SWA_KV_HEADS),
        in_specs=[pl.BlockSpec((seq, qw), lambda b, j: (b, j)),
                  pl.BlockSpec((seq, LANES), lambda b, j: (b, 0)),
                  pl.BlockSpec((seq, LANES), lambda b, j: (b, 0)),
                  pl.BlockSpec((3, pairs, w, 6 * w), lambda b, j: (0, j, 0, 0)),
                  pl.BlockSpec(memory_space=pltpu.SMEM)],
        out_specs=pl.BlockSpec((seq, qw), lambda b, j: (b, j)),
        out_shape=jax.ShapeDtypeStruct((t, Q_SWA_W), BF16),
        scratch_shapes=[staged_k, staged_k, staged_v, staged_v,
                        pltpu.VMEM((pairs, w, 6 * w), F32),
                        pltpu.VMEM((pairs, w, 6 * w), BF16),
                        pltpu.VMEM((pairs, w, LANES), F32)],
        compiler_params=pltpu.CompilerParams(
            dimension_semantics=("arbitrary", "arbitrary"), vmem_limit_bytes=VMEM_LIMIT),
        name="swa",
    )(q, k, v, bias, sink)


def _ffn_kernel(x_ref, oa_ref, os_ref, wa_ref, ws_ref, gpost_ref, gpre_ref, wup_ref, wdn_ref,
                gout_ref, o_ref):
    slab = FFN_TOKEN_BLOCK // FFN_SLABS
    slabs = [slice(s * slab, (s + 1) * slab) for s in range(FFN_SLABS)]
    mixes = [_dot(oa_ref[rows, :], wa_ref[...]) + _dot(os_ref[rows, :], ws_ref[...])
             for rows in slabs]
    for rows, mix in zip(slabs, mixes):
        h = x_ref[rows, :] + _rms(mix) * gpost_ref[...]
        hn = (_rms(h) * gpre_ref[...]).astype(BF16)
        acc = jnp.zeros(h.shape, F32)
        for j in range(D_FF // FF_BLOCK):
            cols = slice(j * FF_BLOCK, (j + 1) * FF_BLOCK)
            zj = jnp.maximum(_dot(hn, wup_ref[:, cols]), 0.0)
            acc = acc + _dot((zj * zj).astype(BF16), wdn_ref[cols, :])
        o_ref[rows, :] = h + _rms(acc) * gout_ref[...]


def _ffn(x2, oa, os_, wa, ws, gpost, gpre, wup, wdn, gout):
    t = x2.shape[0]
    row = lambda width: pl.BlockSpec((FFN_TOKEN_BLOCK, width), lambda i: (i, 0))
    return pl.pallas_call(
        _ffn_kernel,
        grid=(t // FFN_TOKEN_BLOCK,),
        in_specs=[row(D_MODEL), row(V_GLA_W), row(Q_SWA_W),
                  _const_spec((V_GLA_W, D_MODEL)), _const_spec((Q_SWA_W, D_MODEL)),
                  _const_spec((1, D_MODEL)), _const_spec((1, D_MODEL)),
                  _const_spec((D_MODEL, D_FF)), _const_spec((D_FF, D_MODEL)),
                  _const_spec((1, D_MODEL))],
        out_specs=row(D_MODEL),
        out_shape=jax.ShapeDtypeStruct((t, D_MODEL), F32),
        compiler_params=pltpu.CompilerParams(
            dimension_semantics=("arbitrary",), vmem_limit_bytes=VMEM_LIMIT),
        name="outproj_ffn",
    )(x2, oa, os_, wa, ws, gpost, gpre, wup, wdn, gout)


def _layer(h, norm_mix_pre, w_in, w_gu_f, b_g_f, w_gu_b, b_g_b, gla_norm, swa_sink, bias_tab,
           w_out, norm_mix_post, norm_mlp_pre, w_up, w_down, norm_mlp_post):
    batch, seq, _ = h.shape
    x2 = h.reshape(batch * seq, D_MODEL)
    row = lambda g: g.reshape(1, -1).astype(F32)

    qa, ka, va, ga, qs, ks, vs, z = _inproj(x2, row(norm_mix_pre), w_in.astype(F32))

    wf = w_gu_f.reshape(GLA_RANK, GLA_HEADS, GLA_DK).transpose(1, 0, 2)
    wb = w_gu_b.reshape(GLA_RANK, GLA_HEADS, GLA_DK).transpose(1, 0, 2)
    zeros = jnp.zeros_like(wf)
    wgu = jnp.concatenate([jnp.concatenate([wf, zeros], axis=2),
                           jnp.concatenate([zeros, wb], axis=2)], axis=1).astype(BF16)
    bg = jnp.concatenate([b_g_f.reshape(GLA_HEADS, 1, GLA_DK),
                          b_g_b.reshape(GLA_HEADS, 1, GLA_DK)], axis=2).astype(F32)
    o_a = _gla(qa, ka, va, ga, z, wgu, bg, row(gla_norm), batch, seq)

    o_s = _swa(qs, ks, vs, bias_tab, swa_sink.astype(F32), batch, seq)

    out = _ffn(x2, o_a, o_s, w_out[:V_GLA_W].astype(BF16), w_out[V_GLA_W:].astype(BF16),
               row(norm_mix_post), row(norm_mlp_pre), w_up.astype(BF16), w_down.astype(BF16),
               row(norm_mlp_post))
    return out.reshape(batch, seq, D_MODEL)


def kernel(x, norm_mix_pre, w_in, w_gate_up_fwd, b_gate_fwd, w_gate_up_bwd, b_gate_bwd, gla_norm,
           swa_sink, rel_bias, w_out, norm_mix_post, norm_mlp_pre, w_up, w_down, norm_mlp_post):
    bias_tab = _bias_tables(rel_bias)
    h = x
    for l in range(w_in.shape[0]):
        h = _layer(h, norm_mix_pre[l], w_in[l], w_gate_up_fwd[l], b_gate_fwd[l], w_gate_up_bwd[l],
                   b_gate_bwd[l], gla_norm[l], swa_sink[l], bias_tab, w_out[l], norm_mix_post[l],
                   norm_mlp_pre[l], w_up[l], w_down[l], norm_mlp_post[l])
    return h
```

```python
import functools
import math

import jax
import jax.numpy as jnp
import numpy as np
from jax import lax
from jax.experimental import pallas as pl
from jax.experimental.pallas import tpu as pltpu

F32 = jnp.float32
BF16 = jnp.bfloat16

D_MODEL = 1024
GLA_HEADS = 4
GLA_DK = 64
GLA_DV = 128
GLA_RANK = 16
GLA_GATE_NORM = 16.0
GLA_CHUNK = 64
SWA_DH = 64
SWA_Q_HEADS = 8
SWA_KV_HEADS = 2
SWA_BLOCK = 128
SWA_WINDOW = 128
REL_BUCKETS = 32
REL_MAX_DIST = 128
D_FF = 4 * D_MODEL
NORM_EPS = 1e-6
MASK_VALUE = -1e30

LANES = 128
HALF = LANES // 2

QK_GLA_W = GLA_HEADS * GLA_DK
V_GLA_W = GLA_HEADS * GLA_DV
G_GLA_W = GLA_HEADS * GLA_DV
Q_SWA_W = SWA_Q_HEADS * SWA_DH
KV_SWA_W = SWA_KV_HEADS * SWA_DH
Z_W = 2 * GLA_RANK
IN_WIDTHS = (QK_GLA_W, QK_GLA_W, V_GLA_W, G_GLA_W, Q_SWA_W, KV_SWA_W, KV_SWA_W, Z_W)
IN_COLS = sum(IN_WIDTHS)
IN_DOT_GROUPS = ((0, 1), (2,), (3,), (4,), (5, 6), (7,))

GLA_UNROLL = 8
GLA_GROUP = 256
GLA_PREP_UNROLL = 6
GLA_OUT_UNROLL = 15
TOKEN_BLOCK = 1024
FFN_TOKEN_BLOCK = 1024
FF_BLOCK = 1024
FFN_SLABS = 2
VMEM_LIMIT = 56 * 1024 * 1024


def _rms(x):
    return x * lax.rsqrt(jnp.mean(x * x, axis=-1, keepdims=True) + NORM_EPS)


def _dot(a, b):
    return jnp.dot(a, b, preferred_element_type=F32)


def _dot_nt(a, b):
    return lax.dot_general(a, b, (((1,), (1,)), ((), ())), preferred_element_type=F32)


def _dot_tn(a, b):
    return lax.dot_general(a, b, (((0,), (0,)), ((), ())), preferred_element_type=F32)


def _const_spec(shape):
    nd = len(shape)
    return pl.BlockSpec(shape, lambda *_: (0,) * nd, pipeline_mode=pl.Buffered(1))


def _rows(item, size):
    start = item * size
    if not isinstance(start, int):
        start = pl.multiple_of(start, size)
    return pl.ds(start, size)


def _software_pipeline(n_items, stages, unroll=1):
    depth = len(stages)

    def trip(t, static):
        for k in reversed(range(depth)):
            if not static or 0 <= t - k < n_items:
                stages[k](t - k)

    for t in range(depth - 1):
        trip(t, True)

    def steady(t, carry):
        trip(t, False)
        return carry

    lax.fori_loop(depth - 1, n_items, steady, 0, unroll=unroll)
    for t in range(n_items, n_items + depth - 1):
        trip(t, True)


def _stage_w_in(wt_ref, w_s):
    z_src = 2 * QK_GLA_W + V_GLA_W + G_GLA_W
    tail = z_src + Z_W
    step = 128

    def copy_rows(src, dst, n_rows, scale):
        def body(i, carry):
            w = wt_ref[pl.ds(src + i * step, step), :]
            w_s[pl.ds(dst + i * step, step), :] = (w if scale == 1.0 else w * scale).astype(BF16)
            return carry
        lax.fori_loop(0, n_rows // step, body, 0)

    copy_rows(0, 0, QK_GLA_W, GLA_DK ** -0.5)
    copy_rows(QK_GLA_W, QK_GLA_W, z_src - QK_GLA_W, 1.0)
    copy_rows(tail, z_src, Q_SWA_W, SWA_DH ** -0.5)
    copy_rows(tail + Q_SWA_W, z_src + Q_SWA_W, 2 * KV_SWA_W, 1.0)
    w_s[IN_COLS - Z_W:IN_COLS, :] = wt_ref[z_src:tail, :].astype(BF16)


def _inproj_kernel(x_ref, g_ref, w_ref, *refs):
    out_refs, w_s = refs[:-1], refs[-1]

    @pl.when(pl.program_id(0) == 0)
    def _():
        _stage_w_in(w_ref, w_s)

    u = (_rms(x_ref[...]) * g_ref[...]).astype(BF16)
    off = 0
    for group in IN_DOT_GROUPS:
        width = sum(IN_WIDTHS[i] for i in group)
        y = _dot_nt(u, w_s[off:off + width, :])
        off += width
        col = 0
        for i in group:
            out_refs[i][...] = y[:, col:col + IN_WIDTHS[i]].astype(out_refs[i].dtype)
            col += IN_WIDTHS[i]


def _inproj(x2, g, w):
    t = x2.shape[0]
    row = lambda width: pl.BlockSpec((TOKEN_BLOCK, width), lambda i: (i, 0))
    widths = IN_WIDTHS
    dtypes = (BF16,) * (len(IN_WIDTHS) - 1) + (F32,)
    return pl.pallas_call(
        _inproj_kernel,
        grid=(t // TOKEN_BLOCK,),
        in_specs=[row(D_MODEL), _const_spec((1, D_MODEL)), _const_spec((IN_COLS, D_MODEL))],
        out_specs=[row(wd) for wd in widths],
        out_shape=[jax.ShapeDtypeStruct((t, wd), dt) for wd, dt in zip(widths, dtypes)],
        scratch_shapes=[pltpu.VMEM((IN_COLS, D_MODEL), BF16)],
        compiler_params=pltpu.CompilerParams(
            dimension_semantics=("arbitrary",), vmem_limit_bytes=VMEM_LIMIT),
        name="inproj",
    )(x2, g, w)


def _cumsum_matrix():
    r = np.arange(GLA_GROUP)[:, None]
    s = np.arange(GLA_GROUP)[None, :]
    return (((r // GLA_CHUNK) == (s // GLA_CHUNK)) & (s <= r)).astype(np.float32)


def _gla_kernel(q_ref, k_ref, v_ref, g_ref, z_ref, wgu_ref, bg_ref, nrm_ref, cm_ref, o_ref,
                la_s, hl_s, cs_s, qdec_s, kina_s, kinb_s, kst_s, am_s, ds_s, dec_s, sc_s, *, seq):
    c = GLA_CHUNK
    nc = seq // c
    grp = GLA_GROUP
    cpg = grp // c
    n_groups = seq // grp
    wgu = wgu_ref[0]
    bg = bg_ref[0]
    fwd_p = lax.broadcasted_iota(jnp.int16, (grp, LANES), 1) < HALF
    own_p = fwd_p == (pl.program_id(1) % 2 == 0)
    zero_p = jnp.zeros((grp, LANES), BF16)

    def own_head_twice(tile):
        return jnp.where(own_p, tile, pltpu.roll(tile, HALF, axis=1)).astype(F32)

    fwd_3 = lax.broadcasted_iota(jnp.int32, (cpg, c, LANES), 2) < HALF
    lane_c = lax.broadcasted_iota(jnp.int32, (c, LANES), 1)
    row_c = lax.broadcasted_iota(jnp.int32, (c, LANES), 0)
    tri = ((lane_c < HALF) & (lane_c <= row_c)) | ((lane_c >= HALF) & ((lane_c - HALF) >= row_c))
    zero_c = jnp.zeros((c, LANES), F32)

    def gate_preact(i):
        rows = _rows(i, grp)
        la_s[rows, :] = _dot(z_ref[rows, :].astype(BF16), wgu)

    def decay_logs(i):
        rows = _rows(i, grp)
        pre = la_s[rows, :] + bg
        la = ((jnp.minimum(pre, 0.0) - jnp.log(1.0 + jnp.exp(-jnp.abs(pre))))
              * (1.0 / GLA_GATE_NORM))
        la_hi = la.astype(BF16)
        la_s[rows, :] = la
        hl_s[rows, 0:LANES] = la_hi
        hl_s[rows, LANES:2 * LANES] = (la - la_hi.astype(F32)).astype(BF16)

    def chunk_cumsum(i):
        rows = _rows(i, grp)
        cs_s[rows, :] = _dot(cm_ref[...], hl_s[rows, :])

    def decayed_qk(i):
        rows = _rows(i, grp)
        cs = cs_s[rows, :]
        la3 = la_s[rows, :].reshape(cpg, c, LANES)
        cum3 = (cs[:, :LANES] + cs[:, LANES:]).reshape(cpg, c, LANES)
        tot = jnp.sum(la3, axis=1, keepdims=True)
        b3 = jnp.where(fwd_3, cum3, tot - cum3 + la3)
        b = b3.reshape(grp, LANES)
        rest = jnp.exp(tot - b3).reshape(grp, LANES)
        qq = own_head_twice(q_ref[rows, :])
        kk = own_head_twice(k_ref[rows, :])
        kin = (kk * jnp.exp(-b)).astype(BF16)
        qdec_s[rows, :] = (qq * jnp.exp(b)).astype(BF16)
        kina_s[rows, :] = jnp.where(fwd_p, kin, zero_p)
        kinb_s[rows, :] = jnp.where(fwd_p, zero_p, kin)
        kst_s[rows, :] = (kk * rest).astype(BF16)
        dec_s[_rows(i, cpg)] = jnp.broadcast_to(jnp.exp(tot), (cpg, 8, LANES))

    def chunk_products(i):
        for cc in range(cpg):
            n = i * cpg + cc
            rows = _rows(n, c)
            kin_bd = jnp.concatenate([kina_s[rows, :], kinb_s[rows, :]], axis=0)
            a2 = _dot_nt(qdec_s[rows, :], kin_bd)
            am_s[rows, :] = jnp.where(tri, a2, zero_c).astype(BF16)
            ds_s[n] = _dot_tn(v_ref[rows, :], kst_s[rows, :])

    _software_pipeline(n_groups, [gate_preact, decay_logs, chunk_cumsum, decayed_qk,
                                  chunk_products], unroll=GLA_PREP_UNROLL)

    fwd_sq = lax.broadcasted_iota(jnp.int32, (LANES, LANES), 1) < HALF
    fwd_row = lax.broadcasted_iota(jnp.int32, (8, LANES), 1) < HALF

    def scan_step(i, s):
        j = nc - 1 - i
        sb = s.astype(BF16)
        sc_s[i, :, 0:HALF] = sb[:, 0:HALF]
        sc_s[j, :, HALF:LANES] = sb[:, HALF:LANES]
        dec = jnp.where(fwd_row, dec_s[i], dec_s[j])[0:1]
        return dec * s + jnp.where(fwd_sq, ds_s[i], ds_s[j])

    lax.fori_loop(0, nc, scan_step, jnp.zeros((LANES, LANES), F32), unroll=GLA_UNROLL)

    nrm = nrm_ref[...]

    def mix(i):
        for cc in range(cpg):
            n = i * cpg + cc
            rows = _rows(n, c)
            vc = v_ref[rows, :]
            la_s[rows, :] = (_dot_nt(qdec_s[rows, :], sc_s[n])
                             + _dot(am_s[rows, :], jnp.concatenate([vc, vc], axis=0)))

    def norm_gate(i):
        rows = _rows(i, grp)
        gate = g_ref[rows, :].astype(F32)
        o = (_rms(la_s[rows, :]) * nrm) * (gate / (1.0 + jnp.exp(-gate)))
        o_ref[rows, :] = o.astype(o_ref.dtype)

    _software_pipeline(n_groups, [mix, norm_gate], unroll=GLA_OUT_UNROLL)


def _gla(q, k, v, g, z, wgu, bg, nrm, batch, seq):
    t = batch * seq
    nc = seq // GLA_CHUNK
    tile = pl.BlockSpec((seq, LANES), lambda b, h: (b, h))
    pair_tile = pl.BlockSpec((seq, LANES), lambda b, h: (b, h // 2))
    tok_bf = pltpu.VMEM((seq, LANES), BF16)
    cm = jnp.asarray(_cumsum_matrix(), dtype=BF16)
    return pl.pallas_call(
        functools.partial(_gla_kernel, seq=seq),
        grid=(batch, GLA_HEADS),
        in_specs=[pair_tile, pair_tile, tile, tile,
                  pl.BlockSpec((seq, Z_W), lambda b, h: (b, 0)),
                  pl.BlockSpec((1, Z_W, LANES), lambda b, h: (h, 0, 0)),
                  pl.BlockSpec((1, 1, LANES), lambda b, h: (h, 0, 0)),
                  pl.BlockSpec((1, LANES), lambda b, h: (0, 0)),
                  _const_spec(cm.shape)],
        out_specs=tile,
        out_shape=jax.ShapeDtypeStruct((t, V_GLA_W), BF16),
        scratch_shapes=[pltpu.VMEM((seq, LANES), F32),
                        pltpu.VMEM((seq, 2 * LANES), BF16),
                        pltpu.VMEM((seq, 2 * LANES), F32),
                        tok_bf, tok_bf, tok_bf, tok_bf, tok_bf,
                        pltpu.VMEM((nc, LANES, LANES), F32),
                        pltpu.VMEM((nc, 8, LANES), F32),
                        pltpu.VMEM((nc, LANES, LANES), BF16)],
        compiler_params=pltpu.CompilerParams(
            dimension_semantics=("arbitrary", "arbitrary"), vmem_limit_bytes=VMEM_LIMIT),
        name="gla",
    )(q, k, v, g, z, wgu, bg, nrm, cm)


def _t5_buckets(rel):
    nb = REL_BUCKETS // 2
    ret = (rel > 0).astype(np.int32) * nb
    n = np.abs(rel)
    max_exact = nb // 2
    large = max_exact + (np.log(np.maximum(n, 1).astype(np.float32) / max_exact)
                         / math.log(REL_MAX_DIST / max_exact) * (nb - max_exact)).astype(np.int32)
    large = np.minimum(large, nb - 1)
    return ret + np.where(n < max_exact, n, large)


def _bucket_map():
    w = SWA_BLOCK
    cq = np.arange(w)[:, None]
    s = np.arange(3 * w)[None, :]
    rel = s - w - cq
    return np.where(np.abs(rel) <= SWA_WINDOW, _t5_buckets(rel), -1).astype(np.int32)


def _bias_kernel(bm_ref, tab_ref, o_ref):
    w = SWA_BLOCK
    pair = pl.program_id(0)
    bm = bm_ref[...]
    key = lax.broadcasted_iota(jnp.int32, bm.shape, 1)
    masked = jnp.full(bm.shape, MASK_VALUE, F32)
    for hh in range(2):
        head = 2 * pair + hh
        acc = masked
        for bkt in range(REL_BUCKETS):
            acc = jnp.where(bm == bkt, tab_ref[bkt, head], acc)
        hcols = slice(hh * 3 * w, (hh + 1) * 3 * w)
        o_ref[0, 0, :, hcols] = jnp.where(key >= w, acc, masked)
        o_ref[1, 0, :, hcols] = acc
        o_ref[2, 0, :, hcols] = jnp.where(key < 2 * w, acc, masked)


def _bias_tables(rel_bias):
    w = SWA_BLOCK
    pairs = SWA_Q_HEADS // 2
    return pl.pallas_call(
        _bias_kernel,
        grid=(pairs,),
        in_specs=[pl.BlockSpec((w, 3 * w), lambda p: (0, 0)),
                  pl.BlockSpec(memory_space=pltpu.SMEM)],
        out_specs=pl.BlockSpec((3, 1, w, 6 * w), lambda p: (0, p, 0, 0)),
        out_shape=jax.ShapeDtypeStruct((3, pairs, w, 6 * w), F32),
        compiler_params=pltpu.CompilerParams(dimension_semantics=("arbitrary",)),
        name="swa_bias",
    )(jnp.asarray(_bucket_map()), rel_bias.astype(F32))


def _swa_kernel(q_ref, k_ref, v_ref, bias_ref, sink_ref, o_ref, ka_s, kb_s, va_s, vb_s,
                lg_s, p_s, st_s, *, seq):
    w = SWA_BLOCK
    nb = seq // w
    kvh = pl.program_id(1)
    pairs = SWA_Q_HEADS // SWA_KV_HEADS // 2
    stage_rows = 512
    lane_s = lax.broadcasted_iota(jnp.int16, (stage_rows, LANES), 1)
    lo = lane_s < HALF
    own = lo == (kvh == 0)
    for s_ref in (ka_s, kb_s, va_s, vb_s):
        zpad = jnp.zeros((w, s_ref.shape[1]), BF16)
        s_ref[0:w, :] = zpad
        s_ref[w + seq:2 * w + seq, :] = zpad

    def stage(i, carry):
        src = pl.ds(pl.multiple_of(i * stage_rows, stage_rows), stage_rows)
        dst = pl.ds(pl.multiple_of(i * stage_rows + w, w), stage_rows)
        k2 = k_ref[src, :]
        v2 = v_ref[src, :]
        kk = jnp.where(own, k2, pltpu.roll(k2, HALF, axis=1))
        vv = jnp.where(own, v2, pltpu.roll(v2, HALF, axis=1))
        zero = jnp.zeros_like(kk)
        one = jnp.ones_like(kk)
        ka_s[dst, :] = jnp.where(lo, kk, zero)
        kb_s[dst, :] = jnp.where(lo, zero, kk)
        va_s[dst, 0:LANES] = jnp.where(lo, vv, zero)
        va_s[dst, LANES:2 * LANES] = jnp.where(lo, one, zero)
        vb_s[dst, 0:LANES] = jnp.where(lo, zero, vv)
        vb_s[dst, LANES:2 * LANES] = jnp.where(lo, zero, one)
        return carry

    lax.fori_loop(0, seq // stage_rows, stage, 0)

    lo_w = lax.broadcasted_iota(jnp.int32, (w, LANES), 1) < HALF

    def logits_stage(n, variant):
        qrows = pl.ds(pl.multiple_of(n * w, w), w)
        band = pl.ds(pl.multiple_of(n * w, w), 3 * w)
        kt = jnp.concatenate([ka_s[band, :], kb_s[band, :]], axis=0)
        for pair in range(pairs):
            cols = slice(pair * LANES, (pair + 1) * LANES)
            lg_s[pair] = _dot_nt(q_ref[qrows, cols], kt) + bias_ref[variant, pair]

    def softmax_stage():
        for pair in range(pairs):
            sink_terms = []
            for hh in range(2):
                hcols = slice(hh * 3 * w, (hh + 1) * 3 * w)
                lg = lg_s[pair, :, hcols]
                sink = sink_ref[kvh * 2 * pairs + 2 * pair + hh]
                m = jnp.maximum(jnp.max(lg, axis=-1, keepdims=True), sink)
                p_s[pair, :, hcols] = jnp.exp(lg - m).astype(BF16)
                sink_terms.append(jnp.exp(sink - m))
            st_s[pair] = jnp.where(lo_w, sink_terms[0], sink_terms[1])

    def output_stage(n):
        qrows = pl.ds(pl.multiple_of(n * w, w), w)
        band = pl.ds(pl.multiple_of(n * w, w), 3 * w)
        vx = jnp.concatenate([va_s[band, :], vb_s[band, :]], axis=0)
        for pair in range(pairs):
            ox = _dot(p_s[pair], vx)
            o = ox[:, :LANES] / (ox[:, LANES:] + st_s[pair])
            o_ref[qrows, pair * LANES:(pair + 1) * LANES] = o.astype(o_ref.dtype)

    logits_stage(0, 0)
    softmax_stage()
    logits_stage(1, 1)

    def steady(n, carry):
        output_stage(n - 1)
        softmax_stage()
        logits_stage(n + 1, 1)
        return carry

    lax.fori_loop(1, nb - 2, steady, 0, unroll=2)
    output_stage(nb - 3)
    softmax_stage()
    logits_stage(nb - 1, 2)
    output_stage(nb - 2)
    softmax_stage()
    output_stage(nb - 1)


def _swa(q, k, v, bias, sink, batch, seq):
    t = batch * seq
    w = SWA_BLOCK
    qw = Q_SWA_W // SWA_KV_HEADS
    pairs = SWA_Q_HEADS // SWA_KV_HEADS // 2
    staged_k = pltpu.VMEM((seq + 2 * w, LANES), BF16)
    staged_v = pltpu.VMEM((seq + 2 * w, 2 * LANES), BF16)
    return pl.pallas_call(
        functools.partial(_swa_kernel, seq=seq),
        grid=(batch, SWA_KV_HEADS),
        in_specs=[pl.BlockSpec((seq, qw), lambda b, j: (b, j)),
                  pl.BlockSpec((seq, LANES), lambda b, j: (b, 0)),
                  pl.BlockSpec((seq, LANES), lambda b, j: (b, 0)),
                  pl.BlockSpec((3, pairs, w, 6 * w), lambda b, j: (0, j, 0, 0)),
                  pl.BlockSpec(memory_space=pltpu.SMEM)],
        out_specs=pl.BlockSpec((seq, qw), lambda b, j: (b, j)),
        out_shape=jax.ShapeDtypeStruct((t, Q_SWA_W), BF16),
        scratch_shapes=[staged_k, staged_k, staged_v, staged_v,
                        pltpu.VMEM((pairs, w, 6 * w), F32),
                        pltpu.VMEM((pairs, w, 6 * w), BF16),
                        pltpu.VMEM((pairs, w, LANES), F32)],
        compiler_params=pltpu.CompilerParams(
            dimension_semantics=("arbitrary", "arbitrary"), vmem_limit_bytes=VMEM_LIMIT),
        name="swa",
    )(q, k, v, bias, sink)


def _ffn_kernel(x_ref, oa_ref, os_ref, wa_ref, ws_ref, gpost_ref, gpre_ref, wup_ref, wdn_ref,
                gout_ref, o_ref):
    slab = FFN_TOKEN_BLOCK // FFN_SLABS
    slabs = [slice(s * slab, (s + 1) * slab) for s in range(FFN_SLABS)]
    mixes = [_dot(oa_ref[rows, :], wa_ref[...]) + _dot(os_ref[rows, :], ws_ref[...])
             for rows in slabs]
    for rows, mix in zip(slabs, mixes):
        h = x_ref[rows, :] + _rms(mix) * gpost_ref[...]
        hn = (_rms(h) * gpre_ref[...]).astype(BF16)
        acc = jnp.zeros(h.shape, F32)
        for j in range(D_FF // FF_BLOCK):
            cols = slice(j * FF_BLOCK, (j + 1) * FF_BLOCK)
            zj = jnp.maximum(_dot(hn, wup_ref[:, cols]), 0.0)
            acc = acc + _dot((zj * zj).astype(BF16), wdn_ref[cols, :])
        o_ref[rows, :] = h + _rms(acc) * gout_ref[...]


def _ffn(x2, oa, os_, wa, ws, gpost, gpre, wup, wdn, gout):
    t = x2.shape[0]
    row = lambda width: pl.BlockSpec((FFN_TOKEN_BLOCK, width), lambda i: (i, 0))
    return pl.pallas_call(
        _ffn_kernel,
        grid=(t // FFN_TOKEN_BLOCK,),
        in_specs=[row(D_MODEL), row(V_GLA_W), row(Q_SWA_W),
                  _const_spec((V_GLA_W, D_MODEL)), _const_spec((Q_SWA_W, D_MODEL)),
                  _const_spec((1, D_MODEL)), _const_spec((1, D_MODEL)),
                  _const_spec((D_MODEL, D_FF)), _const_spec((D_FF, D_MODEL)),
                  _const_spec((1, D_MODEL))],
        out_specs=row(D_MODEL),
        out_shape=jax.ShapeDtypeStruct((t, D_MODEL), F32),
        compiler_params=pltpu.CompilerParams(
            dimension_semantics=("arbitrary",), vmem_limit_bytes=VMEM_LIMIT),
        name="outproj_ffn",
    )(x2, oa, os_, wa, ws, gpost, gpre, wup, wdn, gout)


def _layer(h, norm_mix_pre, w_in, w_gu_f, b_g_f, w_gu_b, b_g_b, gla_norm, swa_sink, bias_tab,
           w_out, norm_mix_post, norm_mlp_pre, w_up, w_down, norm_mlp_post):
    batch, seq, _ = h.shape
    x2 = h.reshape(batch * seq, D_MODEL)
    row = lambda g: g.reshape(1, -1).astype(F32)

    qa, ka, va, ga, qs, ks, vs, z = _inproj(x2, row(norm_mix_pre), w_in.astype(F32).T)

    wf = w_gu_f.reshape(GLA_RANK, GLA_HEADS, GLA_DK).transpose(1, 0, 2)
    wb = w_gu_b.reshape(GLA_RANK, GLA_HEADS, GLA_DK).transpose(1, 0, 2)
    zeros = jnp.zeros_like(wf)
    wgu = jnp.concatenate([jnp.concatenate([wf, zeros], axis=2),
                           jnp.concatenate([zeros, wb], axis=2)], axis=1).astype(BF16)
    bg = jnp.concatenate([b_g_f.reshape(GLA_HEADS, 1, GLA_DK),
                          b_g_b.reshape(GLA_HEADS, 1, GLA_DK)], axis=2).astype(F32)
    o_a = _gla(qa, ka, va, ga, z, wgu, bg, row(gla_norm), batch, seq)

    o_s = _swa(qs, ks, vs, bias_tab, swa_sink.astype(F32), batch, seq)

    out = _ffn(x2, o_a, o_s, w_out[:V_GLA_W].astype(BF16), w_out[V_GLA_W:].astype(BF16),
               row(norm_mix_post), row(norm_mlp_pre), w_up.astype(BF16), w_down.astype(BF16),
               row(norm_mlp_post))
    return out.reshape(batch, seq, D_MODEL)


def kernel(x, norm_mix_pre, w_in, w_gate_up_fwd, b_gate_fwd, w_gate_up_bwd, b_gate_bwd, gla_norm,
           swa_sink, rel_bias, w_out, norm_mix_post, norm_mlp_pre, w_up, w_down, norm_mlp_post):
    bias_tab = _bias_tables(rel_bias)
    h = x
    for l in range(w_in.shape[0]):
        h = _layer(h, norm_mix_pre[l], w_in[l], w_gate_up_fwd[l], b_gate_fwd[l], w_gate_up_bwd[l],
                   b_gate_bwd[l], gla_norm[l], swa_sink[l], bias_tab, w_out[l], norm_mix_post[l],
                   norm_mlp_pre[l], w_up[l], w_down[l], norm_mlp_post[l])
    return h
```

```python
import functools
import math

import jax
import jax.numpy as jnp
import numpy as np
from jax import lax
from jax.experimental import pallas as pl
from jax.experimental.pallas import tpu as pltpu

F32 = jnp.float32
BF16 = jnp.bfloat16

D_MODEL = 1024
GLA_HEADS = 4
GLA_DK = 64
GLA_DV = 128
GLA_RANK = 16
GLA_GATE_NORM = 16.0
GLA_CHUNK = 64
SWA_DH = 64
SWA_Q_HEADS = 8
SWA_KV_HEADS = 2
SWA_BLOCK = 128
SWA_WINDOW = 128
REL_BUCKETS = 32
REL_MAX_DIST = 128
D_FF = 4 * D_MODEL
NORM_EPS = 1e-6
MASK_VALUE = -1e30

LANES = 128
HALF = LANES // 2

QK_GLA_W = GLA_HEADS * GLA_DK
V_GLA_W = GLA_HEADS * GLA_DV
G_GLA_W = GLA_HEADS * GLA_DV
Q_SWA_W = SWA_Q_HEADS * SWA_DH
KV_SWA_W = SWA_KV_HEADS * SWA_DH
Z_W = 2 * GLA_RANK
IN_WIDTHS = (QK_GLA_W, QK_GLA_W, V_GLA_W, G_GLA_W, Q_SWA_W, KV_SWA_W, KV_SWA_W, Z_W)
IN_COLS = sum(IN_WIDTHS)
IN_DOT_GROUPS = ((0, 1), (2,), (3,), (4,), (5, 6), (7,))
N_LATER_WEIGHTS = 3

GLA_UNROLL = 8
GLA_GROUP = 256
GLA_PREP_UNROLL = 6
GLA_OUT_UNROLL = 15
TOKEN_BLOCK = 1024
FFN_TOKEN_BLOCK = 1024
FF_BLOCK = 1024
FFN_SLABS = 4
VMEM_LIMIT = 56 * 1024 * 1024


def _rms(x):
    return x * lax.rsqrt(jnp.mean(x * x, axis=-1, keepdims=True) + NORM_EPS)


def _dot(a, b):
    return jnp.dot(a, b, preferred_element_type=F32)


def _dot_nt(a, b):
    return lax.dot_general(a, b, (((1,), (1,)), ((), ())), preferred_element_type=F32)


def _dot_tn(a, b):
    return lax.dot_general(a, b, (((0,), (0,)), ((), ())), preferred_element_type=F32)


def _const_spec(shape):
    nd = len(shape)
    return pl.BlockSpec(shape, lambda *_: (0,) * nd, pipeline_mode=pl.Buffered(1))


def _rows(item, size):
    start = item * size
    if not isinstance(start, int):
        start = pl.multiple_of(start, size)
    return pl.ds(start, size)


def _software_pipeline(n_items, stages, unroll=1):
    depth = len(stages)

    def trip(t, static):
        for k in reversed(range(depth)):
            if not static or 0 <= t - k < n_items:
                stages[k](t - k)

    for t in range(depth - 1):
        trip(t, True)

    def steady(t, carry):
        trip(t, False)
        return carry

    lax.fori_loop(depth - 1, n_items, steady, 0, unroll=unroll)
    for t in range(n_items, n_items + depth - 1):
        trip(t, True)


def _stage_w_in(wt_ref, w_s):
    z_src = 2 * QK_GLA_W + V_GLA_W + G_GLA_W
    tail = z_src + Z_W
    step = 128

    def copy_rows(src, dst, n_rows, scale):
        def body(i, carry):
            w = wt_ref[pl.ds(src + i * step, step), :]
            w_s[pl.ds(dst + i * step, step), :] = (w if scale == 1.0 else w * scale).astype(BF16)
            return carry
        lax.fori_loop(0, n_rows // step, body, 0)

    copy_rows(0, 0, QK_GLA_W, GLA_DK ** -0.5)
    copy_rows(QK_GLA_W, QK_GLA_W, z_src - QK_GLA_W, 1.0)
    copy_rows(tail, z_src, Q_SWA_W, SWA_DH ** -0.5)
    copy_rows(tail + Q_SWA_W, z_src + Q_SWA_W, 2 * KV_SWA_W, 1.0)
    w_s[IN_COLS - Z_W:IN_COLS, :] = wt_ref[z_src:tail, :].astype(BF16)


def _inproj_kernel(x_ref, g_ref, w_ref, *refs):
    n_proj = len(IN_WIDTHS)
    later_f32, refs = refs[:N_LATER_WEIGHTS], refs[N_LATER_WEIGHTS:]
    out_refs, later_bf16, w_s = refs[:n_proj], refs[n_proj:-1], refs[-1]

    @pl.when(pl.program_id(0) == 0)
    def _():
        _stage_w_in(w_ref, w_s)

    for src, dst in zip(later_f32, later_bf16):
        dst[...] = src[...].astype(BF16)

    u = (_rms(x_ref[...]) * g_ref[...]).astype(BF16)
    off = 0
    for group in IN_DOT_GROUPS:
        width = sum(IN_WIDTHS[i] for i in group)
        y = _dot_nt(u, w_s[off:off + width, :])
        off += width
        col = 0
        for i in group:
            out_refs[i][...] = y[:, col:col + IN_WIDTHS[i]].astype(out_refs[i].dtype)
            col += IN_WIDTHS[i]


def _inproj(x2, g, w, later_weights):
    t = x2.shape[0]
    steps = t // TOKEN_BLOCK
    row = lambda width: pl.BlockSpec((TOKEN_BLOCK, width), lambda i: (i, 0))
    widths = IN_WIDTHS
    dtypes = (BF16,) * (len(IN_WIDTHS) - 1) + (F32,)
    assert len(later_weights) == N_LATER_WEIGHTS
    slices = [pl.BlockSpec((lw.shape[0] // steps, lw.shape[1]), lambda i: (i, 0))
              for lw in later_weights]
    return pl.pallas_call(
        _inproj_kernel,
        grid=(steps,),
        in_specs=[row(D_MODEL), _const_spec((1, D_MODEL)), _const_spec((IN_COLS, D_MODEL))] + slices,
        out_specs=[row(wd) for wd in widths] + slices,
        out_shape=([jax.ShapeDtypeStruct((t, wd), dt) for wd, dt in zip(widths, dtypes)]
                   + [jax.ShapeDtypeStruct(lw.shape, BF16) for lw in later_weights]),
        scratch_shapes=[pltpu.VMEM((IN_COLS, D_MODEL), BF16)],
        compiler_params=pltpu.CompilerParams(
            dimension_semantics=("arbitrary",), vmem_limit_bytes=VMEM_LIMIT),
        name="inproj",
    )(x2, g, w, *later_weights)


def _cumsum_matrix():
    r = np.arange(GLA_GROUP)[:, None]
    s = np.arange(GLA_GROUP)[None, :]
    return (((r // GLA_CHUNK) == (s // GLA_CHUNK)) & (s <= r)).astype(np.float32)


def _gla_kernel(q_ref, k_ref, v_ref, g_ref, z_ref, wgu_ref, bg_ref, nrm_ref, cm_ref, o_ref,
                la_s, hl_s, cs_s, qdec_s, kina_s, kinb_s, kst_s, am_s, ds_s, dec_s, sc_s, *, seq):
    c = GLA_CHUNK
    nc = seq // c
    grp = GLA_GROUP
    cpg = grp // c
    n_groups = seq // grp
    wgu = wgu_ref[0]
    bg = bg_ref[0]
    fwd_p = lax.broadcasted_iota(jnp.int16, (grp, LANES), 1) < HALF
    own_p = fwd_p == (pl.program_id(1) % 2 == 0)
    zero_p = jnp.zeros((grp, LANES), BF16)

    def own_head_twice(tile):
        return jnp.where(own_p, tile, pltpu.roll(tile, HALF, axis=1)).astype(F32)

    fwd_3 = lax.broadcasted_iota(jnp.int32, (cpg, c, LANES), 2) < HALF
    lane_c = lax.broadcasted_iota(jnp.int32, (c, LANES), 1)
    row_c = lax.broadcasted_iota(jnp.int32, (c, LANES), 0)
    tri = ((lane_c < HALF) & (lane_c <= row_c)) | ((lane_c >= HALF) & ((lane_c - HALF) >= row_c))
    zero_c = jnp.zeros((c, LANES), F32)

    def gate_preact(i):
        rows = _rows(i, grp)
        la_s[rows, :] = _dot(z_ref[rows, :].astype(BF16), wgu)

    def decay_logs(i):
        rows = _rows(i, grp)
        pre = la_s[rows, :] + bg
        la = ((jnp.minimum(pre, 0.0) - jnp.log(1.0 + jnp.exp(-jnp.abs(pre))))
              * (1.0 / GLA_GATE_NORM))
        la_hi = la.astype(BF16)
        la_s[rows, :] = la
        hl_s[rows, 0:LANES] = la_hi
        hl_s[rows, LANES:2 * LANES] = (la - la_hi.astype(F32)).astype(BF16)

    def chunk_cumsum(i):
        rows = _rows(i, grp)
        cs = _dot(cm_ref[...], hl_s[rows, :])
        cs_s[rows, :] = cs[:, :LANES] + cs[:, LANES:]

    def decayed_qk(i):
        rows = _rows(i, grp)
        la3 = la_s[rows, :].reshape(cpg, c, LANES)
        cum3 = cs_s[rows, :].reshape(cpg, c, LANES)
        tot = jnp.sum(la3, axis=1, keepdims=True)
        b3 = jnp.where(fwd_3, cum3, tot - cum3 + la3)
        b = b3.reshape(grp, LANES)
        rest = jnp.exp(tot - b3).reshape(grp, LANES)
        qq = own_head_twice(q_ref[rows, :])
        kk = own_head_twice(k_ref[rows, :])
        kin = (kk * jnp.exp(-b)).astype(BF16)
        qdec_s[rows, :] = (qq * jnp.exp(b)).astype(BF16)
        kina_s[rows, :] = jnp.where(fwd_p, kin, zero_p)
        kinb_s[rows, :] = jnp.where(fwd_p, zero_p, kin)
        kst_s[rows, :] = (kk * rest).astype(BF16)
        dec_s[_rows(i, cpg)] = jnp.broadcast_to(jnp.exp(tot), (cpg, 8, LANES))

    def chunk_products(i):
        for cc in range(cpg):
            n = i * cpg + cc
            rows = _rows(n, c)
            kin_bd = jnp.concatenate([kina_s[rows, :], kinb_s[rows, :]], axis=0)
            a2 = _dot_nt(qdec_s[rows, :], kin_bd)
            am_s[rows, :] = jnp.where(tri, a2, zero_c).astype(BF16)
            ds_s[n] = _dot_tn(v_ref[rows, :], kst_s[rows, :])

    _software_pipeline(n_groups, [gate_preact, decay_logs, chunk_cumsum, decayed_qk,
                                  chunk_products], unroll=GLA_PREP_UNROLL)

    fwd_sq = lax.broadcasted_iota(jnp.int32, (LANES, LANES), 1) < HALF
    fwd_row = lax.broadcasted_iota(jnp.int32, (8, LANES), 1) < HALF

    def scan_step(i, s):
        j = nc - 1 - i
        sb = s.astype(BF16)
        sc_s[i, :, 0:HALF] = sb[:, 0:HALF]
        sc_s[j, :, HALF:LANES] = sb[:, HALF:LANES]
        dec = jnp.where(fwd_row, dec_s[i], dec_s[j])[0:1]
        return dec * s + jnp.where(fwd_sq, ds_s[i], ds_s[j])

    lax.fori_loop(0, nc, scan_step, jnp.zeros((LANES, LANES), F32), unroll=GLA_UNROLL)

    nrm = nrm_ref[...]

    def mix(i):
        for cc in range(cpg):
            n = i * cpg + cc
            rows = _rows(n, c)
            vc = v_ref[rows, :]
            la_s[rows, :] = (_dot_nt(qdec_s[rows, :], sc_s[n])
                             + _dot(am_s[rows, :], jnp.concatenate([vc, vc], axis=0)))

    def norm_gate(i):
        rows = _rows(i, grp)
        gate = g_ref[rows, :].astype(F32)
        o = (_rms(la_s[rows, :]) * nrm) * (gate / (1.0 + jnp.exp(-gate)))
        o_ref[rows, :] = o.astype(o_ref.dtype)

    _software_pipeline(n_groups, [mix, norm_gate], unroll=GLA_OUT_UNROLL)


def _gla(q, k, v, g, z, wgu, bg, nrm, batch, seq):
    t = batch * seq
    nc = seq // GLA_CHUNK
    tile = pl.BlockSpec((seq, LANES), lambda b, h: (b, h))
    pair_tile = pl.BlockSpec((seq, LANES), lambda b, h: (b, h // 2))
    tok_bf = pltpu.VMEM((seq, LANES), BF16)
    cm = jnp.asarray(_cumsum_matrix(), dtype=BF16)
    return pl.pallas_call(
        functools.partial(_gla_kernel, seq=seq),
        grid=(batch, GLA_HEADS),
        in_specs=[pair_tile, pair_tile, tile, tile,
                  pl.BlockSpec((seq, Z_W), lambda b, h: (b, 0)),
                  pl.BlockSpec((1, Z_W, LANES), lambda b, h: (h, 0, 0)),
                  pl.BlockSpec((1, 1, LANES), lambda b, h: (h, 0, 0)),
                  pl.BlockSpec((1, LANES), lambda b, h: (0, 0)),
                  _const_spec(cm.shape)],
        out_specs=tile,
        out_shape=jax.ShapeDtypeStruct((t, V_GLA_W), BF16),
        scratch_shapes=[pltpu.VMEM((seq, LANES), F32),
                        pltpu.VMEM((seq, 2 * LANES), BF16),
                        pltpu.VMEM((seq, LANES), F32),
                        tok_bf, tok_bf, tok_bf, tok_bf, tok_bf,
                        pltpu.VMEM((nc, LANES, LANES), F32),
                        pltpu.VMEM((nc, 8, LANES), F32),
                        pltpu.VMEM((nc, LANES, LANES), BF16)],
        compiler_params=pltpu.CompilerParams(
            dimension_semantics=("arbitrary", "arbitrary"), vmem_limit_bytes=VMEM_LIMIT),
        name="gla",
    )(q, k, v, g, z, wgu, bg, nrm, cm)


def _t5_buckets(rel):
    nb = REL_BUCKETS // 2
    ret = (rel > 0).astype(np.int32) * nb
    n = np.abs(rel)
    max_exact = nb // 2
    large = max_exact + (np.log(np.maximum(n, 1).astype(np.float32) / max_exact)
                         / math.log(REL_MAX_DIST / max_exact) * (nb - max_exact)).astype(np.int32)
    large = np.minimum(large, nb - 1)
    return ret + np.where(n < max_exact, n, large)


def _bucket_map():
    w = SWA_BLOCK
    cq = np.arange(w)[:, None]
    s = np.arange(3 * w)[None, :]
    rel = s - w - cq
    return np.where(np.abs(rel) <= SWA_WINDOW, _t5_buckets(rel), -1).astype(np.int32)


def _bias_kernel(bm_ref, tab_ref, o_ref):
    w = SWA_BLOCK
    pair = pl.program_id(0)
    bm = bm_ref[...]
    key = lax.broadcasted_iota(jnp.int32, bm.shape, 1)
    masked = jnp.full(bm.shape, MASK_VALUE, F32)
    for hh in range(2):
        head = 2 * pair + hh
        acc = masked
        for bkt in range(REL_BUCKETS):
            acc = jnp.where(bm == bkt, tab_ref[bkt, head], acc)
        hcols = slice(hh * 3 * w, (hh + 1) * 3 * w)
        o_ref[0, 0, :, hcols] = jnp.where(key >= w, acc, masked)
        o_ref[1, 0, :, hcols] = acc
        o_ref[2, 0, :, hcols] = jnp.where(key < 2 * w, acc, masked)


def _bias_tables(rel_bias):
    w = SWA_BLOCK
    pairs = SWA_Q_HEADS // 2
    return pl.pallas_call(
        _bias_kernel,
        grid=(pairs,),
        in_specs=[pl.BlockSpec((w, 3 * w), lambda p: (0, 0)),
                  pl.BlockSpec(memory_space=pltpu.SMEM)],
        out_specs=pl.BlockSpec((3, 1, w, 6 * w), lambda p: (0, p, 0, 0)),
        out_shape=jax.ShapeDtypeStruct((3, pairs, w, 6 * w), F32),
        compiler_params=pltpu.CompilerParams(dimension_semantics=("arbitrary",)),
        name="swa_bias",
    )(jnp.asarray(_bucket_map()), rel_bias.astype(F32))


def _swa_kernel(q_ref, k_ref, v_ref, bias_ref, sink_ref, o_ref, ka_s, kb_s, va_s, vb_s,
                lg_s, p_s, st_s, *, seq):
    w = SWA_BLOCK
    nb = seq // w
    kvh = pl.program_id(1)
    pairs = SWA_Q_HEADS // SWA_KV_HEADS // 2
    stage_rows = 512
    lane_s = lax.broadcasted_iota(jnp.int16, (stage_rows, LANES), 1)
    lo = lane_s < HALF
    own = lo == (kvh == 0)
    for s_ref in (ka_s, kb_s, va_s, vb_s):
        zpad = jnp.zeros((w, s_ref.shape[1]), BF16)
        s_ref[0:w, :] = zpad
        s_ref[w + seq:2 * w + seq, :] = zpad

    def stage(i, carry):
        src = pl.ds(pl.multiple_of(i * stage_rows, stage_rows), stage_rows)
        dst = pl.ds(pl.multiple_of(i * stage_rows + w, w), stage_rows)
        k2 = k_ref[src, :]
        v2 = v_ref[src, :]
        kk = jnp.where(own, k2, pltpu.roll(k2, HALF, axis=1))
        vv = jnp.where(own, v2, pltpu.roll(v2, HALF, axis=1))
        zero = jnp.zeros_like(kk)
        one = jnp.ones_like(kk)
        ka_s[dst, :] = jnp.where(lo, kk, zero)
        kb_s[dst, :] = jnp.where(lo, zero, kk)
        va_s[dst, 0:LANES] = jnp.where(lo, vv, zero)
        va_s[dst, LANES:2 * LANES] = jnp.where(lo, one, zero)
        vb_s[dst, 0:LANES] = jnp.where(lo, zero, vv)
        vb_s[dst, LANES:2 * LANES] = jnp.where(lo, zero, one)
        return carry

    lax.fori_loop(0, seq // stage_rows, stage, 0)

    lo_w = lax.broadcasted_iota(jnp.int32, (w, LANES), 1) < HALF

    def logits_stage(n, variant):
        qrows = pl.ds(pl.multiple_of(n * w, w), w)
        band = pl.ds(pl.multiple_of(n * w, w), 3 * w)
        kt = jnp.concatenate([ka_s[band, :], kb_s[band, :]], axis=0)
        for pair in range(pairs):
            cols = slice(pair * LANES, (pair + 1) * LANES)
            lg_s[pair] = _dot_nt(q_ref[qrows, cols], kt) + bias_ref[variant, pair]

    def softmax_stage():
        for pair in range(pairs):
            sink_terms = []
            for hh in range(2):
                hcols = slice(hh * 3 * w, (hh + 1) * 3 * w)
                lg = lg_s[pair, :, hcols]
                sink = sink_ref[kvh * 2 * pairs + 2 * pair + hh]
                m = jnp.maximum(jnp.max(lg, axis=-1, keepdims=True), sink)
                p_s[pair, :, hcols] = jnp.exp(lg - m).astype(BF16)
                sink_terms.append(jnp.exp(sink - m))
            st_s[pair] = jnp.where(lo_w, sink_terms[0], sink_terms[1])

    def output_stage(n):
        qrows = pl.ds(pl.multiple_of(n * w, w), w)
        band = pl.ds(pl.multiple_of(n * w, w), 3 * w)
        vx = jnp.concatenate([va_s[band, :], vb_s[band, :]], axis=0)
        for pair in range(pairs):
            ox = _dot(p_s[pair], vx)
            o = ox[:, :LANES] / (ox[:, LANES:] + st_s[pair])
            o_ref[qrows, pair * LANES:(pair + 1) * LANES] = o.astype(o_ref.dtype)

    logits_stage(0, 0)
    softmax_stage()
    logits_stage(1, 1)

    def steady(n, carry):
        output_stage(n - 1)
        softmax_stage()
        logits_stage(n + 1, 1)
        return carry

    lax.fori_loop(1, nb - 2, steady, 0, unroll=2)
    output_stage(nb - 3)
    softmax_stage()
    logits_stage(nb - 1, 2)
    output_stage(nb - 2)
    softmax_stage()
    output_stage(nb - 1)


def _swa(q, k, v, bias, sink, batch, seq):
    t = batch * seq
    w = SWA_BLOCK
    qw = Q_SWA_W // SWA_KV_HEADS
    pairs = SWA_Q_HEADS // SWA_KV_HEADS // 2
    staged_k = pltpu.VMEM((seq + 2 * w, LANES), BF16)
    staged_v = pltpu.VMEM((seq + 2 * w, 2 * LANES), BF16)
    return pl.pallas_call(
        functools.partial(_swa_kernel, seq=seq),
        grid=(batch, SWA_KV_HEADS),
        in_specs=[pl.BlockSpec((seq, qw), lambda b, j: (b, j)),
                  pl.BlockSpec((seq, LANES), lambda b, j: (b, 0)),
                  pl.BlockSpec((seq, LANES), lambda b, j: (b, 0)),
                  pl.BlockSpec((3, pairs, w, 6 * w), lambda b, j: (0, j, 0, 0)),
                  pl.BlockSpec(memory_space=pltpu.SMEM)],
        out_specs=pl.BlockSpec((seq, qw), lambda b, j: (b, j)),
        out_shape=jax.ShapeDtypeStruct((t, Q_SWA_W), BF16),
        scratch_shapes=[staged_k, staged_k, staged_v, staged_v,
                        pltpu.VMEM((pairs, w, 6 * w), F32),
                        pltpu.VMEM((pairs, w, 6 * w), BF16),
                        pltpu.VMEM((pairs, w, LANES), F32)],
        compiler_params=pltpu.CompilerParams(
            dimension_semantics=("arbitrary", "arbitrary"), vmem_limit_bytes=VMEM_LIMIT),
        name="swa",
    )(q, k, v, bias, sink)


def _ffn_kernel(x_ref, oa_ref, os_ref, wout_ref, gpost_ref, gpre_ref, wup_ref, wdn_ref,
                gout_ref, o_ref):
    slab = FFN_TOKEN_BLOCK // FFN_SLABS
    slabs = [slice(s * slab, (s + 1) * slab) for s in range(FFN_SLABS)]
    mixes = [_dot(oa_ref[rows, :], wout_ref[0:V_GLA_W, :])
             + _dot(os_ref[rows, :], wout_ref[V_GLA_W:V_GLA_W + Q_SWA_W, :]) for rows in slabs]
    for rows, mix in zip(slabs, mixes):
        h = x_ref[rows, :] + _rms(mix) * gpost_ref[...]
        hn = (_rms(h) * gpre_ref[...]).astype(BF16)
        acc = jnp.zeros(h.shape, F32)
        for j in range(D_FF // FF_BLOCK):
            cols = slice(j * FF_BLOCK, (j + 1) * FF_BLOCK)
            zj = jnp.maximum(_dot(hn, wup_ref[:, cols]), 0.0)
            acc = acc + _dot((zj * zj).astype(BF16), wdn_ref[cols, :])
        o_ref[rows, :] = h + _rms(acc) * gout_ref[...]


def _ffn(x2, oa, os_, wout, gpost, gpre, wup, wdn, gout):
    t = x2.shape[0]
    row = lambda width: pl.BlockSpec((FFN_TOKEN_BLOCK, width), lambda i: (i, 0))
    return pl.pallas_call(
        _ffn_kernel,
        grid=(t // FFN_TOKEN_BLOCK,),
        in_specs=[row(D_MODEL), row(V_GLA_W), row(Q_SWA_W),
                  _const_spec((V_GLA_W + Q_SWA_W, D_MODEL)),
                  _const_spec((1, D_MODEL)), _const_spec((1, D_MODEL)),
                  _const_spec((D_MODEL, D_FF)), _const_spec((D_FF, D_MODEL)),
                  _const_spec((1, D_MODEL))],
        out_specs=row(D_MODEL),
        out_shape=jax.ShapeDtypeStruct((t, D_MODEL), F32),
        compiler_params=pltpu.CompilerParams(
            dimension_semantics=("arbitrary",), vmem_limit_bytes=VMEM_LIMIT),
        name="outproj_ffn",
    )(x2, oa, os_, wout, gpost, gpre, wup, wdn, gout)


def _layer(h, norm_mix_pre, w_in, w_gu_f, b_g_f, w_gu_b, b_g_b, gla_norm, swa_sink, bias_tab,
           w_out, norm_mix_post, norm_mlp_pre, w_up, w_down, norm_mlp_post):
    batch, seq, _ = h.shape
    x2 = h.reshape(batch * seq, D_MODEL)
    row = lambda g: g.reshape(1, -1).astype(F32)

    later = [w.astype(F32) for w in (w_out, w_up, w_down)]
    qa, ka, va, ga, qs, ks, vs, z, w_out_b, w_up_b, w_down_b = _inproj(
        x2, row(norm_mix_pre), w_in.astype(F32).T, later)

    wf = w_gu_f.reshape(GLA_RANK, GLA_HEADS, GLA_DK).transpose(1, 0, 2)
    wb = w_gu_b.reshape(GLA_RANK, GLA_HEADS, GLA_DK).transpose(1, 0, 2)
    zeros = jnp.zeros_like(wf)
    wgu = jnp.concatenate([jnp.concatenate([wf, zeros], axis=2),
                           jnp.concatenate([zeros, wb], axis=2)], axis=1).astype(BF16)
    bg = jnp.concatenate([b_g_f.reshape(GLA_HEADS, 1, GLA_DK),
                          b_g_b.reshape(GLA_HEADS, 1, GLA_DK)], axis=2).astype(F32)
    o_a = _gla(qa, ka, va, ga, z, wgu, bg, row(gla_norm), batch, seq)

    o_s = _swa(qs, ks, vs, bias_tab, swa_sink.astype(F32), batch, seq)

    out = _ffn(x2, o_a, o_s, w_out_b, row(norm_mix_post), row(norm_mlp_pre), w_up_b, w_down_b,
               row(norm_mlp_post))
    return out.reshape(batch, seq, D_MODEL)


def kernel(x, norm_mix_pre, w_in, w_gate_up_fwd, b_gate_fwd, w_gate_up_bwd, b_gate_bwd, gla_norm,
           swa_sink, rel_bias, w_out, norm_mix_post, norm_mlp_pre, w_up, w_down, norm_mlp_post):
    bias_tab = _bias_tables(rel_bias)
    h = x
    for l in range(w_in.shape[0]):
        h = _layer(h, norm_mix_pre[l], w_in[l], w_gate_up_fwd[l], b_gate_fwd[l], w_gate_up_bwd[l],
                   b_gate_bwd[l], gla_norm[l], swa_sink[l], bias_tab, w_out[l], norm_mix_post[l],
                   norm_mlp_pre[l], w_up[l], w_down[l], norm_mlp_post[l])
    return h
```

```python
import functools
import math

import jax
import jax.numpy as jnp
import numpy as np
from jax import lax
from jax.experimental import pallas as pl
from jax.experimental.pallas import tpu as pltpu

F32 = jnp.float32
BF16 = jnp.bfloat16

D_MODEL = 1024
GLA_HEADS = 4
GLA_DK = 64
GLA_DV = 128
GLA_RANK = 16
GLA_GATE_NORM = 16.0
GLA_CHUNK = 64
SWA_DH = 64
SWA_Q_HEADS = 8
SWA_KV_HEADS = 2
SWA_BLOCK = 128
SWA_WINDOW = 128
REL_BUCKETS = 32
REL_MAX_DIST = 128
D_FF = 4 * D_MODEL
NORM_EPS = 1e-6
MASK_VALUE = -1e30

LANES = 128
HALF = LANES // 2

QK_GLA_W = GLA_HEADS * GLA_DK
V_GLA_W = GLA_HEADS * GLA_DV
G_GLA_W = GLA_HEADS * GLA_DV
Q_SWA_W = SWA_Q_HEADS * SWA_DH
KV_SWA_W = SWA_KV_HEADS * SWA_DH
Z_W = 2 * GLA_RANK
IN_WIDTHS = (QK_GLA_W, QK_GLA_W, V_GLA_W, G_GLA_W, Q_SWA_W, KV_SWA_W, KV_SWA_W, Z_W)
IN_COLS = sum(IN_WIDTHS)
IN_DOT_GROUPS = ((0, 1), (2,), (3,), (4,), (5, 6), (7,))
N_LATER_WEIGHTS = 3

GLA_UNROLL = 8
GLA_GROUP = 256
GLA_PREP_UNROLL = 6
GLA_OUT_UNROLL = 15
TOKEN_BLOCK = 1024
FFN_TOKEN_BLOCK = 1024
FF_BLOCK = 1024
FFN_SLABS = 4
VMEM_LIMIT = 56 * 1024 * 1024


def _rms(x):
    return x * lax.rsqrt(jnp.mean(x * x, axis=-1, keepdims=True) + NORM_EPS)


def _dot(a, b):
    return jnp.dot(a, b, preferred_element_type=F32)


def _dot_nt(a, b):
    return lax.dot_general(a, b, (((1,), (1,)), ((), ())), preferred_element_type=F32)


def _dot_tn(a, b):
    return lax.dot_general(a, b, (((0,), (0,)), ((), ())), preferred_element_type=F32)


def _const_spec(shape):
    nd = len(shape)
    return pl.BlockSpec(shape, lambda *_: (0,) * nd, pipeline_mode=pl.Buffered(1))


def _rows(item, size):
    start = item * size
    if not isinstance(start, int):
        start = pl.multiple_of(start, size)
    return pl.ds(start, size)


def _software_pipeline(n_items, stages, unroll=1):
    depth = len(stages)

    def trip(t, static):
        for k in reversed(range(depth)):
            if not static or 0 <= t - k < n_items:
                stages[k](t - k)

    for t in range(depth - 1):
        trip(t, True)

    def steady(t, carry):
        trip(t, False)
        return carry

    lax.fori_loop(depth - 1, n_items, steady, 0, unroll=unroll)
    for t in range(n_items, n_items + depth - 1):
        trip(t, True)


def _stage_w_in(wt_ref, w_s):
    z_src = 2 * QK_GLA_W + V_GLA_W + G_GLA_W
    tail = z_src + Z_W
    step = 128

    def copy_rows(src, dst, n_rows, scale):
        def body(i, carry):
            w = wt_ref[pl.ds(src + i * step, step), :]
            w_s[pl.ds(dst + i * step, step), :] = (w if scale == 1.0 else w * scale).astype(BF16)
            return carry
        lax.fori_loop(0, n_rows // step, body, 0)

    copy_rows(0, 0, QK_GLA_W, GLA_DK ** -0.5)
    copy_rows(QK_GLA_W, QK_GLA_W, z_src - QK_GLA_W, 1.0)
    copy_rows(tail, z_src, Q_SWA_W, SWA_DH ** -0.5)
    copy_rows(tail + Q_SWA_W, z_src + Q_SWA_W, 2 * KV_SWA_W, 1.0)
    w_s[IN_COLS - Z_W:IN_COLS, :] = wt_ref[z_src:tail, :].astype(BF16)


def _inproj_kernel(x_ref, g_ref, w_ref, *refs):
    n_proj = len(IN_WIDTHS)
    later_f32, refs = refs[:N_LATER_WEIGHTS], refs[N_LATER_WEIGHTS:]
    out_refs, later_bf16, w_s = refs[:n_proj], refs[n_proj:-1], refs[-1]

    @pl.when(pl.program_id(0) == 0)
    def _():
        _stage_w_in(w_ref, w_s)

    for src, dst in zip(later_f32, later_bf16):
        dst[...] = src[...].astype(BF16)

    u = (_rms(x_ref[...]) * g_ref[...]).astype(BF16)
    off = 0
    for group in IN_DOT_GROUPS:
        width = sum(IN_WIDTHS[i] for i in group)
        y = _dot_nt(u, w_s[off:off + width, :])
        off += width
        col = 0
        for i in group:
            out_refs[i][...] = y[:, col:col + IN_WIDTHS[i]].astype(out_refs[i].dtype)
            col += IN_WIDTHS[i]


def _inproj(x2, g, w, later_weights):
    t = x2.shape[0]
    steps = t // TOKEN_BLOCK
    row = lambda width: pl.BlockSpec((TOKEN_BLOCK, width), lambda i: (i, 0))
    widths = IN_WIDTHS
    dtypes = (BF16,) * (len(IN_WIDTHS) - 1) + (F32,)
    assert len(later_weights) == N_LATER_WEIGHTS
    slices = [pl.BlockSpec((lw.shape[0] // steps, lw.shape[1]), lambda i: (i, 0))
              for lw in later_weights]
    return pl.pallas_call(
        _inproj_kernel,
        grid=(steps,),
        in_specs=[row(D_MODEL), _const_spec((1, D_MODEL)), _const_spec((IN_COLS, D_MODEL))] + slices,
        out_specs=[row(wd) for wd in widths] + slices,
        out_shape=([jax.ShapeDtypeStruct((t, wd), dt) for wd, dt in zip(widths, dtypes)]
                   + [jax.ShapeDtypeStruct(lw.shape, BF16) for lw in later_weights]),
        scratch_shapes=[pltpu.VMEM((IN_COLS, D_MODEL), BF16)],
        compiler_params=pltpu.CompilerParams(
            dimension_semantics=("arbitrary",), vmem_limit_bytes=VMEM_LIMIT),
        name="inproj",
    )(x2, g, w, *later_weights)


def _cumsum_matrix():
    r = np.arange(GLA_GROUP)[:, None]
    s = np.arange(GLA_GROUP)[None, :]
    return (((r // GLA_CHUNK) == (s // GLA_CHUNK)) & (s <= r)).astype(np.float32)


def _gla_kernel(q_ref, k_ref, v_ref, g_ref, z_ref, wgu_ref, bg_ref, nrm_ref, cm_ref, o_ref,
                la_s, hl_s, cs_s, qdec_s, kina_s, kinb_s, kst_s, am_s, ds_s, dec_s, sc_s, *, seq):
    c = GLA_CHUNK
    nc = seq // c
    grp = GLA_GROUP
    cpg = grp // c
    n_groups = seq // grp
    wgu = wgu_ref[0]
    bg = bg_ref[0]
    fwd_p = lax.broadcasted_iota(jnp.int16, (grp, LANES), 1) < HALF
    own_p = fwd_p == (pl.program_id(1) % 2 == 0)
    zero_p = jnp.zeros((grp, LANES), BF16)

    def own_head_twice(tile):
        return jnp.where(own_p, tile, pltpu.roll(tile, HALF, axis=1)).astype(F32)

    fwd_3 = lax.broadcasted_iota(jnp.int32, (cpg, c, LANES), 2) < HALF
    lane_c = lax.broadcasted_iota(jnp.int32, (c, LANES), 1)
    row_c = lax.broadcasted_iota(jnp.int32, (c, LANES), 0)
    tri = ((lane_c < HALF) & (lane_c <= row_c)) | ((lane_c >= HALF) & ((lane_c - HALF) >= row_c))
    zero_c = jnp.zeros((c, LANES), F32)

    def gate_preact(i):
        rows = _rows(i, grp)
        la_s[rows, :] = _dot(z_ref[rows, :].astype(BF16), wgu)

    def decay_logs(i):
        rows = _rows(i, grp)
        pre = la_s[rows, :] + bg
        la = ((jnp.minimum(pre, 0.0) - jnp.log(1.0 + jnp.exp(-jnp.abs(pre))))
              * (1.0 / GLA_GATE_NORM))
        la_hi = la.astype(BF16)
        la_s[rows, :] = la
        hl_s[rows, 0:LANES] = la_hi
        hl_s[rows, LANES:2 * LANES] = (la - la_hi.astype(F32)).astype(BF16)

    def chunk_cumsum(i):
        rows = _rows(i, grp)
        cs = _dot(cm_ref[...], hl_s[rows, :])
        cs_s[rows, :] = cs[:, :LANES] + cs[:, LANES:]

    def decayed_qk(i):
        rows = _rows(i, grp)
        la3 = la_s[rows, :].reshape(cpg, c, LANES)
        cum3 = cs_s[rows, :].reshape(cpg, c, LANES)
        tot = jnp.sum(la3, axis=1, keepdims=True)
        b3 = jnp.where(fwd_3, cum3, tot - cum3 + la3)
        b = b3.reshape(grp, LANES)
        rest = jnp.exp(tot - b3).reshape(grp, LANES)
        qq = own_head_twice(q_ref[rows, :])
        kk = own_head_twice(k_ref[rows, :])
        kin = (kk * jnp.exp(-b)).astype(BF16)
        qdec_s[rows, :] = (qq * jnp.exp(b)).astype(BF16)
        kina_s[rows, :] = jnp.where(fwd_p, kin, zero_p)
        kinb_s[rows, :] = jnp.where(fwd_p, zero_p, kin)
        kst_s[rows, :] = (kk * rest).astype(BF16)
        dec_s[_rows(i, cpg)] = jnp.broadcast_to(jnp.exp(tot), (cpg, 8, LANES))

    def chunk_products(i):
        for cc in range(cpg):
            n = i * cpg + cc
            rows = _rows(n, c)
            kin_bd = jnp.concatenate([kina_s[rows, :], kinb_s[rows, :]], axis=0)
            a2 = _dot_nt(qdec_s[rows, :], kin_bd)
            am_s[rows, :] = jnp.where(tri, a2, zero_c).astype(BF16)
            ds_s[n] = _dot_tn(v_ref[rows, :], kst_s[rows, :])

    _software_pipeline(n_groups, [gate_preact, decay_logs, chunk_cumsum, decayed_qk,
                                  chunk_products], unroll=GLA_PREP_UNROLL)

    fwd_sq = lax.broadcasted_iota(jnp.int32, (LANES, LANES), 1) < HALF
    fwd_row = lax.broadcasted_iota(jnp.int32, (8, LANES), 1) < HALF

    def scan_step(i, s):
        j = nc - 1 - i
        sb = s.astype(BF16)
        sc_s[i, :, 0:HALF] = sb[:, 0:HALF]
        sc_s[j, :, HALF:LANES] = sb[:, HALF:LANES]
        dec = jnp.where(fwd_row, dec_s[i], dec_s[j])[0:1]
        return dec * s + jnp.where(fwd_sq, ds_s[i], ds_s[j])

    lax.fori_loop(0, nc, scan_step, jnp.zeros((LANES, LANES), F32), unroll=GLA_UNROLL)

    nrm = nrm_ref[...]

    def mix(i):
        for cc in range(cpg):
            n = i * cpg + cc
            rows = _rows(n, c)
            vc = v_ref[rows, :]
            vvt = jnp.concatenate([vc, vc], axis=0).T
            la_s[rows, :] = _dot_nt(jnp.concatenate([qdec_s[rows, :], am_s[rows, :]], axis=1),
                                    jnp.concatenate([sc_s[n], vvt], axis=1))

    def norm_gate(i):
        rows = _rows(i, grp)
        gate = g_ref[rows, :].astype(F32)
        o = (_rms(la_s[rows, :]) * nrm) * (gate / (1.0 + jnp.exp(-gate)))
        o_ref[rows, :] = o.astype(o_ref.dtype)

    _software_pipeline(n_groups, [mix, norm_gate], unroll=GLA_OUT_UNROLL)


def _gla(q, k, v, g, z, wgu, bg, nrm, batch, seq):
    t = batch * seq
    nc = seq // GLA_CHUNK
    tile = pl.BlockSpec((seq, LANES), lambda b, h: (b, h))
    pair_tile = pl.BlockSpec((seq, LANES), lambda b, h: (b, h // 2))
    tok_bf = pltpu.VMEM((seq, LANES), BF16)
    cm = jnp.asarray(_cumsum_matrix(), dtype=BF16)
    return pl.pallas_call(
        functools.partial(_gla_kernel, seq=seq),
        grid=(batch, GLA_HEADS),
        in_specs=[pair_tile, pair_tile, tile, tile,
                  pl.BlockSpec((seq, Z_W), lambda b, h: (b, 0)),
                  pl.BlockSpec((1, Z_W, LANES), lambda b, h: (h, 0, 0)),
                  pl.BlockSpec((1, 1, LANES), lambda b, h: (h, 0, 0)),
                  pl.BlockSpec((1, LANES), lambda b, h: (0, 0)),
                  _const_spec(cm.shape)],
        out_specs=tile,
        out_shape=jax.ShapeDtypeStruct((t, V_GLA_W), BF16),
        scratch_shapes=[pltpu.VMEM((seq, LANES), F32),
                        pltpu.VMEM((seq, 2 * LANES), BF16),
                        pltpu.VMEM((seq, LANES), F32),
                        tok_bf, tok_bf, tok_bf, tok_bf, tok_bf,
                        pltpu.VMEM((nc, LANES, LANES), F32),
                        pltpu.VMEM((nc, 8, LANES), F32),
                        pltpu.VMEM((nc, LANES, LANES), BF16)],
        compiler_params=pltpu.CompilerParams(
            dimension_semantics=("arbitrary", "arbitrary"), vmem_limit_bytes=VMEM_LIMIT),
        name="gla",
    )(q, k, v, g, z, wgu, bg, nrm, cm)


def _t5_buckets(rel):
    nb = REL_BUCKETS // 2
    ret = (rel > 0).astype(np.int32) * nb
    n = np.abs(rel)
    max_exact = nb // 2
    large = max_exact + (np.log(np.maximum(n, 1).astype(np.float32) / max_exact)
                         / math.log(REL_MAX_DIST / max_exact) * (nb - max_exact)).astype(np.int32)
    large = np.minimum(large, nb - 1)
    return ret + np.where(n < max_exact, n, large)


def _bucket_map():
    w = SWA_BLOCK
    cq = np.arange(w)[:, None]
    s = np.arange(3 * w)[None, :]
    rel = s - w - cq
    return np.where(np.abs(rel) <= SWA_WINDOW, _t5_buckets(rel), -1).astype(np.int32)


def _bias_kernel(bm_ref, tab_ref, o_ref):
    w = SWA_BLOCK
    pair = pl.program_id(0)
    bm = bm_ref[...]
    key = lax.broadcasted_iota(jnp.int32, bm.shape, 1)
    masked = jnp.full(bm.shape, MASK_VALUE, F32)
    for hh in range(2):
        head = 2 * pair + hh
        acc = masked
        for bkt in range(REL_BUCKETS):
            acc = jnp.where(bm == bkt, tab_ref[bkt, head], acc)
        hcols = slice(hh * 3 * w, (hh + 1) * 3 * w)
        o_ref[0, 0, :, hcols] = jnp.where(key >= w, acc, masked)
        o_ref[1, 0, :, hcols] = acc
        o_ref[2, 0, :, hcols] = jnp.where(key < 2 * w, acc, masked)


def _bias_tables(rel_bias):
    w = SWA_BLOCK
    pairs = SWA_Q_HEADS // 2
    return pl.pallas_call(
        _bias_kernel,
        grid=(pairs,),
        in_specs=[pl.BlockSpec((w, 3 * w), lambda p: (0, 0)),
                  pl.BlockSpec(memory_space=pltpu.SMEM)],
        out_specs=pl.BlockSpec((3, 1, w, 6 * w), lambda p: (0, p, 0, 0)),
        out_shape=jax.ShapeDtypeStruct((3, pairs, w, 6 * w), F32),
        compiler_params=pltpu.CompilerParams(dimension_semantics=("arbitrary",)),
        name="swa_bias",
    )(jnp.asarray(_bucket_map()), rel_bias.astype(F32))


def _swa_kernel(q_ref, k_ref, v_ref, bias_ref, sink_ref, o_ref, ka_s, kb_s, va_s, vb_s,
                lg_s, p_s, st_s, *, seq):
    w = SWA_BLOCK
    nb = seq // w
    kvh = pl.program_id(1)
    pairs = SWA_Q_HEADS // SWA_KV_HEADS // 2
    stage_rows = 512
    lane_s = lax.broadcasted_iota(jnp.int16, (stage_rows, LANES), 1)
    lo = lane_s < HALF
    own = lo == (kvh == 0)
    for s_ref in (ka_s, kb_s, va_s, vb_s):
        zpad = jnp.zeros((w, s_ref.shape[1]), BF16)
        s_ref[0:w, :] = zpad
        s_ref[w + seq:2 * w + seq, :] = zpad

    def stage(i, carry):
        src = pl.ds(pl.multiple_of(i * stage_rows, stage_rows), stage_rows)
        dst = pl.ds(pl.multiple_of(i * stage_rows + w, w), stage_rows)
        k2 = k_ref[src, :]
        v2 = v_ref[src, :]
        kk = jnp.where(own, k2, pltpu.roll(k2, HALF, axis=1))
        vv = jnp.where(own, v2, pltpu.roll(v2, HALF, axis=1))
        zero = jnp.zeros_like(kk)
        one = jnp.ones_like(kk)
        ka_s[dst, :] = jnp.where(lo, kk, zero)
        kb_s[dst, :] = jnp.where(lo, zero, kk)
        va_s[dst, 0:LANES] = jnp.where(lo, vv, zero)
        va_s[dst, LANES:2 * LANES] = jnp.where(lo, one, zero)
        vb_s[dst, 0:LANES] = jnp.where(lo, zero, vv)
        vb_s[dst, LANES:2 * LANES] = jnp.where(lo, zero, one)
        return carry

    lax.fori_loop(0, seq // stage_rows, stage, 0)

    lo_w = lax.broadcasted_iota(jnp.int32, (w, LANES), 1) < HALF

    def logits_stage(n, variant):
        qrows = pl.ds(pl.multiple_of(n * w, w), w)
        band = pl.ds(pl.multiple_of(n * w, w), 3 * w)
        kt = jnp.concatenate([ka_s[band, :], kb_s[band, :]], axis=0)
        for pair in range(pairs):
            cols = slice(pair * LANES, (pair + 1) * LANES)
            lg_s[pair] = _dot_nt(q_ref[qrows, cols], kt) + bias_ref[variant, pair]

    def softmax_stage():
        for pair in range(pairs):
            sink_terms = []
            for hh in range(2):
                hcols = slice(hh * 3 * w, (hh + 1) * 3 * w)
                lg = lg_s[pair, :, hcols]
                sink = sink_ref[kvh * 2 * pairs + 2 * pair + hh]
                m = jnp.maximum(jnp.max(lg, axis=-1, keepdims=True), sink)
                p_s[pair, :, hcols] = jnp.exp(lg - m).astype(BF16)
                sink_terms.append(jnp.exp(sink - m))
            st_s[pair] = jnp.where(lo_w, sink_terms[0], sink_terms[1])

    def output_stage(n):
        qrows = pl.ds(pl.multiple_of(n * w, w), w)
        band = pl.ds(pl.multiple_of(n * w, w), 3 * w)
        vx = jnp.concatenate([va_s[band, :], vb_s[band, :]], axis=0)
        for pair in range(pairs):
            ox = _dot(p_s[pair], vx)
            o = ox[:, :LANES] / (ox[:, LANES:] + st_s[pair])
            o_ref[qrows, pair * LANES:(pair + 1) * LANES] = o.astype(o_ref.dtype)

    logits_stage(0, 0)
    softmax_stage()
    logits_stage(1, 1)

    def steady(n, carry):
        output_stage(n - 1)
        softmax_stage()
        logits_stage(n + 1, 1)
        return carry

    lax.fori_loop(1, nb - 2, steady, 0, unroll=2)
    output_stage(nb - 3)
    softmax_stage()
    logits_stage(nb - 1, 2)
    output_stage(nb - 2)
    softmax_stage()
    output_stage(nb - 1)


def _swa(q, k, v, bias, sink, batch, seq):
    t = batch * seq
    w = SWA_BLOCK
    qw = Q_SWA_W // SWA_KV_HEADS
    pairs = SWA_Q_HEADS // SWA_KV_HEADS // 2
    staged_k = pltpu.VMEM((seq + 2 * w, LANES), BF16)
    staged_v = pltpu.VMEM((seq + 2 * w, 2 * LANES), BF16)
    return pl.pallas_call(
        functools.partial(_swa_kernel, seq=seq),
        grid=(batch, SWA_KV_HEADS),
        in_specs=[pl.BlockSpec((seq, qw), lambda b, j: (b, j)),
                  pl.BlockSpec((seq, LANES), lambda b, j: (b, 0)),
                  pl.BlockSpec((seq, LANES), lambda b, j: (b, 0)),
                  pl.BlockSpec((3, pairs, w, 6 * w), lambda b, j: (0, j, 0, 0)),
                  pl.BlockSpec(memory_space=pltpu.SMEM)],
        out_specs=pl.BlockSpec((seq, qw), lambda b, j: (b, j)),
        out_shape=jax.ShapeDtypeStruct((t, Q_SWA_W), BF16),
        scratch_shapes=[staged_k, staged_k, staged_v, staged_v,
                        pltpu.VMEM((pairs, w, 6 * w), F32),
                        pltpu.VMEM((pairs, w, 6 * w), BF16),
                        pltpu.VMEM((pairs, w, LANES), F32)],
        compiler_params=pltpu.CompilerParams(
            dimension_semantics=("arbitrary", "arbitrary"), vmem_limit_bytes=VMEM_LIMIT),
        name="swa",
    )(q, k, v, bias, sink)


def _ffn_kernel(x_ref, oa_ref, os_ref, wout_ref, gpost_ref, gpre_ref, wup_ref, wdn_ref,
                gout_ref, o_ref):
    slab = FFN_TOKEN_BLOCK // FFN_SLABS
    slabs = [slice(s * slab, (s + 1) * slab) for s in range(FFN_SLABS)]
    mixes = [_dot(oa_ref[rows, :], wout_ref[0:V_GLA_W, :])
             + _dot(os_ref[rows, :], wout_ref[V_GLA_W:V_GLA_W + Q_SWA_W, :]) for rows in slabs]
    for rows, mix in zip(slabs, mixes):
        h = x_ref[rows, :] + _rms(mix) * gpost_ref[...]
        hn = (_rms(h) * gpre_ref[...]).astype(BF16)
        acc = jnp.zeros(h.shape, F32)
        for j in range(D_FF // FF_BLOCK):
            cols = slice(j * FF_BLOCK, (j + 1) * FF_BLOCK)
            zj = jnp.maximum(_dot(hn, wup_ref[:, cols]), 0.0)
            acc = acc + _dot((zj * zj).astype(BF16), wdn_ref[cols, :])
        o_ref[rows, :] = h + _rms(acc) * gout_ref[...]


def _ffn(x2, oa, os_, wout, gpost, gpre, wup, wdn, gout):
    t = x2.shape[0]
    row = lambda width: pl.BlockSpec((FFN_TOKEN_BLOCK, width), lambda i: (i, 0))
    return pl.pallas_call(
        _ffn_kernel,
        grid=(t // FFN_TOKEN_BLOCK,),
        in_specs=[row(D_MODEL), row(V_GLA_W), row(Q_SWA_W),
                  _const_spec((V_GLA_W + Q_SWA_W, D_MODEL)),
                  _const_spec((1, D_MODEL)), _const_spec((1, D_MODEL)),
                  _const_spec((D_MODEL, D_FF)), _const_spec((D_FF, D_MODEL)),
                  _const_spec((1, D_MODEL))],
        out_specs=row(D_MODEL),
        out_shape=jax.ShapeDtypeStruct((t, D_MODEL), F32),
        compiler_params=pltpu.CompilerParams(
            dimension_semantics=("arbitrary",), vmem_limit_bytes=VMEM_LIMIT),
        name="outproj_ffn",
    )(x2, oa, os_, wout, gpost, gpre, wup, wdn, gout)


def _layer(h, norm_mix_pre, w_in, w_gu_f, b_g_f, w_gu_b, b_g_b, gla_norm, swa_sink, bias_tab,
           w_out, norm_mix_post, norm_mlp_pre, w_up, w_down, norm_mlp_post):
    batch, seq, _ = h.shape
    x2 = h.reshape(batch * seq, D_MODEL)
    row = lambda g: g.reshape(1, -1).astype(F32)

    later = [w.astype(F32) for w in (w_out, w_up, w_down)]
    qa, ka, va, ga, qs, ks, vs, z, w_out_b, w_up_b, w_down_b = _inproj(
        x2, row(norm_mix_pre), w_in.astype(F32).T, later)

    wf = w_gu_f.reshape(GLA_RANK, GLA_HEADS, GLA_DK).transpose(1, 0, 2)
    wb = w_gu_b.reshape(GLA_RANK, GLA_HEADS, GLA_DK).transpose(1, 0, 2)
    zeros = jnp.zeros_like(wf)
    wgu = jnp.concatenate([jnp.concatenate([wf, zeros], axis=2),
                           jnp.concatenate([zeros, wb], axis=2)], axis=1).astype(BF16)
    bg = jnp.concatenate([b_g_f.reshape(GLA_HEADS, 1, GLA_DK),
                          b_g_b.reshape(GLA_HEADS, 1, GLA_DK)], axis=2).astype(F32)
    o_a = _gla(qa, ka, va, ga, z, wgu, bg, row(gla_norm), batch, seq)

    o_s = _swa(qs, ks, vs, bias_tab, swa_sink.astype(F32), batch, seq)

    out = _ffn(x2, o_a, o_s, w_out_b, row(norm_mix_post), row(norm_mlp_pre), w_up_b, w_down_b,
               row(norm_mlp_post))
    return out.reshape(batch, seq, D_MODEL)


def kernel(x, norm_mix_pre, w_in, w_gate_up_fwd, b_gate_fwd, w_gate_up_bwd, b_gate_bwd, gla_norm,
           swa_sink, rel_bias, w_out, norm_mix_post, norm_mlp_pre, w_up, w_down, norm_mlp_post):
    bias_tab = _bias_tables(rel_bias)
    h = x
    for l in range(w_in.shape[0]):
        h = _layer(h, norm_mix_pre[l], w_in[l], w_gate_up_fwd[l], b_gate_fwd[l], w_gate_up_bwd[l],
                   b_gate_bwd[l], gla_norm[l], swa_sink[l], bias_tab, w_out[l], norm_mix_post[l],
                   norm_mlp_pre[l], w_up[l], w_down[l], norm_mlp_post[l])
    return h
```

```python
import functools
import math

import jax
import jax.numpy as jnp
import numpy as np
from jax import lax
from jax.experimental import pallas as pl
from jax.experimental.pallas import tpu as pltpu

F32 = jnp.float32
BF16 = jnp.bfloat16

D_MODEL = 1024
GLA_HEADS = 4
GLA_DK = 64
GLA_DV = 128
GLA_RANK = 16
GLA_GATE_NORM = 16.0
GLA_CHUNK = 64
SWA_DH = 64
SWA_Q_HEADS = 8
SWA_KV_HEADS = 2
SWA_BLOCK = 128
SWA_WINDOW = 128
REL_BUCKETS = 32
REL_MAX_DIST = 128
D_FF = 4 * D_MODEL
NORM_EPS = 1e-6
MASK_VALUE = -1e30

LANES = 128
HALF = LANES // 2

QK_GLA_W = GLA_HEADS * GLA_DK
V_GLA_W = GLA_HEADS * GLA_DV
G_GLA_W = GLA_HEADS * GLA_DV
Q_SWA_W = SWA_Q_HEADS * SWA_DH
KV_SWA_W = SWA_KV_HEADS * SWA_DH
Z_W = 2 * GLA_RANK
IN_WIDTHS = (QK_GLA_W, QK_GLA_W, V_GLA_W, G_GLA_W, Q_SWA_W, KV_SWA_W, KV_SWA_W, Z_W)
IN_COLS = sum(IN_WIDTHS)
IN_DOT_GROUPS = ((0, 1), (2,), (3,), (4,), (5, 6), (7,))
N_LATER_WEIGHTS = 3

GLA_UNROLL = 8
GLA_GROUP = 256
GLA_PREP_UNROLL = 12
GLA_OUT_UNROLL = 15
TOKEN_BLOCK = 1024
FFN_TOKEN_BLOCK = 1024
FF_BLOCK = 1024
FFN_SLABS = 4
VMEM_LIMIT = 56 * 1024 * 1024


def _rms(x):
    return x * lax.rsqrt(jnp.mean(x * x, axis=-1, keepdims=True) + NORM_EPS)


def _dot(a, b):
    return jnp.dot(a, b, preferred_element_type=F32)


def _dot_nt(a, b):
    return lax.dot_general(a, b, (((1,), (1,)), ((), ())), preferred_element_type=F32)


def _dot_tn(a, b):
    return lax.dot_general(a, b, (((0,), (0,)), ((), ())), preferred_element_type=F32)


def _const_spec(shape):
    nd = len(shape)
    return pl.BlockSpec(shape, lambda *_: (0,) * nd, pipeline_mode=pl.Buffered(1))


def _rows(item, size):
    start = item * size
    if not isinstance(start, int):
        start = pl.multiple_of(start, size)
    return pl.ds(start, size)


def _software_pipeline(n_items, stages, unroll=1):
    depth = len(stages)

    def trip(t, static):
        for k in reversed(range(depth)):
            if not static or 0 <= t - k < n_items:
                stages[k](t - k)

    for t in range(depth - 1):
        trip(t, True)

    def steady(t, carry):
        trip(t, False)
        return carry

    lax.fori_loop(depth - 1, n_items, steady, 0, unroll=unroll)
    for t in range(n_items, n_items + depth - 1):
        trip(t, True)


def _stage_w_in(wt_ref, w_s):
    z_src = 2 * QK_GLA_W + V_GLA_W + G_GLA_W
    tail = z_src + Z_W
    step = 128

    def copy_rows(src, dst, n_rows, scale):
        def body(i, carry):
            w = wt_ref[pl.ds(src + i * step, step), :]
            w_s[pl.ds(dst + i * step, step), :] = (w if scale == 1.0 else w * scale).astype(BF16)
            return carry
        lax.fori_loop(0, n_rows // step, body, 0)

    copy_rows(0, 0, QK_GLA_W, GLA_DK ** -0.5)
    copy_rows(QK_GLA_W, QK_GLA_W, z_src - QK_GLA_W, 1.0)
    copy_rows(tail, z_src, Q_SWA_W, SWA_DH ** -0.5)
    copy_rows(tail + Q_SWA_W, z_src + Q_SWA_W, 2 * KV_SWA_W, 1.0)
    w_s[IN_COLS - Z_W:IN_COLS, :] = wt_ref[z_src:tail, :].astype(BF16)


def _inproj_kernel(x_ref, g_ref, w_ref, *refs):
    n_proj = len(IN_WIDTHS)
    later_f32, refs = refs[:N_LATER_WEIGHTS], refs[N_LATER_WEIGHTS:]
    out_refs, later_bf16, w_s = refs[:n_proj], refs[n_proj:-1], refs[-1]

    @pl.when(pl.program_id(0) == 0)
    def _():
        _stage_w_in(w_ref, w_s)

    for src, dst in zip(later_f32, later_bf16):
        dst[...] = src[...].astype(BF16)

    u = (_rms(x_ref[...]) * g_ref[...]).astype(BF16)
    off = 0
    for group in IN_DOT_GROUPS:
        width = sum(IN_WIDTHS[i] for i in group)
        y = _dot_nt(u, w_s[off:off + width, :])
        off += width
        col = 0
        for i in group:
            out_refs[i][...] = y[:, col:col + IN_WIDTHS[i]].astype(out_refs[i].dtype)
            col += IN_WIDTHS[i]


def _inproj(x2, g, w, later_weights):
    t = x2.shape[0]
    steps = t // TOKEN_BLOCK
    row = lambda width: pl.BlockSpec((TOKEN_BLOCK, width), lambda i: (i, 0))
    widths = IN_WIDTHS
    dtypes = (BF16,) * (len(IN_WIDTHS) - 1) + (F32,)
    assert len(later_weights) == N_LATER_WEIGHTS
    slices = [pl.BlockSpec((lw.shape[0] // steps, lw.shape[1]), lambda i: (i, 0))
              for lw in later_weights]
    return pl.pallas_call(
        _inproj_kernel,
        grid=(steps,),
        in_specs=[row(D_MODEL), _const_spec((1, D_MODEL)), _const_spec((IN_COLS, D_MODEL))] + slices,
        out_specs=[row(wd) for wd in widths] + slices,
        out_shape=([jax.ShapeDtypeStruct((t, wd), dt) for wd, dt in zip(widths, dtypes)]
                   + [jax.ShapeDtypeStruct(lw.shape, BF16) for lw in later_weights]),
        scratch_shapes=[pltpu.VMEM((IN_COLS, D_MODEL), BF16)],
        compiler_params=pltpu.CompilerParams(
            dimension_semantics=("arbitrary",), vmem_limit_bytes=VMEM_LIMIT),
        name="inproj",
    )(x2, g, w, *later_weights)


def _cumsum_matrix():
    r = np.arange(GLA_GROUP)[:, None]
    s = np.arange(GLA_GROUP)[None, :]
    return (((r // GLA_CHUNK) == (s // GLA_CHUNK)) & (s <= r)).astype(np.float32)


def _gla_kernel(q_ref, k_ref, v_ref, g_ref, z_ref, wgu_ref, bg_ref, nrm_ref, cm_ref, o_ref,
                la_s, hl_s, cs_s, qdec_s, kina_s, kinb_s, kst_s, am_s, ds_s, dec_s, sc_s, *, seq):
    c = GLA_CHUNK
    nc = seq // c
    grp = GLA_GROUP
    cpg = grp // c
    n_groups = seq // grp
    wgu = wgu_ref[0]
    bg = bg_ref[0]
    fwd_p = lax.broadcasted_iota(jnp.int16, (grp, LANES), 1) < HALF
    own_p = fwd_p == (pl.program_id(1) % 2 == 0)
    zero_p = jnp.zeros((grp, LANES), BF16)

    def own_head_twice(tile):
        return jnp.where(own_p, tile, pltpu.roll(tile, HALF, axis=1)).astype(F32)

    fwd_3 = lax.broadcasted_iota(jnp.int32, (cpg, c, LANES), 2) < HALF
    lane_c = lax.broadcasted_iota(jnp.int32, (c, LANES), 1)
    row_c = lax.broadcasted_iota(jnp.int32, (c, LANES), 0)
    tri = ((lane_c < HALF) & (lane_c <= row_c)) | ((lane_c >= HALF) & ((lane_c - HALF) >= row_c))
    zero_c = jnp.zeros((c, LANES), F32)

    def gate_preact(i):
        rows = _rows(i, grp)
        la_s[rows, :] = _dot(z_ref[rows, :].astype(BF16), wgu)

    def decay_logs(i):
        rows = _rows(i, grp)
        pre = la_s[rows, :] + bg
        la = ((jnp.minimum(pre, 0.0) - jnp.log(1.0 + jnp.exp(-jnp.abs(pre))))
              * (1.0 / GLA_GATE_NORM))
        la_hi = la.astype(BF16)
        la_s[rows, :] = la
        hl_s[rows, 0:LANES] = la_hi
        hl_s[rows, LANES:2 * LANES] = (la - la_hi.astype(F32)).astype(BF16)

    def chunk_cumsum(i):
        rows = _rows(i, grp)
        cs = _dot(cm_ref[...], hl_s[rows, :])
        cs_s[rows, :] = cs[:, :LANES] + cs[:, LANES:]

    def decayed_qk(i):
        rows = _rows(i, grp)
        la3 = la_s[rows, :].reshape(cpg, c, LANES)
        cum3 = cs_s[rows, :].reshape(cpg, c, LANES)
        tot = jnp.sum(la3, axis=1, keepdims=True)
        b3 = jnp.where(fwd_3, cum3, tot - cum3 + la3)
        b = b3.reshape(grp, LANES)
        rest = jnp.exp(tot - b3).reshape(grp, LANES)
        qq = own_head_twice(q_ref[rows, :])
        kk = own_head_twice(k_ref[rows, :])
        kin = (kk * jnp.exp(-b)).astype(BF16)
        qdec_s[rows, :] = (qq * jnp.exp(b)).astype(BF16)
        kina_s[rows, :] = jnp.where(fwd_p, kin, zero_p)
        kinb_s[rows, :] = jnp.where(fwd_p, zero_p, kin)
        kst_s[rows, :] = (kk * rest).astype(BF16)
        dec_s[_rows(i, cpg)] = jnp.broadcast_to(jnp.exp(tot), (cpg, 8, LANES))

    def chunk_products(i):
        for cc in range(cpg):
            n = i * cpg + cc
            rows = _rows(n, c)
            kin_bd = jnp.concatenate([kina_s[rows, :], kinb_s[rows, :]], axis=0)
            a2 = _dot_nt(qdec_s[rows, :], kin_bd)
            am_s[rows, :] = jnp.where(tri, a2, zero_c).astype(BF16)
            ds_s[n] = _dot_tn(v_ref[rows, :], kst_s[rows, :])

    _software_pipeline(n_groups, [gate_preact, decay_logs, chunk_cumsum, decayed_qk,
                                  chunk_products], unroll=GLA_PREP_UNROLL)

    fwd_sq = lax.broadcasted_iota(jnp.int32, (LANES, LANES), 1) < HALF
    fwd_row = lax.broadcasted_iota(jnp.int32, (8, LANES), 1) < HALF

    def scan_step(i, s):
        j = nc - 1 - i
        sb = s.astype(BF16)
        sc_s[i, :, 0:HALF] = sb[:, 0:HALF]
        sc_s[j, :, HALF:LANES] = sb[:, HALF:LANES]
        dec = jnp.where(fwd_row, dec_s[i], dec_s[j])[0:1]
        return dec * s + jnp.where(fwd_sq, ds_s[i], ds_s[j])

    lax.fori_loop(0, nc, scan_step, jnp.zeros((LANES, LANES), F32), unroll=GLA_UNROLL)

    nrm = nrm_ref[...]

    def mix(i):
        for cc in range(cpg):
            n = i * cpg + cc
            rows = _rows(n, c)
            vc = v_ref[rows, :]
            vvt = jnp.concatenate([vc, vc], axis=0).T
            la_s[rows, :] = _dot_nt(jnp.concatenate([qdec_s[rows, :], am_s[rows, :]], axis=1),
                                    jnp.concatenate([sc_s[n], vvt], axis=1))

    def norm_gate(i):
        rows = _rows(i, grp)
        gate = g_ref[rows, :].astype(F32)
        o = (_rms(la_s[rows, :]) * nrm) * (gate / (1.0 + jnp.exp(-gate)))
        o_ref[rows, :] = o.astype(o_ref.dtype)

    _software_pipeline(n_groups, [mix, norm_gate], unroll=GLA_OUT_UNROLL)


def _gla(q, k, v, g, z, wgu, bg, nrm, batch, seq):
    t = batch * seq
    nc = seq // GLA_CHUNK
    tile = pl.BlockSpec((seq, LANES), lambda b, h: (b, h))
    pair_tile = pl.BlockSpec((seq, LANES), lambda b, h: (b, h // 2))
    tok_bf = pltpu.VMEM((seq, LANES), BF16)
    cm = jnp.asarray(_cumsum_matrix(), dtype=BF16)
    return pl.pallas_call(
        functools.partial(_gla_kernel, seq=seq),
        grid=(batch, GLA_HEADS),
        in_specs=[pair_tile, pair_tile, tile, tile,
                  pl.BlockSpec((seq, Z_W), lambda b, h: (b, 0)),
                  pl.BlockSpec((1, Z_W, LANES), lambda b, h: (h, 0, 0)),
                  pl.BlockSpec((1, 1, LANES), lambda b, h: (h, 0, 0)),
                  pl.BlockSpec((1, LANES), lambda b, h: (0, 0)),
                  _const_spec(cm.shape)],
        out_specs=tile,
        out_shape=jax.ShapeDtypeStruct((t, V_GLA_W), BF16),
        scratch_shapes=[pltpu.VMEM((seq, LANES), F32),
                        pltpu.VMEM((seq, 2 * LANES), BF16),
                        pltpu.VMEM((seq, LANES), F32),
                        tok_bf, tok_bf, tok_bf, tok_bf, tok_bf,
                        pltpu.VMEM((nc, LANES, LANES), F32),
                        pltpu.VMEM((nc, 8, LANES), F32),
                        pltpu.VMEM((nc, LANES, LANES), BF16)],
        compiler_params=pltpu.CompilerParams(
            dimension_semantics=("arbitrary", "arbitrary"), vmem_limit_bytes=VMEM_LIMIT),
        name="gla",
    )(q, k, v, g, z, wgu, bg, nrm, cm)


def _t5_buckets(rel):
    nb = REL_BUCKETS // 2
    ret = (rel > 0).astype(np.int32) * nb
    n = np.abs(rel)
    max_exact = nb // 2
    large = max_exact + (np.log(np.maximum(n, 1).astype(np.float32) / max_exact)
                         / math.log(REL_MAX_DIST / max_exact) * (nb - max_exact)).astype(np.int32)
    large = np.minimum(large, nb - 1)
    return ret + np.where(n < max_exact, n, large)


def _bucket_map():
    w = SWA_BLOCK
    cq = np.arange(w)[:, None]
    s = np.arange(3 * w)[None, :]
    rel = s - w - cq
    return np.where(np.abs(rel) <= SWA_WINDOW, _t5_buckets(rel), -1).astype(np.int32)


def _bias_kernel(bm_ref, tab_ref, o_ref):
    w = SWA_BLOCK
    pair = pl.program_id(0)
    bm = bm_ref[...]
    key = lax.broadcasted_iota(jnp.int32, bm.shape, 1)
    masked = jnp.full(bm.shape, MASK_VALUE, F32)
    for hh in range(2):
        head = 2 * pair + hh
        acc = masked
        for bkt in range(REL_BUCKETS):
            acc = jnp.where(bm == bkt, tab_ref[bkt, head], acc)
        hcols = slice(hh * 3 * w, (hh + 1) * 3 * w)
        o_ref[0, 0, :, hcols] = jnp.where(key >= w, acc, masked)
        o_ref[1, 0, :, hcols] = acc
        o_ref[2, 0, :, hcols] = jnp.where(key < 2 * w, acc, masked)


def _bias_tables(rel_bias):
    w = SWA_BLOCK
    pairs = SWA_Q_HEADS // 2
    return pl.pallas_call(
        _bias_kernel,
        grid=(pairs,),
        in_specs=[pl.BlockSpec((w, 3 * w), lambda p: (0, 0)),
                  pl.BlockSpec(memory_space=pltpu.SMEM)],
        out_specs=pl.BlockSpec((3, 1, w, 6 * w), lambda p: (0, p, 0, 0)),
        out_shape=jax.ShapeDtypeStruct((3, pairs, w, 6 * w), F32),
        compiler_params=pltpu.CompilerParams(dimension_semantics=("arbitrary",)),
        name="swa_bias",
    )(jnp.asarray(_bucket_map()), rel_bias.astype(F32))


def _swa_kernel(q_ref, k_ref, v_ref, bias_ref, sink_ref, o_ref, ka_s, kb_s, va_s, vb_s,
                lg_s, p_s, st_s, *, seq):
    w = SWA_BLOCK
    nb = seq // w
    kvh = pl.program_id(1)
    pairs = SWA_Q_HEADS // SWA_KV_HEADS // 2
    stage_rows = 512
    lane_s = lax.broadcasted_iota(jnp.int16, (stage_rows, LANES), 1)
    lo = lane_s < HALF
    own = lo == (kvh == 0)
    for s_ref in (ka_s, kb_s, va_s, vb_s):
        zpad = jnp.zeros((w, s_ref.shape[1]), BF16)
        s_ref[0:w, :] = zpad
        s_ref[w + seq:2 * w + seq, :] = zpad

    def stage(i, carry):
        src = pl.ds(pl.multiple_of(i * stage_rows, stage_rows), stage_rows)
        dst = pl.ds(pl.multiple_of(i * stage_rows + w, w), stage_rows)
        k2 = k_ref[src, :]
        v2 = v_ref[src, :]
        kk = jnp.where(own, k2, pltpu.roll(k2, HALF, axis=1))
        vv = jnp.where(own, v2, pltpu.roll(v2, HALF, axis=1))
        zero = jnp.zeros_like(kk)
        one = jnp.ones_like(kk)
        ka_s[dst, :] = jnp.where(lo, kk, zero)
        kb_s[dst, :] = jnp.where(lo, zero, kk)
        va_s[dst, 0:LANES] = jnp.where(lo, vv, zero)
        va_s[dst, LANES:2 * LANES] = jnp.where(lo, one, zero)
        vb_s[dst, 0:LANES] = jnp.where(lo, zero, vv)
        vb_s[dst, LANES:2 * LANES] = jnp.where(lo, zero, one)
        return carry

    lax.fori_loop(0, seq // stage_rows, stage, 0)

    lo_w = lax.broadcasted_iota(jnp.int32, (w, LANES), 1) < HALF

    def logits_stage(n, variant):
        qrows = pl.ds(pl.multiple_of(n * w, w), w)
        band = pl.ds(pl.multiple_of(n * w, w), 3 * w)
        kt = jnp.concatenate([ka_s[band, :], kb_s[band, :]], axis=0)
        for pair in range(pairs):
            cols = slice(pair * LANES, (pair + 1) * LANES)
            lg_s[pair] = _dot_nt(q_ref[qrows, cols], kt) + bias_ref[variant, pair]

    def softmax_stage():
        for pair in range(pairs):
            sink_terms = []
            for hh in range(2):
                hcols = slice(hh * 3 * w, (hh + 1) * 3 * w)
                lg = lg_s[pair, :, hcols]
                sink = sink_ref[kvh * 2 * pairs + 2 * pair + hh]
                m = jnp.maximum(jnp.max(lg, axis=-1, keepdims=True), sink)
                p_s[pair, :, hcols] = jnp.exp(lg - m).astype(BF16)
                sink_terms.append(jnp.exp(sink - m))
            st_s[pair] = jnp.where(lo_w, sink_terms[0], sink_terms[1])

    def output_stage(n):
        qrows = pl.ds(pl.multiple_of(n * w, w), w)
        band = pl.ds(pl.multiple_of(n * w, w), 3 * w)
        vx = jnp.concatenate([va_s[band, :], vb_s[band, :]], axis=0)
        for pair in range(pairs):
            ox = _dot(p_s[pair], vx)
            o = ox[:, :LANES] / (ox[:, LANES:] + st_s[pair])
            o_ref[qrows, pair * LANES:(pair + 1) * LANES] = o.astype(o_ref.dtype)

    logits_stage(0, 0)
    softmax_stage()
    logits_stage(1, 1)

    def steady(n, carry):
        output_stage(n - 1)
        softmax_stage()
        logits_stage(n + 1, 1)
        return carry

    lax.fori_loop(1, nb - 2, steady, 0, unroll=2)
    output_stage(nb - 3)
    softmax_stage()
    logits_stage(nb - 1, 2)
    output_stage(nb - 2)
    softmax_stage()
    output_stage(nb - 1)


def _swa(q, k, v, bias, sink, batch, seq):
    t = batch * seq
    w = SWA_BLOCK
    qw = Q_SWA_W // SWA_KV_HEADS
    pairs = SWA_Q_HEADS // SWA_KV_HEADS // 2
    staged_k = pltpu.VMEM((seq + 2 * w, LANES), BF16)
    staged_v = pltpu.VMEM((seq + 2 * w, 2 * LANES), BF16)
    return pl.pallas_call(
        functools.partial(_swa_kernel, seq=seq),
        grid=(batch, SWA_KV_HEADS),
        in_specs=[pl.BlockSpec((seq, qw), lambda b, j: (b, j)),
                  pl.BlockSpec((seq, LANES), lambda b, j: (b, 0)),
                  pl.BlockSpec((seq, LANES), lambda b, j: (b, 0)),
                  pl.BlockSpec((3, pairs, w, 6 * w), lambda b, j: (0, j, 0, 0)),
                  pl.BlockSpec(memory_space=pltpu.SMEM)],
        out_specs=pl.BlockSpec((seq, qw), lambda b, j: (b, j)),
        out_shape=jax.ShapeDtypeStruct((t, Q_SWA_W), BF16),
        scratch_shapes=[staged_k, staged_k, staged_v, staged_v,
                        pltpu.VMEM((pairs, w, 6 * w), F32),
                        pltpu.VMEM((pairs, w, 6 * w), BF16),
                        pltpu.VMEM((pairs, w, LANES), F32)],
        compiler_params=pltpu.CompilerParams(
            dimension_semantics=("arbitrary", "arbitrary"), vmem_limit_bytes=VMEM_LIMIT),
        name="swa",
    )(q, k, v, bias, sink)


def _ffn_kernel(x_ref, oa_ref, os_ref, wout_ref, gpost_ref, gpre_ref, wup_ref, wdn_ref,
                gout_ref, o_ref):
    slab = FFN_TOKEN_BLOCK // FFN_SLABS
    slabs = [slice(s * slab, (s + 1) * slab) for s in range(FFN_SLABS)]
    mixes = [_dot(oa_ref[rows, :], wout_ref[0:V_GLA_W, :])
             + _dot(os_ref[rows, :], wout_ref[V_GLA_W:V_GLA_W + Q_SWA_W, :]) for rows in slabs]
    for rows, mix in zip(slabs, mixes):
        h = x_ref[rows, :] + _rms(mix) * gpost_ref[...]
        hn = (_rms(h) * gpre_ref[...]).astype(BF16)
        acc = jnp.zeros(h.shape, F32)
        for j in range(D_FF // FF_BLOCK):
            cols = slice(j * FF_BLOCK, (j + 1) * FF_BLOCK)
            zj = jnp.maximum(_dot(hn, wup_ref[:, cols]), 0.0)
            acc = acc + _dot((zj * zj).astype(BF16), wdn_ref[cols, :])
        o_ref[rows, :] = h + _rms(acc) * gout_ref[...]


def _ffn(x2, oa, os_, wout, gpost, gpre, wup, wdn, gout):
    t = x2.shape[0]
    row = lambda width: pl.BlockSpec((FFN_TOKEN_BLOCK, width), lambda i: (i, 0))
    return pl.pallas_call(
        _ffn_kernel,
        grid=(t // FFN_TOKEN_BLOCK,),
        in_specs=[row(D_MODEL), row(V_GLA_W), row(Q_SWA_W),
                  _const_spec((V_GLA_W + Q_SWA_W, D_MODEL)),
                  _const_spec((1, D_MODEL)), _const_spec((1, D_MODEL)),
                  _const_spec((D_MODEL, D_FF)), _const_spec((D_FF, D_MODEL)),
                  _const_spec((1, D_MODEL))],
        out_specs=row(D_MODEL),
        out_shape=jax.ShapeDtypeStruct((t, D_MODEL), F32),
        compiler_params=pltpu.CompilerParams(
            dimension_semantics=("arbitrary",), vmem_limit_bytes=VMEM_LIMIT),
        name="outproj_ffn",
    )(x2, oa, os_, wout, gpost, gpre, wup, wdn, gout)


def _layer(h, norm_mix_pre, w_in, w_gu_f, b_g_f, w_gu_b, b_g_b, gla_norm, swa_sink, bias_tab,
           w_out, norm_mix_post, norm_mlp_pre, w_up, w_down, norm_mlp_post):
    batch, seq, _ = h.shape
    x2 = h.reshape(batch * seq, D_MODEL)
    row = lambda g: g.reshape(1, -1).astype(F32)

    later = [w.astype(F32) for w in (w_out, w_up, w_down)]
    qa, ka, va, ga, qs, ks, vs, z, w_out_b, w_up_b, w_down_b = _inproj(
        x2, row(norm_mix_pre), w_in.astype(F32).T, later)

    wf = w_gu_f.reshape(GLA_RANK, GLA_HEADS, GLA_DK).transpose(1, 0, 2)
    wb = w_gu_b.reshape(GLA_RANK, GLA_HEADS, GLA_DK).transpose(1, 0, 2)
    zeros = jnp.zeros_like(wf)
    wgu = jnp.concatenate([jnp.concatenate([wf, zeros], axis=2),
                           jnp.concatenate([zeros, wb], axis=2)], axis=1).astype(BF16)
    bg = jnp.concatenate([b_g_f.reshape(GLA_HEADS, 1, GLA_DK),
                          b_g_b.reshape(GLA_HEADS, 1, GLA_DK)], axis=2).astype(F32)
    o_a = _gla(qa, ka, va, ga, z, wgu, bg, row(gla_norm), batch, seq)

    o_s = _swa(qs, ks, vs, bias_tab, swa_sink.astype(F32), batch, seq)

    out = _ffn(x2, o_a, o_s, w_out_b, row(norm_mix_post), row(norm_mlp_pre), w_up_b, w_down_b,
               row(norm_mlp_post))
    return out.reshape(batch, seq, D_MODEL)


def kernel(x, norm_mix_pre, w_in, w_gate_up_fwd, b_gate_fwd, w_gate_up_bwd, b_gate_bwd, gla_norm,
           swa_sink, rel_bias, w_out, norm_mix_post, norm_mlp_pre, w_up, w_down, norm_mlp_post):
    bias_tab = _bias_tables(rel_bias)
    h = x
    for l in range(w_in.shape[0]):
        h = _layer(h, norm_mix_pre[l], w_in[l], w_gate_up_fwd[l], b_gate_fwd[l], w_gate_up_bwd[l],
                   b_gate_bwd[l], gla_norm[l], swa_sink[l], bias_tab, w_out[l], norm_mix_post[l],
                   norm_mlp_pre[l], w_up[l], w_down[l], norm_mlp_post[l])
    return h
```

```python
import functools
import math

import jax
import jax.numpy as jnp
import numpy as np
from jax import lax
from jax.experimental import pallas as pl
from jax.experimental.pallas import tpu as pltpu

F32 = jnp.float32
BF16 = jnp.bfloat16

D_MODEL = 1024
GLA_HEADS = 4
GLA_DK = 64
GLA_DV = 128
GLA_RANK = 16
GLA_GATE_NORM = 16.0
GLA_CHUNK = 64
SWA_DH = 64
SWA_Q_HEADS = 8
SWA_KV_HEADS = 2
SWA_BLOCK = 128
SWA_WINDOW = 128
REL_BUCKETS = 32
REL_MAX_DIST = 128
D_FF = 4 * D_MODEL
NORM_EPS = 1e-6
MASK_VALUE = -1e30

LANES = 128
HALF = LANES // 2

QK_GLA_W = GLA_HEADS * GLA_DK
V_GLA_W = GLA_HEADS * GLA_DV
G_GLA_W = GLA_HEADS * GLA_DV
Q_SWA_W = SWA_Q_HEADS * SWA_DH
KV_SWA_W = SWA_KV_HEADS * SWA_DH
Z_W = 2 * GLA_RANK
IN_WIDTHS = (QK_GLA_W, QK_GLA_W, V_GLA_W, G_GLA_W, Q_SWA_W, KV_SWA_W, KV_SWA_W, Z_W)
IN_COLS = sum(IN_WIDTHS)
IN_DOT_GROUPS = ((0, 1), (2,), (3,), (4,), (5, 6), (7,))
N_LATER_WEIGHTS = 3

GLA_UNROLL = 64
GLA_GROUP = 256
GLA_PREP_UNROLL = 12
GLA_OUT_UNROLL = 15
TOKEN_BLOCK = 1024
FFN_TOKEN_BLOCK = 1024
FF_BLOCK = 1024
FFN_SLABS = 4
VMEM_LIMIT = 56 * 1024 * 1024


def _rms(x):
    return x * lax.rsqrt(jnp.mean(x * x, axis=-1, keepdims=True) + NORM_EPS)


def _dot(a, b):
    return jnp.dot(a, b, preferred_element_type=F32)


def _dot_nt(a, b):
    return lax.dot_general(a, b, (((1,), (1,)), ((), ())), preferred_element_type=F32)


def _dot_tn(a, b):
    return lax.dot_general(a, b, (((0,), (0,)), ((), ())), preferred_element_type=F32)


def _const_spec(shape):
    nd = len(shape)
    return pl.BlockSpec(shape, lambda *_: (0,) * nd, pipeline_mode=pl.Buffered(1))


def _rows(item, size):
    start = item * size
    if not isinstance(start, int):
        start = pl.multiple_of(start, size)
    return pl.ds(start, size)


def _software_pipeline(n_items, stages, unroll=1):
    depth = len(stages)

    def trip(t, static):
        for k in reversed(range(depth)):
            if not static or 0 <= t - k < n_items:
                stages[k](t - k)

    for t in range(depth - 1):
        trip(t, True)

    def steady(t, carry):
        trip(t, False)
        return carry

    lax.fori_loop(depth - 1, n_items, steady, 0, unroll=unroll)
    for t in range(n_items, n_items + depth - 1):
        trip(t, True)


def _stage_w_in(wt_ref, w_s):
    z_src = 2 * QK_GLA_W + V_GLA_W + G_GLA_W
    tail = z_src + Z_W
    step = 128

    def copy_rows(src, dst, n_rows, scale):
        def body(i, carry):
            w = wt_ref[pl.ds(src + i * step, step), :]
            w_s[pl.ds(dst + i * step, step), :] = (w if scale == 1.0 else w * scale).astype(BF16)
            return carry
        lax.fori_loop(0, n_rows // step, body, 0)

    copy_rows(0, 0, QK_GLA_W, GLA_DK ** -0.5)
    copy_rows(QK_GLA_W, QK_GLA_W, z_src - QK_GLA_W, 1.0)
    copy_rows(tail, z_src, Q_SWA_W, SWA_DH ** -0.5)
    copy_rows(tail + Q_SWA_W, z_src + Q_SWA_W, 2 * KV_SWA_W, 1.0)
    w_s[IN_COLS - Z_W:IN_COLS, :] = wt_ref[z_src:tail, :].astype(BF16)


def _inproj_kernel(x_ref, g_ref, w_ref, *refs):
    n_proj = len(IN_WIDTHS)
    later_f32, refs = refs[:N_LATER_WEIGHTS], refs[N_LATER_WEIGHTS:]
    out_refs, later_bf16, w_s = refs[:n_proj], refs[n_proj:-1], refs[-1]

    @pl.when(pl.program_id(0) == 0)
    def _():
        _stage_w_in(w_ref, w_s)

    for src, dst in zip(later_f32, later_bf16):
        dst[...] = src[...].astype(BF16)

    u = (_rms(x_ref[...]) * g_ref[...]).astype(BF16)
    off = 0
    for group in IN_DOT_GROUPS:
        width = sum(IN_WIDTHS[i] for i in group)
        y = _dot_nt(u, w_s[off:off + width, :])
        off += width
        col = 0
        for i in group:
            out_refs[i][...] = y[:, col:col + IN_WIDTHS[i]].astype(out_refs[i].dtype)
            col += IN_WIDTHS[i]


def _inproj(x2, g, w, later_weights):
    t = x2.shape[0]
    steps = t // TOKEN_BLOCK
    row = lambda width: pl.BlockSpec((TOKEN_BLOCK, width), lambda i: (i, 0))
    widths = IN_WIDTHS
    dtypes = (BF16,) * (len(IN_WIDTHS) - 1) + (F32,)
    assert len(later_weights) == N_LATER_WEIGHTS
    slices = [pl.BlockSpec((lw.shape[0] // steps, lw.shape[1]), lambda i: (i, 0))
              for lw in later_weights]
    return pl.pallas_call(
        _inproj_kernel,
        grid=(steps,),
        in_specs=[row(D_MODEL), _const_spec((1, D_MODEL)), _const_spec((IN_COLS, D_MODEL))] + slices,
        out_specs=[row(wd) for wd in widths] + slices,
        out_shape=([jax.ShapeDtypeStruct((t, wd), dt) for wd, dt in zip(widths, dtypes)]
                   + [jax.ShapeDtypeStruct(lw.shape, BF16) for lw in later_weights]),
        scratch_shapes=[pltpu.VMEM((IN_COLS, D_MODEL), BF16)],
        compiler_params=pltpu.CompilerParams(
            dimension_semantics=("arbitrary",), vmem_limit_bytes=VMEM_LIMIT),
        name="inproj",
    )(x2, g, w, *later_weights)


def _cumsum_matrix():
    r = np.arange(GLA_GROUP)[:, None]
    s = np.arange(GLA_GROUP)[None, :]
    return (((r // GLA_CHUNK) == (s // GLA_CHUNK)) & (s <= r)).astype(np.float32)


def _gla_kernel(q_ref, k_ref, v_ref, g_ref, z_ref, wgu_ref, bg_ref, nrm_ref, cm_ref, o_ref,
                la_s, hl_s, cs_s, qdec_s, kina_s, kinb_s, kst_s, am_s, ds_s, dec_s, sc_s, *, seq):
    c = GLA_CHUNK
    nc = seq // c
    grp = GLA_GROUP
    cpg = grp // c
    n_groups = seq // grp
    wgu = wgu_ref[0]
    bg = bg_ref[0]
    fwd_p = lax.broadcasted_iota(jnp.int16, (grp, LANES), 1) < HALF
    own_p = fwd_p == (pl.program_id(1) % 2 == 0)
    zero_p = jnp.zeros((grp, LANES), BF16)

    def own_head_twice(tile):
        return jnp.where(own_p, tile, pltpu.roll(tile, HALF, axis=1)).astype(F32)

    fwd_3 = lax.broadcasted_iota(jnp.int32, (cpg, c, LANES), 2) < HALF
    lane_c = lax.broadcasted_iota(jnp.int32, (c, LANES), 1)
    row_c = lax.broadcasted_iota(jnp.int32, (c, LANES), 0)
    tri = ((lane_c < HALF) & (lane_c <= row_c)) | ((lane_c >= HALF) & ((lane_c - HALF) >= row_c))
    zero_c = jnp.zeros((c, LANES), F32)

    def gate_preact(i):
        rows = _rows(i, grp)
        la_s[rows, :] = _dot(z_ref[rows, :].astype(BF16), wgu)

    def decay_logs(i):
        rows = _rows(i, grp)
        pre = la_s[rows, :] + bg
        la = ((jnp.minimum(pre, 0.0) - jnp.log(1.0 + jnp.exp(-jnp.abs(pre))))
              * (1.0 / GLA_GATE_NORM))
        la_hi = la.astype(BF16)
        la_s[rows, :] = la
        hl_s[rows, 0:LANES] = la_hi
        hl_s[rows, LANES:2 * LANES] = (la - la_hi.astype(F32)).astype(BF16)

    def chunk_cumsum(i):
        rows = _rows(i, grp)
        cs = _dot(cm_ref[...], hl_s[rows, :])
        cs_s[rows, :] = cs[:, :LANES] + cs[:, LANES:]

    def decayed_qk(i):
        rows = _rows(i, grp)
        la3 = la_s[rows, :].reshape(cpg, c, LANES)
        cum3 = cs_s[rows, :].reshape(cpg, c, LANES)
        tot = jnp.sum(la3, axis=1, keepdims=True)
        b3 = jnp.where(fwd_3, cum3, tot - cum3 + la3)
        b = b3.reshape(grp, LANES)
        rest = jnp.exp(tot - b3).reshape(grp, LANES)
        qq = own_head_twice(q_ref[rows, :])
        kk = own_head_twice(k_ref[rows, :])
        kin = (kk * jnp.exp(-b)).astype(BF16)
        qdec_s[rows, :] = (qq * jnp.exp(b)).astype(BF16)
        kina_s[rows, :] = jnp.where(fwd_p, kin, zero_p)
        kinb_s[rows, :] = jnp.where(fwd_p, zero_p, kin)
        kst_s[rows, :] = (kk * rest).astype(BF16)
        dec_s[_rows(i, cpg)] = jnp.broadcast_to(jnp.exp(tot), (cpg, 8, LANES))

    def chunk_products(i):
        for cc in range(cpg):
            n = i * cpg + cc
            rows = _rows(n, c)
            kin_bd = jnp.concatenate([kina_s[rows, :], kinb_s[rows, :]], axis=0)
            a2 = _dot_nt(qdec_s[rows, :], kin_bd)
            am_s[rows, :] = jnp.where(tri, a2, zero_c).astype(BF16)
            ds_s[n] = _dot_tn(v_ref[rows, :], kst_s[rows, :])

    _software_pipeline(n_groups, [gate_preact, decay_logs, chunk_cumsum, decayed_qk,
                                  chunk_products], unroll=GLA_PREP_UNROLL)

    fwd_sq = lax.broadcasted_iota(jnp.int32, (LANES, LANES), 1) < HALF
    fwd_row = lax.broadcasted_iota(jnp.int32, (8, LANES), 1) < HALF

    def scan_step(i, s):
        j = nc - 1 - i
        sb = s.astype(BF16)
        sc_s[i, :, 0:HALF] = sb[:, 0:HALF]
        sc_s[j, :, HALF:LANES] = sb[:, HALF:LANES]
        dec = jnp.where(fwd_row, dec_s[i], dec_s[j])[0:1]
        return dec * s + jnp.where(fwd_sq, ds_s[i], ds_s[j])

    lax.fori_loop(0, nc, scan_step, jnp.zeros((LANES, LANES), F32), unroll=GLA_UNROLL)

    nrm = nrm_ref[...]

    def mix(i):
        for cc in range(cpg):
            n = i * cpg + cc
            rows = _rows(n, c)
            vc = v_ref[rows, :]
            vvt = jnp.concatenate([vc, vc], axis=0).T
            la_s[rows, :] = _dot_nt(jnp.concatenate([qdec_s[rows, :], am_s[rows, :]], axis=1),
                                    jnp.concatenate([sc_s[n], vvt], axis=1))

    def norm_gate(i):
        rows = _rows(i, grp)
        gate = g_ref[rows, :].astype(F32)
        o = (_rms(la_s[rows, :]) * nrm) * (gate / (1.0 + jnp.exp(-gate)))
        o_ref[rows, :] = o.astype(o_ref.dtype)

    _software_pipeline(n_groups, [mix, norm_gate], unroll=GLA_OUT_UNROLL)


def _gla(q, k, v, g, z, wgu, bg, nrm, batch, seq):
    t = batch * seq
    nc = seq // GLA_CHUNK
    tile = pl.BlockSpec((seq, LANES), lambda b, h: (b, h))
    pair_tile = pl.BlockSpec((seq, LANES), lambda b, h: (b, h // 2))
    tok_bf = pltpu.VMEM((seq, LANES), BF16)
    cm = jnp.asarray(_cumsum_matrix(), dtype=BF16)
    return pl.pallas_call(
        functools.partial(_gla_kernel, seq=seq),
        grid=(batch, GLA_HEADS),
        in_specs=[pair_tile, pair_tile, tile, tile,
                  pl.BlockSpec((seq, Z_W), lambda b, h: (b, 0)),
                  pl.BlockSpec((1, Z_W, LANES), lambda b, h: (h, 0, 0)),
                  pl.BlockSpec((1, 1, LANES), lambda b, h: (h, 0, 0)),
                  pl.BlockSpec((1, LANES), lambda b, h: (0, 0)),
                  _const_spec(cm.shape)],
        out_specs=tile,
        out_shape=jax.ShapeDtypeStruct((t, V_GLA_W), BF16),
        scratch_shapes=[pltpu.VMEM((seq, LANES), F32),
                        pltpu.VMEM((seq, 2 * LANES), BF16),
                        pltpu.VMEM((seq, LANES), F32),
                        tok_bf, tok_bf, tok_bf, tok_bf, tok_bf,
                        pltpu.VMEM((nc, LANES, LANES), F32),
                        pltpu.VMEM((nc, 8, LANES), F32),
                        pltpu.VMEM((nc, LANES, LANES), BF16)],
        compiler_params=pltpu.CompilerParams(
            dimension_semantics=("arbitrary", "arbitrary"), vmem_limit_bytes=VMEM_LIMIT),
        name="gla",
    )(q, k, v, g, z, wgu, bg, nrm, cm)


def _t5_buckets(rel):
    nb = REL_BUCKETS // 2
    ret = (rel > 0).astype(np.int32) * nb
    n = np.abs(rel)
    max_exact = nb // 2
    large = max_exact + (np.log(np.maximum(n, 1).astype(np.float32) / max_exact)
                         / math.log(REL_MAX_DIST / max_exact) * (nb - max_exact)).astype(np.int32)
    large = np.minimum(large, nb - 1)
    return ret + np.where(n < max_exact, n, large)


def _bucket_map():
    w = SWA_BLOCK
    cq = np.arange(w)[:, None]
    s = np.arange(3 * w)[None, :]
    rel = s - w - cq
    return np.where(np.abs(rel) <= SWA_WINDOW, _t5_buckets(rel), -1).astype(np.int32)


def _bias_kernel(bm_ref, tab_ref, o_ref):
    w = SWA_BLOCK
    pair = pl.program_id(0)
    bm = bm_ref[...]
    key = lax.broadcasted_iota(jnp.int32, bm.shape, 1)
    masked = jnp.full(bm.shape, MASK_VALUE, F32)
    for hh in range(2):
        head = 2 * pair + hh
        acc = masked
        for bkt in range(REL_BUCKETS):
            acc = jnp.where(bm == bkt, tab_ref[bkt, head], acc)
        hcols = slice(hh * 3 * w, (hh + 1) * 3 * w)
        o_ref[0, 0, :, hcols] = jnp.where(key >= w, acc, masked)
        o_ref[1, 0, :, hcols] = acc
        o_ref[2, 0, :, hcols] = jnp.where(key < 2 * w, acc, masked)


def _bias_tables(rel_bias):
    w = SWA_BLOCK
    pairs = SWA_Q_HEADS // 2
    return pl.pallas_call(
        _bias_kernel,
        grid=(pairs,),
        in_specs=[pl.BlockSpec((w, 3 * w), lambda p: (0, 0)),
                  pl.BlockSpec(memory_space=pltpu.SMEM)],
        out_specs=pl.BlockSpec((3, 1, w, 6 * w), lambda p: (0, p, 0, 0)),
        out_shape=jax.ShapeDtypeStruct((3, pairs, w, 6 * w), F32),
        compiler_params=pltpu.CompilerParams(dimension_semantics=("arbitrary",)),
        name="swa_bias",
    )(jnp.asarray(_bucket_map()), rel_bias.astype(F32))


def _swa_kernel(q_ref, k_ref, v_ref, bias_ref, sink_ref, o_ref, ka_s, kb_s, va_s, vb_s,
                lg_s, p_s, st_s, *, seq):
    w = SWA_BLOCK
    nb = seq // w
    kvh = pl.program_id(1)
    pairs = SWA_Q_HEADS // SWA_KV_HEADS // 2
    stage_rows = 512
    lane_s = lax.broadcasted_iota(jnp.int16, (stage_rows, LANES), 1)
    lo = lane_s < HALF
    own = lo == (kvh == 0)
    for s_ref in (ka_s, kb_s, va_s, vb_s):
        zpad = jnp.zeros((w, s_ref.shape[1]), BF16)
        s_ref[0:w, :] = zpad
        s_ref[w + seq:2 * w + seq, :] = zpad

    def stage(i, carry):
        src = pl.ds(pl.multiple_of(i * stage_rows, stage_rows), stage_rows)
        dst = pl.ds(pl.multiple_of(i * stage_rows + w, w), stage_rows)
        k2 = k_ref[src, :]
        v2 = v_ref[src, :]
        kk = jnp.where(own, k2, pltpu.roll(k2, HALF, axis=1))
        vv = jnp.where(own, v2, pltpu.roll(v2, HALF, axis=1))
        zero = jnp.zeros_like(kk)
        one = jnp.ones_like(kk)
        ka_s[dst, :] = jnp.where(lo, kk, zero)
        kb_s[dst, :] = jnp.where(lo, zero, kk)
        va_s[dst, 0:LANES] = jnp.where(lo, vv, zero)
        va_s[dst, LANES:2 * LANES] = jnp.where(lo, one, zero)
        vb_s[dst, 0:LANES] = jnp.where(lo, zero, vv)
        vb_s[dst, LANES:2 * LANES] = jnp.where(lo, zero, one)
        return carry

    lax.fori_loop(0, seq // stage_rows, stage, 0)

    lo_w = lax.broadcasted_iota(jnp.int32, (w, LANES), 1) < HALF

    def logits_stage(n, variant):
        qrows = pl.ds(pl.multiple_of(n * w, w), w)
        band = pl.ds(pl.multiple_of(n * w, w), 3 * w)
        kt = jnp.concatenate([ka_s[band, :], kb_s[band, :]], axis=0)
        for pair in range(pairs):
            cols = slice(pair * LANES, (pair + 1) * LANES)
            lg_s[pair] = _dot_nt(q_ref[qrows, cols], kt) + bias_ref[variant, pair]

    def softmax_stage():
        for pair in range(pairs):
            sink_terms = []
            for hh in range(2):
                hcols = slice(hh * 3 * w, (hh + 1) * 3 * w)
                lg = lg_s[pair, :, hcols]
                sink = sink_ref[kvh * 2 * pairs + 2 * pair + hh]
                m = jnp.maximum(jnp.max(lg, axis=-1, keepdims=True), sink)
                p_s[pair, :, hcols] = jnp.exp(lg - m).astype(BF16)
                sink_terms.append(jnp.exp(sink - m))
            st_s[pair] = jnp.where(lo_w, sink_terms[0], sink_terms[1])

    def output_stage(n):
        qrows = pl.ds(pl.multiple_of(n * w, w), w)
        band = pl.ds(pl.multiple_of(n * w, w), 3 * w)
        vx = jnp.concatenate([va_s[band, :], vb_s[band, :]], axis=0)
        for pair in range(pairs):
            ox = _dot(p_s[pair], vx)
            o = ox[:, :LANES] / (ox[:, LANES:] + st_s[pair])
            o_ref[qrows, pair * LANES:(pair + 1) * LANES] = o.astype(o_ref.dtype)

    logits_stage(0, 0)
    softmax_stage()
    logits_stage(1, 1)

    def steady(n, carry):
        output_stage(n - 1)
        softmax_stage()
        logits_stage(n + 1, 1)
        return carry

    lax.fori_loop(1, nb - 2, steady, 0, unroll=2)
    output_stage(nb - 3)
    softmax_stage()
    logits_stage(nb - 1, 2)
    output_stage(nb - 2)
    softmax_stage()
    output_stage(nb - 1)


def _swa(q, k, v, bias, sink, batch, seq):
    t = batch * seq
    w = SWA_BLOCK
    qw = Q_SWA_W // SWA_KV_HEADS
    pairs = SWA_Q_HEADS // SWA_KV_HEADS // 2
    staged_k = pltpu.VMEM((seq + 2 * w, LANES), BF16)
    staged_v = pltpu.VMEM((seq + 2 * w, 2 * LANES), BF16)
    return pl.pallas_call(
        functools.partial(_swa_kernel, seq=seq),
        grid=(batch, SWA_KV_HEADS),
        in_specs=[pl.BlockSpec((seq, qw), lambda b, j: (b, j)),
                  pl.BlockSpec((seq, LANES), lambda b, j: (b, 0)),
                  pl.BlockSpec((seq, LANES), lambda b, j: (b, 0)),
                  pl.BlockSpec((3, pairs, w, 6 * w), lambda b, j: (0, j, 0, 0)),
                  pl.BlockSpec(memory_space=pltpu.SMEM)],
        out_specs=pl.BlockSpec((seq, qw), lambda b, j: (b, j)),
        out_shape=jax.ShapeDtypeStruct((t, Q_SWA_W), BF16),
        scratch_shapes=[staged_k, staged_k, staged_v, staged_v,
                        pltpu.VMEM((pairs, w, 6 * w), F32),
                        pltpu.VMEM((pairs, w, 6 * w), BF16),
                        pltpu.VMEM((pairs, w, LANES), F32)],
        compiler_params=pltpu.CompilerParams(
            dimension_semantics=("arbitrary", "arbitrary"), vmem_limit_bytes=VMEM_LIMIT),
        name="swa",
    )(q, k, v, bias, sink)


def _ffn_kernel(x_ref, oa_ref, os_ref, wout_ref, gpost_ref, gpre_ref, wup_ref, wdn_ref,
                gout_ref, o_ref):
    slab = FFN_TOKEN_BLOCK // FFN_SLABS
    slabs = [slice(s * slab, (s + 1) * slab) for s in range(FFN_SLABS)]
    mixes = [_dot(oa_ref[rows, :], wout_ref[0:V_GLA_W, :])
             + _dot(os_ref[rows, :], wout_ref[V_GLA_W:V_GLA_W + Q_SWA_W, :]) for rows in slabs]
    for rows, mix in zip(slabs, mixes):
        h = x_ref[rows, :] + _rms(mix) * gpost_ref[...]
        hn = (_rms(h) * gpre_ref[...]).astype(BF16)
        acc = jnp.zeros(h.shape, F32)
        for j in range(D_FF // FF_BLOCK):
            cols = slice(j * FF_BLOCK, (j + 1) * FF_BLOCK)
            zj = jnp.maximum(_dot(hn, wup_ref[:, cols]), 0.0)
            acc = acc + _dot((zj * zj).astype(BF16), wdn_ref[cols, :])
        o_ref[rows, :] = h + _rms(acc) * gout_ref[...]


def _ffn(x2, oa, os_, wout, gpost, gpre, wup, wdn, gout):
    t = x2.shape[0]
    row = lambda width: pl.BlockSpec((FFN_TOKEN_BLOCK, width), lambda i: (i, 0))
    return pl.pallas_call(
        _ffn_kernel,
        grid=(t // FFN_TOKEN_BLOCK,),
        in_specs=[row(D_MODEL), row(V_GLA_W), row(Q_SWA_W),
                  _const_spec((V_GLA_W + Q_SWA_W, D_MODEL)),
                  _const_spec((1, D_MODEL)), _const_spec((1, D_MODEL)),
                  _const_spec((D_MODEL, D_FF)), _const_spec((D_FF, D_MODEL)),
                  _const_spec((1, D_MODEL))],
        out_specs=row(D_MODEL),
        out_shape=jax.ShapeDtypeStruct((t, D_MODEL), F32),
        compiler_params=pltpu.CompilerParams(
            dimension_semantics=("arbitrary",), vmem_limit_bytes=VMEM_LIMIT),
        name="outproj_ffn",
    )(x2, oa, os_, wout, gpost, gpre, wup, wdn, gout)


def _layer(h, norm_mix_pre, w_in, w_gu_f, b_g_f, w_gu_b, b_g_b, gla_norm, swa_sink, bias_tab,
           w_out, norm_mix_post, norm_mlp_pre, w_up, w_down, norm_mlp_post):
    batch, seq, _ = h.shape
    x2 = h.reshape(batch * seq, D_MODEL)
    row = lambda g: g.reshape(1, -1).astype(F32)

    later = [w.astype(F32) for w in (w_out, w_up, w_down)]
    qa, ka, va, ga, qs, ks, vs, z, w_out_b, w_up_b, w_down_b = _inproj(
        x2, row(norm_mix_pre), w_in.astype(F32).T, later)

    wf = w_gu_f.reshape(GLA_RANK, GLA_HEADS, GLA_DK).transpose(1, 0, 2)
    wb = w_gu_b.reshape(GLA_RANK, GLA_HEADS, GLA_DK).transpose(1, 0, 2)
    zeros = jnp.zeros_like(wf)
    wgu = jnp.concatenate([jnp.concatenate([wf, zeros], axis=2),
                           jnp.concatenate([zeros, wb], axis=2)], axis=1).astype(BF16)
    bg = jnp.concatenate([b_g_f.reshape(GLA_HEADS, 1, GLA_DK),
                          b_g_b.reshape(GLA_HEADS, 1, GLA_DK)], axis=2).astype(F32)
    o_a = _gla(qa, ka, va, ga, z, wgu, bg, row(gla_norm), batch, seq)

    o_s = _swa(qs, ks, vs, bias_tab, swa_sink.astype(F32), batch, seq)

    out = _ffn(x2, o_a, o_s, w_out_b, row(norm_mix_post), row(norm_mlp_pre), w_up_b, w_down_b,
               row(norm_mlp_post))
    return out.reshape(batch, seq, D_MODEL)


def kernel(x, norm_mix_pre, w_in, w_gate_up_fwd, b_gate_fwd, w_gate_up_bwd, b_gate_bwd, gla_norm,
           swa_sink, rel_bias, w_out, norm_mix_post, norm_mlp_pre, w_up, w_down, norm_mlp_post):
    bias_tab = _bias_tables(rel_bias)
    h = x
    for l in range(w_in.shape[0]):
        h = _layer(h, norm_mix_pre[l], w_in[l], w_gate_up_fwd[l], b_gate_fwd[l], w_gate_up_bwd[l],
                   b_gate_bwd[l], gla_norm[l], swa_sink[l], bias_tab, w_out[l], norm_mix_post[l],
                   norm_mlp_pre[l], w_up[l], w_down[l], norm_mlp_post[l])
    return h
```

```python
import functools
import math

import jax
import jax.numpy as jnp
import numpy as np
from jax import lax
from jax.experimental import pallas as pl
from jax.experimental.pallas import tpu as pltpu

F32 = jnp.float32
BF16 = jnp.bfloat16

D_MODEL = 1024
GLA_HEADS = 4
GLA_DK = 64
GLA_DV = 128
GLA_RANK = 16
GLA_GATE_NORM = 16.0
GLA_CHUNK = 64
SWA_DH = 64
SWA_Q_HEADS = 8
SWA_KV_HEADS = 2
SWA_BLOCK = 128
SWA_WINDOW = 128
REL_BUCKETS = 32
REL_MAX_DIST = 128
D_FF = 4 * D_MODEL
NORM_EPS = 1e-6
MASK_VALUE = -1e30

LANES = 128
HALF = LANES // 2

QK_GLA_W = GLA_HEADS * GLA_DK
V_GLA_W = GLA_HEADS * GLA_DV
G_GLA_W = GLA_HEADS * GLA_DV
Q_SWA_W = SWA_Q_HEADS * SWA_DH
KV_SWA_W = SWA_KV_HEADS * SWA_DH
Z_W = 2 * GLA_RANK
IN_WIDTHS = (QK_GLA_W, QK_GLA_W, V_GLA_W, G_GLA_W, Q_SWA_W, KV_SWA_W, KV_SWA_W, Z_W)
IN_COLS = sum(IN_WIDTHS)
IN_DOT_GROUPS = ((0, 1), (2,), (3,), (4,), (5, 6), (7,))
N_LATER_WEIGHTS = 3

GLA_UNROLL = 64
SWA_UNROLL = 29
GLA_GROUP = 256
GLA_PREP_UNROLL = 12
GLA_OUT_UNROLL = 15
TOKEN_BLOCK = 1024
FFN_TOKEN_BLOCK = 1024
FF_BLOCK = 1024
FFN_SLABS = 4
VMEM_LIMIT = 56 * 1024 * 1024


def _rms(x):
    return x * lax.rsqrt(jnp.mean(x * x, axis=-1, keepdims=True) + NORM_EPS)


def _dot(a, b):
    return jnp.dot(a, b, preferred_element_type=F32)


def _dot_nt(a, b):
    return lax.dot_general(a, b, (((1,), (1,)), ((), ())), preferred_element_type=F32)


def _dot_tn(a, b):
    return lax.dot_general(a, b, (((0,), (0,)), ((), ())), preferred_element_type=F32)


def _const_spec(shape):
    nd = len(shape)
    return pl.BlockSpec(shape, lambda *_: (0,) * nd, pipeline_mode=pl.Buffered(1))


def _rows(item, size):
    start = item * size
    if not isinstance(start, int):
        start = pl.multiple_of(start, size)
    return pl.ds(start, size)


def _software_pipeline(n_items, stages, unroll=1):
    depth = len(stages)

    def trip(t, static):
        for k in reversed(range(depth)):
            if not static or 0 <= t - k < n_items:
                stages[k](t - k)

    for t in range(depth - 1):
        trip(t, True)

    def steady(t, carry):
        trip(t, False)
        return carry

    lax.fori_loop(depth - 1, n_items, steady, 0, unroll=unroll)
    for t in range(n_items, n_items + depth - 1):
        trip(t, True)


def _stage_w_in(wt_ref, w_s):
    z_src = 2 * QK_GLA_W + V_GLA_W + G_GLA_W
    tail = z_src + Z_W
    step = 128

    def copy_rows(src, dst, n_rows, scale):
        def body(i, carry):
            w = wt_ref[pl.ds(src + i * step, step), :]
            w_s[pl.ds(dst + i * step, step), :] = (w if scale == 1.0 else w * scale).astype(BF16)
            return carry
        lax.fori_loop(0, n_rows // step, body, 0)

    copy_rows(0, 0, QK_GLA_W, GLA_DK ** -0.5)
    copy_rows(QK_GLA_W, QK_GLA_W, z_src - QK_GLA_W, 1.0)
    copy_rows(tail, z_src, Q_SWA_W, SWA_DH ** -0.5)
    copy_rows(tail + Q_SWA_W, z_src + Q_SWA_W, 2 * KV_SWA_W, 1.0)
    w_s[IN_COLS - Z_W:IN_COLS, :] = wt_ref[z_src:tail, :].astype(BF16)


def _inproj_kernel(x_ref, g_ref, w_ref, *refs):
    n_proj = len(IN_WIDTHS)
    later_f32, refs = refs[:N_LATER_WEIGHTS], refs[N_LATER_WEIGHTS:]
    out_refs, later_bf16, w_s = refs[:n_proj], refs[n_proj:-1], refs[-1]

    @pl.when(pl.program_id(0) == 0)
    def _():
        _stage_w_in(w_ref, w_s)

    for src, dst in zip(later_f32, later_bf16):
        dst[...] = src[...].astype(BF16)

    u = (_rms(x_ref[...]) * g_ref[...]).astype(BF16)
    off = 0
    for group in IN_DOT_GROUPS:
        width = sum(IN_WIDTHS[i] for i in group)
        y = _dot_nt(u, w_s[off:off + width, :])
        off += width
        col = 0
        for i in group:
            out_refs[i][...] = y[:, col:col + IN_WIDTHS[i]].astype(out_refs[i].dtype)
            col += IN_WIDTHS[i]


def _inproj(x2, g, w, later_weights):
    t = x2.shape[0]
    steps = t // TOKEN_BLOCK
    row = lambda width: pl.BlockSpec((TOKEN_BLOCK, width), lambda i: (i, 0))
    widths = IN_WIDTHS
    dtypes = (BF16,) * (len(IN_WIDTHS) - 1) + (F32,)
    assert len(later_weights) == N_LATER_WEIGHTS
    slices = [pl.BlockSpec((lw.shape[0] // steps, lw.shape[1]), lambda i: (i, 0))
              for lw in later_weights]
    return pl.pallas_call(
        _inproj_kernel,
        grid=(steps,),
        in_specs=[row(D_MODEL), _const_spec((1, D_MODEL)), _const_spec((IN_COLS, D_MODEL))] + slices,
        out_specs=[row(wd) for wd in widths] + slices,
        out_shape=([jax.ShapeDtypeStruct((t, wd), dt) for wd, dt in zip(widths, dtypes)]
                   + [jax.ShapeDtypeStruct(lw.shape, BF16) for lw in later_weights]),
        scratch_shapes=[pltpu.VMEM((IN_COLS, D_MODEL), BF16)],
        compiler_params=pltpu.CompilerParams(
            dimension_semantics=("arbitrary",), vmem_limit_bytes=VMEM_LIMIT),
        name="inproj",
    )(x2, g, w, *later_weights)


def _cumsum_matrix():
    r = np.arange(GLA_GROUP)[:, None]
    s = np.arange(GLA_GROUP)[None, :]
    return (((r // GLA_CHUNK) == (s // GLA_CHUNK)) & (s <= r)).astype(np.float32)


def _gla_kernel(q_ref, k_ref, v_ref, g_ref, z_ref, wgu_ref, bg_ref, nrm_ref, cm_ref, o_ref,
                la_s, hl_s, cs_s, qdec_s, kina_s, kinb_s, kst_s, am_s, ds_s, dec_s, sc_s, *, seq):
    c = GLA_CHUNK
    nc = seq // c
    grp = GLA_GROUP
    cpg = grp // c
    n_groups = seq // grp
    wgu = wgu_ref[0]
    bg = bg_ref[0]
    fwd_p = lax.broadcasted_iota(jnp.int16, (grp, LANES), 1) < HALF
    own_p = fwd_p == (pl.program_id(1) % 2 == 0)
    zero_p = jnp.zeros((grp, LANES), BF16)

    def own_head_twice(tile):
        return jnp.where(own_p, tile, pltpu.roll(tile, HALF, axis=1)).astype(F32)

    fwd_3 = lax.broadcasted_iota(jnp.int32, (cpg, c, LANES), 2) < HALF
    lane_c = lax.broadcasted_iota(jnp.int32, (c, LANES), 1)
    row_c = lax.broadcasted_iota(jnp.int32, (c, LANES), 0)
    tri = ((lane_c < HALF) & (lane_c <= row_c)) | ((lane_c >= HALF) & ((lane_c - HALF) >= row_c))
    zero_c = jnp.zeros((c, LANES), F32)

    def gate_preact(i):
        rows = _rows(i, grp)
        la_s[rows, :] = _dot(z_ref[rows, :].astype(BF16), wgu)

    def decay_logs(i):
        rows = _rows(i, grp)
        pre = la_s[rows, :] + bg
        la = ((jnp.minimum(pre, 0.0) - jnp.log(1.0 + jnp.exp(-jnp.abs(pre))))
              * (1.0 / GLA_GATE_NORM))
        la_hi = la.astype(BF16)
        la_s[rows, :] = la
        hl_s[rows, 0:LANES] = la_hi
        hl_s[rows, LANES:2 * LANES] = (la - la_hi.astype(F32)).astype(BF16)

    def chunk_cumsum(i):
        rows = _rows(i, grp)
        cs = _dot(cm_ref[...], hl_s[rows, :])
        cs_s[rows, :] = cs[:, :LANES] + cs[:, LANES:]

    def decayed_qk(i):
        rows = _rows(i, grp)
        la3 = la_s[rows, :].reshape(cpg, c, LANES)
        cum3 = cs_s[rows, :].reshape(cpg, c, LANES)
        tot = jnp.sum(la3, axis=1, keepdims=True)
        b3 = jnp.where(fwd_3, cum3, tot - cum3 + la3)
        b = b3.reshape(grp, LANES)
        rest = jnp.exp(tot - b3).reshape(grp, LANES)
        qq = own_head_twice(q_ref[rows, :])
        kk = own_head_twice(k_ref[rows, :])
        kin = (kk * jnp.exp(-b)).astype(BF16)
        qdec_s[rows, :] = (qq * jnp.exp(b)).astype(BF16)
        kina_s[rows, :] = jnp.where(fwd_p, kin, zero_p)
        kinb_s[rows, :] = jnp.where(fwd_p, zero_p, kin)
        kst_s[rows, :] = (kk * rest).astype(BF16)
        dec_s[_rows(i, cpg)] = jnp.broadcast_to(jnp.exp(tot), (cpg, 8, LANES))

    def chunk_products(i):
        for cc in range(cpg):
            n = i * cpg + cc
            rows = _rows(n, c)
            kin_bd = jnp.concatenate([kina_s[rows, :], kinb_s[rows, :]], axis=0)
            a2 = _dot_nt(qdec_s[rows, :], kin_bd)
            am_s[rows, :] = jnp.where(tri, a2, zero_c).astype(BF16)
            ds_s[n] = _dot_tn(v_ref[rows, :], kst_s[rows, :])

    _software_pipeline(n_groups, [gate_preact, decay_logs, chunk_cumsum, decayed_qk,
                                  chunk_products], unroll=GLA_PREP_UNROLL)

    fwd_sq = lax.broadcasted_iota(jnp.int32, (LANES, LANES), 1) < HALF
    fwd_row = lax.broadcasted_iota(jnp.int32, (8, LANES), 1) < HALF

    def scan_step(i, s):
        j = nc - 1 - i
        sb = s.astype(BF16)
        sc_s[i, :, 0:HALF] = sb[:, 0:HALF]
        sc_s[j, :, HALF:LANES] = sb[:, HALF:LANES]
        dec = jnp.where(fwd_row, dec_s[i], dec_s[j])[0:1]
        return dec * s + jnp.where(fwd_sq, ds_s[i], ds_s[j])

    lax.fori_loop(0, nc, scan_step, jnp.zeros((LANES, LANES), F32), unroll=GLA_UNROLL)

    nrm = nrm_ref[...]

    def mix(i):
        for cc in range(cpg):
            n = i * cpg + cc
            rows = _rows(n, c)
            vc = v_ref[rows, :]
            vvt = jnp.concatenate([vc, vc], axis=0).T
            la_s[rows, :] = _dot_nt(jnp.concatenate([qdec_s[rows, :], am_s[rows, :]], axis=1),
                                    jnp.concatenate([sc_s[n], vvt], axis=1))

    def norm_gate(i):
        rows = _rows(i, grp)
        gate = g_ref[rows, :].astype(F32)
        o = (_rms(la_s[rows, :]) * nrm) * (gate / (1.0 + jnp.exp(-gate)))
        o_ref[rows, :] = o.astype(o_ref.dtype)

    _software_pipeline(n_groups, [mix, norm_gate], unroll=GLA_OUT_UNROLL)


def _gla(q, k, v, g, z, wgu, bg, nrm, batch, seq):
    t = batch * seq
    nc = seq // GLA_CHUNK
    tile = pl.BlockSpec((seq, LANES), lambda b, h: (b, h))
    pair_tile = pl.BlockSpec((seq, LANES), lambda b, h: (b, h // 2))
    tok_bf = pltpu.VMEM((seq, LANES), BF16)
    cm = jnp.asarray(_cumsum_matrix(), dtype=BF16)
    return pl.pallas_call(
        functools.partial(_gla_kernel, seq=seq),
        grid=(batch, GLA_HEADS),
        in_specs=[pair_tile, pair_tile, tile, tile,
                  pl.BlockSpec((seq, Z_W), lambda b, h: (b, 0)),
                  pl.BlockSpec((1, Z_W, LANES), lambda b, h: (h, 0, 0)),
                  pl.BlockSpec((1, 1, LANES), lambda b, h: (h, 0, 0)),
                  pl.BlockSpec((1, LANES), lambda b, h: (0, 0)),
                  _const_spec(cm.shape)],
        out_specs=tile,
        out_shape=jax.ShapeDtypeStruct((t, V_GLA_W), BF16),
        scratch_shapes=[pltpu.VMEM((seq, LANES), F32),
                        pltpu.VMEM((seq, 2 * LANES), BF16),
                        pltpu.VMEM((seq, LANES), F32),
                        tok_bf, tok_bf, tok_bf, tok_bf, tok_bf,
                        pltpu.VMEM((nc, LANES, LANES), F32),
                        pltpu.VMEM((nc, 8, LANES), F32),
                        pltpu.VMEM((nc, LANES, LANES), BF16)],
        compiler_params=pltpu.CompilerParams(
            dimension_semantics=("arbitrary", "arbitrary"), vmem_limit_bytes=VMEM_LIMIT),
        name="gla",
    )(q, k, v, g, z, wgu, bg, nrm, cm)


def _t5_buckets(rel):
    nb = REL_BUCKETS // 2
    ret = (rel > 0).astype(np.int32) * nb
    n = np.abs(rel)
    max_exact = nb // 2
    large = max_exact + (np.log(np.maximum(n, 1).astype(np.float32) / max_exact)
                         / math.log(REL_MAX_DIST / max_exact) * (nb - max_exact)).astype(np.int32)
    large = np.minimum(large, nb - 1)
    return ret + np.where(n < max_exact, n, large)


def _bucket_map():
    w = SWA_BLOCK
    cq = np.arange(w)[:, None]
    s = np.arange(3 * w)[None, :]
    rel = s - w - cq
    return np.where(np.abs(rel) <= SWA_WINDOW, _t5_buckets(rel), -1).astype(np.int32)


def _bias_kernel(bm_ref, tab_ref, o_ref):
    w = SWA_BLOCK
    pair = pl.program_id(0)
    bm = bm_ref[...]
    key = lax.broadcasted_iota(jnp.int32, bm.shape, 1)
    masked = jnp.full(bm.shape, MASK_VALUE, F32)
    for hh in range(2):
        head = 2 * pair + hh
        acc = masked
        for bkt in range(REL_BUCKETS):
            acc = jnp.where(bm == bkt, tab_ref[bkt, head], acc)
        hcols = slice(hh * 3 * w, (hh + 1) * 3 * w)
        o_ref[0, 0, :, hcols] = jnp.where(key >= w, acc, masked)
        o_ref[1, 0, :, hcols] = acc
        o_ref[2, 0, :, hcols] = jnp.where(key < 2 * w, acc, masked)


def _bias_tables(rel_bias):
    w = SWA_BLOCK
    pairs = SWA_Q_HEADS // 2
    return pl.pallas_call(
        _bias_kernel,
        grid=(pairs,),
        in_specs=[pl.BlockSpec((w, 3 * w), lambda p: (0, 0)),
                  pl.BlockSpec(memory_space=pltpu.SMEM)],
        out_specs=pl.BlockSpec((3, 1, w, 6 * w), lambda p: (0, p, 0, 0)),
        out_shape=jax.ShapeDtypeStruct((3, pairs, w, 6 * w), F32),
        compiler_params=pltpu.CompilerParams(dimension_semantics=("arbitrary",)),
        name="swa_bias",
    )(jnp.asarray(_bucket_map()), rel_bias.astype(F32))


def _swa_kernel(q_ref, k_ref, v_ref, bias_ref, sink_ref, o_ref, ka_s, kb_s, va_s, vb_s,
                lg_s, p_s, st_s, *, seq):
    w = SWA_BLOCK
    nb = seq // w
    kvh = pl.program_id(1)
    pairs = SWA_Q_HEADS // SWA_KV_HEADS // 2
    stage_rows = 512
    lane_s = lax.broadcasted_iota(jnp.int16, (stage_rows, LANES), 1)
    lo = lane_s < HALF
    own = lo == (kvh == 0)
    for s_ref in (ka_s, kb_s, va_s, vb_s):
        zpad = jnp.zeros((w, s_ref.shape[1]), BF16)
        s_ref[0:w, :] = zpad
        s_ref[w + seq:2 * w + seq, :] = zpad

    def stage(i, carry):
        src = pl.ds(pl.multiple_of(i * stage_rows, stage_rows), stage_rows)
        dst = pl.ds(pl.multiple_of(i * stage_rows + w, w), stage_rows)
        k2 = k_ref[src, :]
        v2 = v_ref[src, :]
        kk = jnp.where(own, k2, pltpu.roll(k2, HALF, axis=1))
        vv = jnp.where(own, v2, pltpu.roll(v2, HALF, axis=1))
        zero = jnp.zeros_like(kk)
        one = jnp.ones_like(kk)
        ka_s[dst, :] = jnp.where(lo, kk, zero)
        kb_s[dst, :] = jnp.where(lo, zero, kk)
        va_s[dst, 0:LANES] = jnp.where(lo, vv, zero)
        va_s[dst, LANES:2 * LANES] = jnp.where(lo, one, zero)
        vb_s[dst, 0:LANES] = jnp.where(lo, zero, vv)
        vb_s[dst, LANES:2 * LANES] = jnp.where(lo, zero, one)
        return carry

    lax.fori_loop(0, seq // stage_rows, stage, 0)

    lo_w = lax.broadcasted_iota(jnp.int32, (w, LANES), 1) < HALF

    def logits_stage(n, variant):
        qrows = pl.ds(pl.multiple_of(n * w, w), w)
        band = pl.ds(pl.multiple_of(n * w, w), 3 * w)
        kt = jnp.concatenate([ka_s[band, :], kb_s[band, :]], axis=0)
        for pair in range(pairs):
            cols = slice(pair * LANES, (pair + 1) * LANES)
            lg_s[pair] = _dot_nt(q_ref[qrows, cols], kt) + bias_ref[variant, pair]

    def softmax_stage():
        for pair in range(pairs):
            sink_terms = []
            for hh in range(2):
                hcols = slice(hh * 3 * w, (hh + 1) * 3 * w)
                lg = lg_s[pair, :, hcols]
                sink = sink_ref[kvh * 2 * pairs + 2 * pair + hh]
                m = jnp.maximum(jnp.max(lg, axis=-1, keepdims=True), sink)
                p_s[pair, :, hcols] = jnp.exp(lg - m).astype(BF16)
                sink_terms.append(jnp.exp(sink - m))
            st_s[pair] = jnp.where(lo_w, sink_terms[0], sink_terms[1])

    def output_stage(n):
        qrows = pl.ds(pl.multiple_of(n * w, w), w)
        band = pl.ds(pl.multiple_of(n * w, w), 3 * w)
        vx = jnp.concatenate([va_s[band, :], vb_s[band, :]], axis=0)
        for pair in range(pairs):
            ox = _dot(p_s[pair], vx)
            o = ox[:, :LANES] / (ox[:, LANES:] + st_s[pair])
            o_ref[qrows, pair * LANES:(pair + 1) * LANES] = o.astype(o_ref.dtype)

    logits_stage(0, 0)
    softmax_stage()
    logits_stage(1, 1)

    def steady(n, carry):
        output_stage(n - 1)
        softmax_stage()
        logits_stage(n + 1, 1)
        return carry

    lax.fori_loop(1, nb - 2, steady, 0, unroll=SWA_UNROLL)
    output_stage(nb - 3)
    softmax_stage()
    logits_stage(nb - 1, 2)
    output_stage(nb - 2)
    softmax_stage()
    output_stage(nb - 1)


def _swa(q, k, v, bias, sink, batch, seq):
    t = batch * seq
    w = SWA_BLOCK
    qw = Q_SWA_W // SWA_KV_HEADS
    pairs = SWA_Q_HEADS // SWA_KV_HEADS // 2
    staged_k = pltpu.VMEM((seq + 2 * w, LANES), BF16)
    staged_v = pltpu.VMEM((seq + 2 * w, 2 * LANES), BF16)
    return pl.pallas_call(
        functools.partial(_swa_kernel, seq=seq),
        grid=(batch, SWA_KV_HEADS),
        in_specs=[pl.BlockSpec((seq, qw), lambda b, j: (b, j)),
                  pl.BlockSpec((seq, LANES), lambda b, j: (b, 0)),
                  pl.BlockSpec((seq, LANES), lambda b, j: (b, 0)),
                  pl.BlockSpec((3, pairs, w, 6 * w), lambda b, j: (0, j, 0, 0)),
                  pl.BlockSpec(memory_space=pltpu.SMEM)],
        out_specs=pl.BlockSpec((seq, qw), lambda b, j: (b, j)),
        out_shape=jax.ShapeDtypeStruct((t, Q_SWA_W), BF16),
        scratch_shapes=[staged_k, staged_k, staged_v, staged_v,
                        pltpu.VMEM((pairs, w, 6 * w), F32),
                        pltpu.VMEM((pairs, w, 6 * w), BF16),
                        pltpu.VMEM((pairs, w, LANES), F32)],
        compiler_params=pltpu.CompilerParams(
            dimension_semantics=("arbitrary", "arbitrary"), vmem_limit_bytes=VMEM_LIMIT),
        name="swa",
    )(q, k, v, bias, sink)


def _ffn_kernel(x_ref, oa_ref, os_ref, wout_ref, gpost_ref, gpre_ref, wup_ref, wdn_ref,
                gout_ref, o_ref):
    slab = FFN_TOKEN_BLOCK // FFN_SLABS
    slabs = [slice(s * slab, (s + 1) * slab) for s in range(FFN_SLABS)]
    mixes = [_dot(oa_ref[rows, :], wout_ref[0:V_GLA_W, :])
             + _dot(os_ref[rows, :], wout_ref[V_GLA_W:V_GLA_W + Q_SWA_W, :]) for rows in slabs]
    for rows, mix in zip(slabs, mixes):
        h = x_ref[rows, :] + _rms(mix) * gpost_ref[...]
        hn = (_rms(h) * gpre_ref[...]).astype(BF16)
        acc = jnp.zeros(h.shape, F32)
        for j in range(D_FF // FF_BLOCK):
            cols = slice(j * FF_BLOCK, (j + 1) * FF_BLOCK)
            zj = jnp.maximum(_dot(hn, wup_ref[:, cols]), 0.0)
            acc = acc + _dot((zj * zj).astype(BF16), wdn_ref[cols, :])
        o_ref[rows, :] = h + _rms(acc) * gout_ref[...]


def _ffn(x2, oa, os_, wout, gpost, gpre, wup, wdn, gout):
    t = x2.shape[0]
    row = lambda width: pl.BlockSpec((FFN_TOKEN_BLOCK, width), lambda i: (i, 0))
    return pl.pallas_call(
        _ffn_kernel,
        grid=(t // FFN_TOKEN_BLOCK,),
        in_specs=[row(D_MODEL), row(V_GLA_W), row(Q_SWA_W),
                  _const_spec((V_GLA_W + Q_SWA_W, D_MODEL)),
                  _const_spec((1, D_MODEL)), _const_spec((1, D_MODEL)),
                  _const_spec((D_MODEL, D_FF)), _const_spec((D_FF, D_MODEL)),
                  _const_spec((1, D_MODEL))],
        out_specs=row(D_MODEL),
        out_shape=jax.ShapeDtypeStruct((t, D_MODEL), F32),
        compiler_params=pltpu.CompilerParams(
            dimension_semantics=("arbitrary",), vmem_limit_bytes=VMEM_LIMIT),
        name="outproj_ffn",
    )(x2, oa, os_, wout, gpost, gpre, wup, wdn, gout)


def _layer(h, norm_mix_pre, w_in, w_gu_f, b_g_f, w_gu_b, b_g_b, gla_norm, swa_sink, bias_tab,
           w_out, norm_mix_post, norm_mlp_pre, w_up, w_down, norm_mlp_post):
    batch, seq, _ = h.shape
    x2 = h.reshape(batch * seq, D_MODEL)
    row = lambda g: g.reshape(1, -1).astype(F32)

    later = [w.astype(F32) for w in (w_out, w_up, w_down)]
    qa, ka, va, ga, qs, ks, vs, z, w_out_b, w_up_b, w_down_b = _inproj(
        x2, row(norm_mix_pre), w_in.astype(F32).T, later)

    wf = w_gu_f.reshape(GLA_RANK, GLA_HEADS, GLA_DK).transpose(1, 0, 2)
    wb = w_gu_b.reshape(GLA_RANK, GLA_HEADS, GLA_DK).transpose(1, 0, 2)
    zeros = jnp.zeros_like(wf)
    wgu = jnp.concatenate([jnp.concatenate([wf, zeros], axis=2),
                           jnp.concatenate([zeros, wb], axis=2)], axis=1).astype(BF16)
    bg = jnp.concatenate([b_g_f.reshape(GLA_HEADS, 1, GLA_DK),
                          b_g_b.reshape(GLA_HEADS, 1, GLA_DK)], axis=2).astype(F32)
    o_a = _gla(qa, ka, va, ga, z, wgu, bg, row(gla_norm), batch, seq)

    o_s = _swa(qs, ks, vs, bias_tab, swa_sink.astype(F32), batch, seq)

    out = _ffn(x2, o_a, o_s, w_out_b, row(norm_mix_post), row(norm_mlp_pre), w_up_b, w_down_b,
               row(norm_mlp_post))
    return out.reshape(batch, seq, D_MODEL)


def kernel(x, norm_mix_pre, w_in, w_gate_up_fwd, b_gate_fwd, w_gate_up_bwd, b_gate_bwd, gla_norm,
           swa_sink, rel_bias, w_out, norm_mix_post, norm_mlp_pre, w_up, w_down, norm_mlp_post):
    bias_tab = _bias_tables(rel_bias)
    h = x
    for l in range(w_in.shape[0]):
        h = _layer(h, norm_mix_pre[l], w_in[l], w_gate_up_fwd[l], b_gate_fwd[l], w_gate_up_bwd[l],
                   b_gate_bwd[l], gla_norm[l], swa_sink[l], bias_tab, w_out[l], norm_mix_post[l],
                   norm_mlp_pre[l], w_up[l], w_down[l], norm_mlp_post[l])
    return h
```

```python
import functools
import math

import jax
import jax.numpy as jnp
import numpy as np
from jax import lax
from jax.experimental import pallas as pl
from jax.experimental.pallas import tpu as pltpu

F32 = jnp.float32
BF16 = jnp.bfloat16

D_MODEL = 1024
GLA_HEADS = 4
GLA_DK = 64
GLA_DV = 128
GLA_RANK = 16
GLA_GATE_NORM = 16.0
GLA_CHUNK = 64
SWA_DH = 64
SWA_Q_HEADS = 8
SWA_KV_HEADS = 2
SWA_BLOCK = 128
SWA_WINDOW = 128
REL_BUCKETS = 32
REL_MAX_DIST = 128
D_FF = 4 * D_MODEL
NORM_EPS = 1e-6
MASK_VALUE = -1e30

LANES = 128
HALF = LANES // 2

QK_GLA_W = GLA_HEADS * GLA_DK
V_GLA_W = GLA_HEADS * GLA_DV
G_GLA_W = GLA_HEADS * GLA_DV
Q_SWA_W = SWA_Q_HEADS * SWA_DH
KV_SWA_W = SWA_KV_HEADS * SWA_DH
Z_W = 2 * GLA_RANK
IN_WIDTHS = (QK_GLA_W, QK_GLA_W, V_GLA_W, G_GLA_W, Q_SWA_W, KV_SWA_W, KV_SWA_W, Z_W)
IN_COLS = sum(IN_WIDTHS)
IN_DOT_GROUPS = ((0, 1), (2,), (3,), (4,), (5, 6), (7,))
N_LATER_WEIGHTS = 3

GLA_UNROLL = 64
SWA_UNROLL = 29
GLA_GROUP = 256
GLA_PREP_UNROLL = 12
GLA_OUT_UNROLL = 15
TOKEN_BLOCK = 1024
FFN_TOKEN_BLOCK = 1024
FF_BLOCK = 1024
FFN_SLABS = 4
VMEM_LIMIT = 56 * 1024 * 1024


def _rms(x):
    return x * lax.rsqrt(jnp.mean(x * x, axis=-1, keepdims=True) + NORM_EPS)


def _dot(a, b):
    return jnp.dot(a, b, preferred_element_type=F32)


def _dot_nt(a, b):
    return lax.dot_general(a, b, (((1,), (1,)), ((), ())), preferred_element_type=F32)


def _dot_tn(a, b):
    return lax.dot_general(a, b, (((0,), (0,)), ((), ())), preferred_element_type=F32)


def _const_spec(shape):
    nd = len(shape)
    return pl.BlockSpec(shape, lambda *_: (0,) * nd, pipeline_mode=pl.Buffered(1))


def _rows(item, size):
    start = item * size
    if not isinstance(start, int):
        start = pl.multiple_of(start, size)
    return pl.ds(start, size)


def _software_pipeline(n_items, stages, unroll=1):
    depth = len(stages)

    def trip(t, static):
        for k in reversed(range(depth)):
            if not static or 0 <= t - k < n_items:
                stages[k](t - k)

    for t in range(depth - 1):
        trip(t, True)

    def steady(t, carry):
        trip(t, False)
        return carry

    lax.fori_loop(depth - 1, n_items, steady, 0, unroll=unroll)
    for t in range(n_items, n_items + depth - 1):
        trip(t, True)


def _stage_w_in(wt_ref, w_s):
    z_src = 2 * QK_GLA_W + V_GLA_W + G_GLA_W
    tail = z_src + Z_W
    step = 128

    def copy_rows(src, dst, n_rows, scale):
        def body(i, carry):
            w = wt_ref[pl.ds(src + i * step, step), :]
            w_s[pl.ds(dst + i * step, step), :] = (w if scale == 1.0 else w * scale).astype(BF16)
            return carry
        lax.fori_loop(0, n_rows // step, body, 0)

    copy_rows(0, 0, QK_GLA_W, GLA_DK ** -0.5)
    copy_rows(QK_GLA_W, QK_GLA_W, z_src - QK_GLA_W, 1.0)
    copy_rows(tail, z_src, Q_SWA_W, SWA_DH ** -0.5)
    copy_rows(tail + Q_SWA_W, z_src + Q_SWA_W, 2 * KV_SWA_W, 1.0)
    w_s[IN_COLS - Z_W:IN_COLS, :] = wt_ref[z_src:tail, :].astype(BF16)


def _inproj_kernel(x_ref, g_ref, w_ref, *refs):
    n_proj = len(IN_WIDTHS)
    later_f32, refs = refs[:N_LATER_WEIGHTS], refs[N_LATER_WEIGHTS:]
    out_refs, later_bf16, w_s = refs[:n_proj], refs[n_proj:-1], refs[-1]

    @pl.when(pl.program_id(0) == 0)
    def _():
        _stage_w_in(w_ref, w_s)

    for src, dst in zip(later_f32, later_bf16):
        dst[...] = src[...].astype(BF16)

    u = (_rms(x_ref[...]) * g_ref[...]).astype(BF16)
    off = 0
    for group in IN_DOT_GROUPS:
        width = sum(IN_WIDTHS[i] for i in group)
        y = _dot_nt(u, w_s[off:off + width, :])
        off += width
        col = 0
        for i in group:
            out_refs[i][...] = y[:, col:col + IN_WIDTHS[i]].astype(out_refs[i].dtype)
            col += IN_WIDTHS[i]


def _inproj(x2, g, w, later_weights):
    t = x2.shape[0]
    steps = t // TOKEN_BLOCK
    row = lambda width: pl.BlockSpec((TOKEN_BLOCK, width), lambda i: (i, 0))
    widths = IN_WIDTHS
    dtypes = (BF16,) * (len(IN_WIDTHS) - 1) + (F32,)
    assert len(later_weights) == N_LATER_WEIGHTS
    slices = [pl.BlockSpec((lw.shape[0] // steps, lw.shape[1]), lambda i: (i, 0))
              for lw in later_weights]
    return pl.pallas_call(
        _inproj_kernel,
        grid=(steps,),
        in_specs=[row(D_MODEL), _const_spec((1, D_MODEL)), _const_spec((IN_COLS, D_MODEL))] + slices,
        out_specs=[row(wd) for wd in widths] + slices,
        out_shape=([jax.ShapeDtypeStruct((t, wd), dt) for wd, dt in zip(widths, dtypes)]
                   + [jax.ShapeDtypeStruct(lw.shape, BF16) for lw in later_weights]),
        scratch_shapes=[pltpu.VMEM((IN_COLS, D_MODEL), BF16)],
        compiler_params=pltpu.CompilerParams(
            dimension_semantics=("arbitrary",), vmem_limit_bytes=VMEM_LIMIT),
        name="inproj",
    )(x2, g, w, *later_weights)


def _cumsum_matrix():
    r = np.arange(GLA_GROUP)[:, None]
    s = np.arange(GLA_GROUP)[None, :]
    return (((r // GLA_CHUNK) == (s // GLA_CHUNK)) & (s <= r)).astype(np.float32)


def _gla_kernel(q_ref, k_ref, v_ref, g_ref, z_ref, wgu_ref, bg_ref, nrm_ref, cm_ref, o_ref,
                la_s, hl_s, cs_s, qdec_s, kina_s, kinb_s, kst_s, am_s, ds_s, dec_s, sc_s, *, seq):
    c = GLA_CHUNK
    nc = seq // c
    grp = GLA_GROUP
    cpg = grp // c
    n_groups = seq // grp
    wgu = wgu_ref[0]
    bg = bg_ref[0]
    fwd_p = lax.broadcasted_iota(jnp.int16, (grp, LANES), 1) < HALF
    own_p = fwd_p == (pl.program_id(1) % 2 == 0)
    zero_p = jnp.zeros((grp, LANES), BF16)

    def own_head_twice(tile):
        return jnp.where(own_p, tile, pltpu.roll(tile, HALF, axis=1)).astype(F32)

    fwd_3 = lax.broadcasted_iota(jnp.int32, (cpg, c, LANES), 2) < HALF
    lane_c = lax.broadcasted_iota(jnp.int32, (c, LANES), 1)
    row_c = lax.broadcasted_iota(jnp.int32, (c, LANES), 0)
    tri = ((lane_c < HALF) & (lane_c <= row_c)) | ((lane_c >= HALF) & ((lane_c - HALF) >= row_c))
    zero_c = jnp.zeros((c, LANES), F32)

    def gate_preact(i):
        rows = _rows(i, grp)
        la_s[rows, :] = _dot(z_ref[rows, :].astype(BF16), wgu)

    def decay_logs(i):
        rows = _rows(i, grp)
        pre = la_s[rows, :] + bg
        la = ((jnp.minimum(pre, 0.0) - jnp.log(1.0 + jnp.exp(-jnp.abs(pre))))
              * (1.0 / GLA_GATE_NORM))
        la_hi = la.astype(BF16)
        la_s[rows, :] = la
        hl_s[rows, 0:LANES] = la_hi
        hl_s[rows, LANES:2 * LANES] = (la - la_hi.astype(F32)).astype(BF16)

    def chunk_cumsum(i):
        rows = _rows(i, grp)
        cs = _dot(cm_ref[...], hl_s[rows, :])
        cs_s[rows, :] = cs[:, :LANES] + cs[:, LANES:]

    def decayed_qk(i):
        rows = _rows(i, grp)
        la3 = la_s[rows, :].reshape(cpg, c, LANES)
        cum3 = cs_s[rows, :].reshape(cpg, c, LANES)
        tot = jnp.sum(la3, axis=1, keepdims=True)
        b3 = jnp.where(fwd_3, cum3, tot - cum3 + la3)
        b = b3.reshape(grp, LANES)
        rest = jnp.exp(tot - b3).reshape(grp, LANES)
        qq = own_head_twice(q_ref[rows, :])
        kk = own_head_twice(k_ref[rows, :])
        kin = (kk * jnp.exp(-b)).astype(BF16)
        qdec_s[rows, :] = (qq * jnp.exp(b)).astype(BF16)
        kina_s[rows, :] = jnp.where(fwd_p, kin, zero_p)
        kinb_s[rows, :] = jnp.where(fwd_p, zero_p, kin)
        kst_s[rows, :] = (kk * rest).astype(BF16)
        dec_s[_rows(i, cpg)] = jnp.broadcast_to(jnp.exp(tot), (cpg, 8, LANES))

    def chunk_products(i):
        for cc in range(cpg):
            n = i * cpg + cc
            rows = _rows(n, c)
            kin_bd = jnp.concatenate([kina_s[rows, :], kinb_s[rows, :]], axis=0)
            a2 = _dot_nt(qdec_s[rows, :], kin_bd)
            am_s[rows, :] = jnp.where(tri, a2, zero_c).astype(BF16)
            ds_s[n] = _dot_tn(v_ref[rows, :], kst_s[rows, :])

    _software_pipeline(n_groups, [gate_preact, decay_logs, chunk_cumsum, decayed_qk,
                                  chunk_products], unroll=GLA_PREP_UNROLL)

    fwd_sq = lax.broadcasted_iota(jnp.int32, (LANES, LANES), 1) < HALF
    fwd_row = lax.broadcasted_iota(jnp.int32, (8, LANES), 1) < HALF

    def scan_step(i, s):
        j = nc - 1 - i
        sb = s.astype(BF16)
        sc_s[i, :, 0:HALF] = sb[:, 0:HALF]
        sc_s[j, :, HALF:LANES] = sb[:, HALF:LANES]
        dec = jnp.where(fwd_row, dec_s[i], dec_s[j])[0:1]
        return dec * s + jnp.where(fwd_sq, ds_s[i], ds_s[j])

    lax.fori_loop(0, nc, scan_step, jnp.zeros((LANES, LANES), F32), unroll=GLA_UNROLL)

    nrm = nrm_ref[...]

    def mix(i):
        for cc in range(cpg):
            n = i * cpg + cc
            rows = _rows(n, c)
            vc = v_ref[rows, :]
            vvt = jnp.concatenate([vc, vc], axis=0).T
            la_s[rows, :] = _dot_nt(jnp.concatenate([qdec_s[rows, :], am_s[rows, :]], axis=1),
                                    jnp.concatenate([sc_s[n], vvt], axis=1))

    def norm_gate(i):
        rows = _rows(i, grp)
        gate = g_ref[rows, :].astype(F32)
        o = (_rms(la_s[rows, :]) * nrm) * (gate / (1.0 + jnp.exp(-gate)))
        o_ref[rows, :] = o.astype(o_ref.dtype)

    _software_pipeline(n_groups, [mix, norm_gate], unroll=GLA_OUT_UNROLL)


def _gla(q, k, v, g, z, wgu, bg, nrm, batch, seq):
    t = batch * seq
    nc = seq // GLA_CHUNK
    tile = pl.BlockSpec((seq, LANES), lambda b, h: (b, h))
    pair_tile = pl.BlockSpec((seq, LANES), lambda b, h: (b, h // 2))
    tok_bf = pltpu.VMEM((seq, LANES), BF16)
    cm = jnp.asarray(_cumsum_matrix(), dtype=BF16)
    return pl.pallas_call(
        functools.partial(_gla_kernel, seq=seq),
        grid=(batch, GLA_HEADS),
        in_specs=[pair_tile, pair_tile, tile, tile,
                  pl.BlockSpec((seq, Z_W), lambda b, h: (b, 0)),
                  pl.BlockSpec((1, Z_W, LANES), lambda b, h: (h, 0, 0)),
                  pl.BlockSpec((1, 1, LANES), lambda b, h: (h, 0, 0)),
                  pl.BlockSpec((1, LANES), lambda b, h: (0, 0)),
                  _const_spec(cm.shape)],
        out_specs=tile,
        out_shape=jax.ShapeDtypeStruct((t, V_GLA_W), BF16),
        scratch_shapes=[pltpu.VMEM((seq, LANES), F32),
                        pltpu.VMEM((seq, 2 * LANES), BF16),
                        pltpu.VMEM((seq, LANES), F32),
                        tok_bf, tok_bf, tok_bf, tok_bf, tok_bf,
                        pltpu.VMEM((nc, LANES, LANES), F32),
                        pltpu.VMEM((nc, 8, LANES), F32),
                        pltpu.VMEM((nc, LANES, LANES), BF16)],
        compiler_params=pltpu.CompilerParams(
            dimension_semantics=("arbitrary", "arbitrary"), vmem_limit_bytes=VMEM_LIMIT),
        name="gla",
    )(q, k, v, g, z, wgu, bg, nrm, cm)


def _t5_buckets(rel):
    nb = REL_BUCKETS // 2
    ret = (rel > 0).astype(np.int32) * nb
    n = np.abs(rel)
    max_exact = nb // 2
    large = max_exact + (np.log(np.maximum(n, 1).astype(np.float32) / max_exact)
                         / math.log(REL_MAX_DIST / max_exact) * (nb - max_exact)).astype(np.int32)
    large = np.minimum(large, nb - 1)
    return ret + np.where(n < max_exact, n, large)


def _bucket_map():
    w = SWA_BLOCK
    cq = np.arange(w)[:, None]
    s = np.arange(3 * w)[None, :]
    rel = s - w - cq
    return np.where(np.abs(rel) <= SWA_WINDOW, _t5_buckets(rel), -1).astype(np.int32)


def _bias_kernel(bm_ref, tab_ref, o_ref):
    w = SWA_BLOCK
    pair = pl.program_id(0)
    bm = bm_ref[...]
    key = lax.broadcasted_iota(jnp.int32, bm.shape, 1)
    masked = jnp.full(bm.shape, MASK_VALUE, F32)
    for hh in range(2):
        head = 2 * pair + hh
        acc = masked
        for bkt in range(REL_BUCKETS):
            acc = jnp.where(bm == bkt, tab_ref[bkt, head], acc)
        hcols = slice(hh * 3 * w, (hh + 1) * 3 * w)
        o_ref[0, 0, :, hcols] = jnp.where(key >= w, acc, masked)
        o_ref[1, 0, :, hcols] = acc
        o_ref[2, 0, :, hcols] = jnp.where(key < 2 * w, acc, masked)


def _bias_tables(rel_bias):
    w = SWA_BLOCK
    pairs = SWA_Q_HEADS // 2
    return pl.pallas_call(
        _bias_kernel,
        grid=(pairs,),
        in_specs=[pl.BlockSpec((w, 3 * w), lambda p: (0, 0)),
                  pl.BlockSpec(memory_space=pltpu.SMEM)],
        out_specs=pl.BlockSpec((3, 1, w, 6 * w), lambda p: (0, p, 0, 0)),
        out_shape=jax.ShapeDtypeStruct((3, pairs, w, 6 * w), F32),
        compiler_params=pltpu.CompilerParams(dimension_semantics=("arbitrary",)),
        name="swa_bias",
    )(jnp.asarray(_bucket_map()), rel_bias.astype(F32))


def _swa_kernel(q_ref, k_ref, v_ref, bias_ref, sink_ref, o_ref, ka_s, kb_s, va_s, vb_s,
                lg_s, p_s, st_s, *, seq):
    w = SWA_BLOCK
    nb = seq // w
    kvh = pl.program_id(1)
    pairs = SWA_Q_HEADS // SWA_KV_HEADS // 2
    stage_rows = 512
    lane_s = lax.broadcasted_iota(jnp.int16, (stage_rows, LANES), 1)
    lo = lane_s < HALF
    own = lo == (kvh == 0)
    for s_ref in (ka_s, kb_s, va_s, vb_s):
        zpad = jnp.zeros((w, s_ref.shape[1]), BF16)
        s_ref[0:w, :] = zpad
        s_ref[w + seq:2 * w + seq, :] = zpad

    def stage(i, carry):
        src = pl.ds(pl.multiple_of(i * stage_rows, stage_rows), stage_rows)
        dst = pl.ds(pl.multiple_of(i * stage_rows + w, w), stage_rows)
        k2 = k_ref[src, :]
        v2 = v_ref[src, :]
        kk = jnp.where(own, k2, pltpu.roll(k2, HALF, axis=1))
        vv = jnp.where(own, v2, pltpu.roll(v2, HALF, axis=1))
        zero = jnp.zeros_like(kk)
        one = jnp.ones_like(kk)
        ka_s[dst, :] = jnp.where(lo, kk, zero)
        kb_s[dst, :] = jnp.where(lo, zero, kk)
        va_s[dst, 0:LANES] = jnp.where(lo, vv, zero)
        va_s[dst, LANES:2 * LANES] = jnp.where(lo, one, zero)
        vb_s[dst, 0:LANES] = jnp.where(lo, zero, vv)
        vb_s[dst, LANES:2 * LANES] = jnp.where(lo, zero, one)
        return carry

    lax.fori_loop(0, seq // stage_rows, stage, 0, unroll=True)

    lo_w = lax.broadcasted_iota(jnp.int32, (w, LANES), 1) < HALF

    def logits_stage(n, variant):
        qrows = pl.ds(pl.multiple_of(n * w, w), w)
        band = pl.ds(pl.multiple_of(n * w, w), 3 * w)
        kt = jnp.concatenate([ka_s[band, :], kb_s[band, :]], axis=0)
        for pair in range(pairs):
            cols = slice(pair * LANES, (pair + 1) * LANES)
            lg_s[pair] = _dot_nt(q_ref[qrows, cols], kt) + bias_ref[variant, pair]

    def softmax_stage():
        for pair in range(pairs):
            sink_terms = []
            for hh in range(2):
                hcols = slice(hh * 3 * w, (hh + 1) * 3 * w)
                lg = lg_s[pair, :, hcols]
                sink = sink_ref[kvh * 2 * pairs + 2 * pair + hh]
                m = jnp.maximum(jnp.max(lg, axis=-1, keepdims=True), sink)
                p_s[pair, :, hcols] = jnp.exp(lg - m).astype(BF16)
                sink_terms.append(jnp.exp(sink - m))
            st_s[pair] = jnp.where(lo_w, sink_terms[0], sink_terms[1])

    def output_stage(n):
        qrows = pl.ds(pl.multiple_of(n * w, w), w)
        band = pl.ds(pl.multiple_of(n * w, w), 3 * w)
        vx = jnp.concatenate([va_s[band, :], vb_s[band, :]], axis=0)
        for pair in range(pairs):
            ox = _dot(p_s[pair], vx)
            o = ox[:, :LANES] / (ox[:, LANES:] + st_s[pair])
            o_ref[qrows, pair * LANES:(pair + 1) * LANES] = o.astype(o_ref.dtype)

    logits_stage(0, 0)
    softmax_stage()
    logits_stage(1, 1)

    def steady(n, carry):
        output_stage(n - 1)
        softmax_stage()
        logits_stage(n + 1, 1)
        return carry

    lax.fori_loop(1, nb - 2, steady, 0, unroll=SWA_UNROLL)
    output_stage(nb - 3)
    softmax_stage()
    logits_stage(nb - 1, 2)
    output_stage(nb - 2)
    softmax_stage()
    output_stage(nb - 1)


def _swa(q, k, v, bias, sink, batch, seq):
    t = batch * seq
    w = SWA_BLOCK
    qw = Q_SWA_W // SWA_KV_HEADS
    pairs = SWA_Q_HEADS // SWA_KV_HEADS // 2
    staged_k = pltpu.VMEM((seq + 2 * w, LANES), BF16)
    staged_v = pltpu.VMEM((seq + 2 * w, 2 * LANES), BF16)
    return pl.pallas_call(
        functools.partial(_swa_kernel, seq=seq),
        grid=(batch, SWA_KV_HEADS),
        in_specs=[pl.BlockSpec((seq, qw), lambda b, j: (b, j)),
                  pl.BlockSpec((seq, LANES), lambda b, j: (b, 0)),
                  pl.BlockSpec((seq, LANES), lambda b, j: (b, 0)),
                  pl.BlockSpec((3, pairs, w, 6 * w), lambda b, j: (0, j, 0, 0)),
                  pl.BlockSpec(memory_space=pltpu.SMEM)],
        out_specs=pl.BlockSpec((seq, qw), lambda b, j: (b, j)),
        out_shape=jax.ShapeDtypeStruct((t, Q_SWA_W), BF16),
        scratch_shapes=[staged_k, staged_k, staged_v, staged_v,
                        pltpu.VMEM((pairs, w, 6 * w), F32),
                        pltpu.VMEM((pairs, w, 6 * w), BF16),
                        pltpu.VMEM((pairs, w, LANES), F32)],
        compiler_params=pltpu.CompilerParams(
            dimension_semantics=("arbitrary", "arbitrary"), vmem_limit_bytes=VMEM_LIMIT),
        name="swa",
    )(q, k, v, bias, sink)


def _ffn_kernel(x_ref, oa_ref, os_ref, wout_ref, gpost_ref, gpre_ref, wup_ref, wdn_ref,
                gout_ref, o_ref):
    slab = FFN_TOKEN_BLOCK // FFN_SLABS
    slabs = [slice(s * slab, (s + 1) * slab) for s in range(FFN_SLABS)]
    mixes = [_dot(oa_ref[rows, :], wout_ref[0:V_GLA_W, :])
             + _dot(os_ref[rows, :], wout_ref[V_GLA_W:V_GLA_W + Q_SWA_W, :]) for rows in slabs]
    for rows, mix in zip(slabs, mixes):
        h = x_ref[rows, :] + _rms(mix) * gpost_ref[...]
        hn = (_rms(h) * gpre_ref[...]).astype(BF16)
        acc = jnp.zeros(h.shape, F32)
        for j in range(D_FF // FF_BLOCK):
            cols = slice(j * FF_BLOCK, (j + 1) * FF_BLOCK)
            zj = jnp.maximum(_dot(hn, wup_ref[:, cols]), 0.0)
            acc = acc + _dot((zj * zj).astype(BF16), wdn_ref[cols, :])
        o_ref[rows, :] = h + _rms(acc) * gout_ref[...]


def _ffn(x2, oa, os_, wout, gpost, gpre, wup, wdn, gout):
    t = x2.shape[0]
    row = lambda width: pl.BlockSpec((FFN_TOKEN_BLOCK, width), lambda i: (i, 0))
    return pl.pallas_call(
        _ffn_kernel,
        grid=(t // FFN_TOKEN_BLOCK,),
        in_specs=[row(D_MODEL), row(V_GLA_W), row(Q_SWA_W),
                  _const_spec((V_GLA_W + Q_SWA_W, D_MODEL)),
                  _const_spec((1, D_MODEL)), _const_spec((1, D_MODEL)),
                  _const_spec((D_MODEL, D_FF)), _const_spec((D_FF, D_MODEL)),
                  _const_spec((1, D_MODEL))],
        out_specs=row(D_MODEL),
        out_shape=jax.ShapeDtypeStruct((t, D_MODEL), F32),
        compiler_params=pltpu.CompilerParams(
            dimension_semantics=("arbitrary",), vmem_limit_bytes=VMEM_LIMIT),
        name="outproj_ffn",
    )(x2, oa, os_, wout, gpost, gpre, wup, wdn, gout)


def _layer(h, norm_mix_pre, w_in, w_gu_f, b_g_f, w_gu_b, b_g_b, gla_norm, swa_sink, bias_tab,
           w_out, norm_mix_post, norm_mlp_pre, w_up, w_down, norm_mlp_post):
    batch, seq, _ = h.shape
    x2 = h.reshape(batch * seq, D_MODEL)
    row = lambda g: g.reshape(1, -1).astype(F32)

    later = [w.astype(F32) for w in (w_out, w_up, w_down)]
    qa, ka, va, ga, qs, ks, vs, z, w_out_b, w_up_b, w_down_b = _inproj(
        x2, row(norm_mix_pre), w_in.astype(F32).T, later)

    wf = w_gu_f.reshape(GLA_RANK, GLA_HEADS, GLA_DK).transpose(1, 0, 2)
    wb = w_gu_b.reshape(GLA_RANK, GLA_HEADS, GLA_DK).transpose(1, 0, 2)
    zeros = jnp.zeros_like(wf)
    wgu = jnp.concatenate([jnp.concatenate([wf, zeros], axis=2),
                           jnp.concatenate([zeros, wb], axis=2)], axis=1).astype(BF16)
    bg = jnp.concatenate([b_g_f.reshape(GLA_HEADS, 1, GLA_DK),
                          b_g_b.reshape(GLA_HEADS, 1, GLA_DK)], axis=2).astype(F32)
    o_a = _gla(qa, ka, va, ga, z, wgu, bg, row(gla_norm), batch, seq)

    o_s = _swa(qs, ks, vs, bias_tab, swa_sink.astype(F32), batch, seq)

    out = _ffn(x2, o_a, o_s, w_out_b, row(norm_mix_post), row(norm_mlp_pre), w_up_b, w_down_b,
               row(norm_mlp_post))
    return out.reshape(batch, seq, D_MODEL)


def kernel(x, norm_mix_pre, w_in, w_gate_up_fwd, b_gate_fwd, w_gate_up_bwd, b_gate_bwd, gla_norm,
           swa_sink, rel_bias, w_out, norm_mix_post, norm_mlp_pre, w_up, w_down, norm_mlp_post):
    bias_tab = _bias_tables(rel_bias)
    h = x
    for l in range(w_in.shape[0]):
        h = _layer(h, norm_mix_pre[l], w_in[l], w_gate_up_fwd[l], b_gate_fwd[l], w_gate_up_bwd[l],
                   b_gate_bwd[l], gla_norm[l], swa_sink[l], bias_tab, w_out[l], norm_mix_post[l],
                   norm_mlp_pre[l], w_up[l], w_down[l], norm_mlp_post[l])
    return h
```

```python
import functools
import math

import jax
import jax.numpy as jnp
import numpy as np
from jax import lax
from jax.experimental import pallas as pl
from jax.experimental.pallas import tpu as pltpu

F32 = jnp.float32
BF16 = jnp.bfloat16

D_MODEL = 1024
GLA_HEADS = 4
GLA_DK = 64
GLA_DV = 128
GLA_RANK = 16
GLA_GATE_NORM = 16.0
GLA_CHUNK = 64
SWA_DH = 64
SWA_Q_HEADS = 8
SWA_KV_HEADS = 2
SWA_BLOCK = 128
SWA_WINDOW = 128
REL_BUCKETS = 32
REL_MAX_DIST = 128
D_FF = 4 * D_MODEL
NORM_EPS = 1e-6
MASK_VALUE = -1e30

LANES = 128
HALF = LANES // 2

QK_GLA_W = GLA_HEADS * GLA_DK
V_GLA_W = GLA_HEADS * GLA_DV
G_GLA_W = GLA_HEADS * GLA_DV
Q_SWA_W = SWA_Q_HEADS * SWA_DH
KV_SWA_W = SWA_KV_HEADS * SWA_DH
Z_W = 2 * GLA_RANK
IN_WIDTHS = (QK_GLA_W, QK_GLA_W, V_GLA_W, G_GLA_W, Q_SWA_W, KV_SWA_W, KV_SWA_W, Z_W)
IN_COLS = sum(IN_WIDTHS)
IN_DOT_GROUPS = ((0, 1), (2,), (3,), (4,), (5, 6), (7,))
N_LATER_WEIGHTS = 3

GLA_UNROLL = 64
SWA_UNROLL = 29
GLA_GROUP = 256
GLA_PREP_UNROLL = 12
GLA_OUT_UNROLL = 15
TOKEN_BLOCK = 1024
FFN_TOKEN_BLOCK = 1024
FF_BLOCK = 1024
FFN_SLABS = 4
VMEM_LIMIT = 56 * 1024 * 1024


def _rms(x):
    return x * lax.rsqrt(jnp.mean(x * x, axis=-1, keepdims=True) + NORM_EPS)


def _dot(a, b):
    return jnp.dot(a, b, preferred_element_type=F32)


def _dot_nt(a, b):
    return lax.dot_general(a, b, (((1,), (1,)), ((), ())), preferred_element_type=F32)


def _dot_tn(a, b):
    return lax.dot_general(a, b, (((0,), (0,)), ((), ())), preferred_element_type=F32)


def _const_spec(shape):
    nd = len(shape)
    return pl.BlockSpec(shape, lambda *_: (0,) * nd, pipeline_mode=pl.Buffered(1))


def _rows(item, size):
    start = item * size
    if not isinstance(start, int):
        start = pl.multiple_of(start, size)
    return pl.ds(start, size)


def _software_pipeline(n_items, stages, unroll=1):
    depth = len(stages)

    def trip(t, static):
        for k in reversed(range(depth)):
            if not static or 0 <= t - k < n_items:
                stages[k](t - k)

    for t in range(depth - 1):
        trip(t, True)

    def steady(t, carry):
        trip(t, False)
        return carry

    lax.fori_loop(depth - 1, n_items, steady, 0, unroll=unroll)
    for t in range(n_items, n_items + depth - 1):
        trip(t, True)


def _stage_w_in(wt_ref, w_s):
    z_src = 2 * QK_GLA_W + V_GLA_W + G_GLA_W
    tail = z_src + Z_W
    step = 128

    def copy_rows(src, dst, n_rows, scale):
        def body(i, carry):
            w = wt_ref[pl.ds(src + i * step, step), :]
            w_s[pl.ds(dst + i * step, step), :] = (w if scale == 1.0 else w * scale).astype(BF16)
            return carry
        lax.fori_loop(0, n_rows // step, body, 0)

    copy_rows(0, 0, QK_GLA_W, GLA_DK ** -0.5)
    copy_rows(QK_GLA_W, QK_GLA_W, z_src - QK_GLA_W, 1.0)
    copy_rows(tail, z_src, Q_SWA_W, SWA_DH ** -0.5)
    copy_rows(tail + Q_SWA_W, z_src + Q_SWA_W, 2 * KV_SWA_W, 1.0)
    w_s[IN_COLS - Z_W:IN_COLS, :] = wt_ref[z_src:tail, :].astype(BF16)


def _inproj_kernel(x_ref, g_ref, w_ref, *refs):
    n_proj = len(IN_WIDTHS)
    later_f32, refs = refs[:N_LATER_WEIGHTS], refs[N_LATER_WEIGHTS:]
    out_refs, later_bf16, w_s = refs[:n_proj], refs[n_proj:-1], refs[-1]

    @pl.when(pl.program_id(0) == 0)
    def _():
        _stage_w_in(w_ref, w_s)

    for src, dst in zip(later_f32, later_bf16):
        dst[...] = src[...].astype(BF16)

    u = (_rms(x_ref[...]) * g_ref[...]).astype(BF16)
    off = 0
    for group in IN_DOT_GROUPS:
        width = sum(IN_WIDTHS[i] for i in group)
        y = _dot_nt(u, w_s[off:off + width, :])
        off += width
        col = 0
        for i in group:
            out_refs[i][...] = y[:, col:col + IN_WIDTHS[i]].astype(out_refs[i].dtype)
            col += IN_WIDTHS[i]


def _inproj(x2, g, w, later_weights):
    t = x2.shape[0]
    steps = t // TOKEN_BLOCK
    row = lambda width: pl.BlockSpec((TOKEN_BLOCK, width), lambda i: (i, 0))
    widths = IN_WIDTHS
    dtypes = (BF16,) * (len(IN_WIDTHS) - 1) + (F32,)
    assert len(later_weights) == N_LATER_WEIGHTS
    slices = [pl.BlockSpec((lw.shape[0] // steps, lw.shape[1]), lambda i: (i, 0))
              for lw in later_weights]
    return pl.pallas_call(
        _inproj_kernel,
        grid=(steps,),
        in_specs=[row(D_MODEL), _const_spec((1, D_MODEL)), _const_spec((IN_COLS, D_MODEL))] + slices,
        out_specs=[row(wd) for wd in widths] + slices,
        out_shape=([jax.ShapeDtypeStruct((t, wd), dt) for wd, dt in zip(widths, dtypes)]
                   + [jax.ShapeDtypeStruct(lw.shape, BF16) for lw in later_weights]),
        scratch_shapes=[pltpu.VMEM((IN_COLS, D_MODEL), BF16)],
        compiler_params=pltpu.CompilerParams(
            dimension_semantics=("arbitrary",), vmem_limit_bytes=VMEM_LIMIT),
        name="inproj",
    )(x2, g, w, *later_weights)


def _cumsum_matrix():
    r = np.arange(GLA_GROUP)[:, None]
    s = np.arange(GLA_GROUP)[None, :]
    return (((r // GLA_CHUNK) == (s // GLA_CHUNK)) & (s <= r)).astype(np.float32)


def _gla_kernel(q_ref, k_ref, v_ref, g_ref, z_ref, wgu_ref, bg_ref, nrm_ref, cm_ref, o_ref,
                la_s, hl_s, cs_s, qdec_s, kina_s, kinb_s, kst_s, am_s, ds_s, dec_s, sc_s, *, seq):
    c = GLA_CHUNK
    nc = seq // c
    grp = GLA_GROUP
    cpg = grp // c
    n_groups = seq // grp
    wgu = wgu_ref[0]
    bg = bg_ref[0]
    fwd_p = lax.broadcasted_iota(jnp.int16, (grp, LANES), 1) < HALF
    own_p = fwd_p == (pl.program_id(1) % 2 == 0)
    zero_p = jnp.zeros((grp, LANES), BF16)

    def own_head_twice(tile):
        return jnp.where(own_p, tile, pltpu.roll(tile, HALF, axis=1)).astype(F32)

    fwd_3 = lax.broadcasted_iota(jnp.int32, (cpg, c, LANES), 2) < HALF
    lane_c = lax.broadcasted_iota(jnp.int32, (c, LANES), 1)
    row_c = lax.broadcasted_iota(jnp.int32, (c, LANES), 0)
    tri = ((lane_c < HALF) & (lane_c <= row_c)) | ((lane_c >= HALF) & ((lane_c - HALF) >= row_c))
    zero_c = jnp.zeros((c, LANES), F32)

    def gate_preact(i):
        rows = _rows(i, grp)
        la_s[rows, :] = _dot(z_ref[rows, :].astype(BF16), wgu)

    def decay_logs(i):
        rows = _rows(i, grp)
        pre = la_s[rows, :] + bg
        la = ((jnp.minimum(pre, 0.0) - jnp.log(1.0 + jnp.exp(-jnp.abs(pre))))
              * (1.0 / GLA_GATE_NORM))
        la_hi = la.astype(BF16)
        la_s[rows, :] = la
        hl_s[rows, 0:LANES] = la_hi
        hl_s[rows, LANES:2 * LANES] = (la - la_hi.astype(F32)).astype(BF16)

    def chunk_cumsum(i):
        rows = _rows(i, grp)
        cs = _dot(cm_ref[...], hl_s[rows, :])
        cs_s[rows, :] = cs[:, :LANES] + cs[:, LANES:]

    def decayed_qk(i):
        rows = _rows(i, grp)
        la3 = la_s[rows, :].reshape(cpg, c, LANES)
        cum3 = cs_s[rows, :].reshape(cpg, c, LANES)
        tot = jnp.sum(la3, axis=1, keepdims=True)
        b3 = jnp.where(fwd_3, cum3, tot - cum3 + la3)
        b = b3.reshape(grp, LANES)
        rest = jnp.exp(tot - b3).reshape(grp, LANES)
        qq = own_head_twice(q_ref[rows, :])
        kk = own_head_twice(k_ref[rows, :])
        grow = jnp.exp(-b)
        kin = (kk * grow).astype(BF16)
        qdec_s[rows, :] = (qq / grow).astype(BF16)
        kina_s[rows, :] = jnp.where(fwd_p, kin, zero_p)
        kinb_s[rows, :] = jnp.where(fwd_p, zero_p, kin)
        kst_s[rows, :] = (kk * rest).astype(BF16)
        dec_s[_rows(i, cpg)] = jnp.broadcast_to(jnp.exp(tot), (cpg, 8, LANES))

    def chunk_products(i):
        for cc in range(cpg):
            n = i * cpg + cc
            rows = _rows(n, c)
            kin_bd = jnp.concatenate([kina_s[rows, :], kinb_s[rows, :]], axis=0)
            a2 = _dot_nt(qdec_s[rows, :], kin_bd)
            am_s[rows, :] = jnp.where(tri, a2, zero_c).astype(BF16)
            ds_s[n] = _dot_tn(v_ref[rows, :], kst_s[rows, :])

    _software_pipeline(n_groups, [gate_preact, decay_logs, chunk_cumsum, decayed_qk,
                                  chunk_products], unroll=GLA_PREP_UNROLL)

    fwd_sq = lax.broadcasted_iota(jnp.int32, (LANES, LANES), 1) < HALF
    fwd_row = lax.broadcasted_iota(jnp.int32, (8, LANES), 1) < HALF

    def scan_step(i, s):
        j = nc - 1 - i
        sb = s.astype(BF16)
        sc_s[i, :, 0:HALF] = sb[:, 0:HALF]
        sc_s[j, :, HALF:LANES] = sb[:, HALF:LANES]
        dec = jnp.where(fwd_row, dec_s[i], dec_s[j])[0:1]
        return dec * s + jnp.where(fwd_sq, ds_s[i], ds_s[j])

    lax.fori_loop(0, nc, scan_step, jnp.zeros((LANES, LANES), F32), unroll=GLA_UNROLL)

    nrm = nrm_ref[...]

    def mix(i):
        for cc in range(cpg):
            n = i * cpg + cc
            rows = _rows(n, c)
            vc = v_ref[rows, :]
            vvt = jnp.concatenate([vc, vc], axis=0).T
            la_s[rows, :] = _dot_nt(jnp.concatenate([qdec_s[rows, :], am_s[rows, :]], axis=1),
                                    jnp.concatenate([sc_s[n], vvt], axis=1))

    def norm_gate(i):
        rows = _rows(i, grp)
        gate = g_ref[rows, :].astype(F32)
        half = 0.5 * gate
        swish = half + half * jnp.tanh(half)
        o = (_rms(la_s[rows, :]) * nrm) * swish
        o_ref[rows, :] = o.astype(o_ref.dtype)

    _software_pipeline(n_groups, [mix, norm_gate], unroll=GLA_OUT_UNROLL)


def _gla(q, k, v, g, z, wgu, bg, nrm, batch, seq):
    t = batch * seq
    nc = seq // GLA_CHUNK
    tile = pl.BlockSpec((seq, LANES), lambda b, h: (b, h))
    pair_tile = pl.BlockSpec((seq, LANES), lambda b, h: (b, h // 2))
    tok_bf = pltpu.VMEM((seq, LANES), BF16)
    cm = jnp.asarray(_cumsum_matrix(), dtype=BF16)
    return pl.pallas_call(
        functools.partial(_gla_kernel, seq=seq),
        grid=(batch, GLA_HEADS),
        in_specs=[pair_tile, pair_tile, tile, tile,
                  pl.BlockSpec((seq, Z_W), lambda b, h: (b, 0)),
                  pl.BlockSpec((1, Z_W, LANES), lambda b, h: (h, 0, 0)),
                  pl.BlockSpec((1, 1, LANES), lambda b, h: (h, 0, 0)),
                  pl.BlockSpec((1, LANES), lambda b, h: (0, 0)),
                  _const_spec(cm.shape)],
        out_specs=tile,
        out_shape=jax.ShapeDtypeStruct((t, V_GLA_W), BF16),
        scratch_shapes=[pltpu.VMEM((seq, LANES), F32),
                        pltpu.VMEM((seq, 2 * LANES), BF16),
                        pltpu.VMEM((seq, LANES), F32),
                        tok_bf, tok_bf, tok_bf, tok_bf, tok_bf,
                        pltpu.VMEM((nc, LANES, LANES), F32),
                        pltpu.VMEM((nc, 8, LANES), F32),
                        pltpu.VMEM((nc, LANES, LANES), BF16)],
        compiler_params=pltpu.CompilerParams(
            dimension_semantics=("arbitrary", "arbitrary"), vmem_limit_bytes=VMEM_LIMIT),
        name="gla",
    )(q, k, v, g, z, wgu, bg, nrm, cm)


def _t5_buckets(rel):
    nb = REL_BUCKETS // 2
    ret = (rel > 0).astype(np.int32) * nb
    n = np.abs(rel)
    max_exact = nb // 2
    large = max_exact + (np.log(np.maximum(n, 1).astype(np.float32) / max_exact)
                         / math.log(REL_MAX_DIST / max_exact) * (nb - max_exact)).astype(np.int32)
    large = np.minimum(large, nb - 1)
    return ret + np.where(n < max_exact, n, large)


def _bucket_map():
    w = SWA_BLOCK
    cq = np.arange(w)[:, None]
    s = np.arange(3 * w)[None, :]
    rel = s - w - cq
    return np.where(np.abs(rel) <= SWA_WINDOW, _t5_buckets(rel), -1).astype(np.int32)


def _bias_kernel(bm_ref, tab_ref, o_ref):
    w = SWA_BLOCK
    pair = pl.program_id(0)
    bm = bm_ref[...]
    key = lax.broadcasted_iota(jnp.int32, bm.shape, 1)
    masked = jnp.full(bm.shape, MASK_VALUE, F32)
    for hh in range(2):
        head = 2 * pair + hh
        acc = masked
        for bkt in range(REL_BUCKETS):
            acc = jnp.where(bm == bkt, tab_ref[bkt, head], acc)
        hcols = slice(hh * 3 * w, (hh + 1) * 3 * w)
        o_ref[0, 0, :, hcols] = jnp.where(key >= w, acc, masked)
        o_ref[1, 0, :, hcols] = acc
        o_ref[2, 0, :, hcols] = jnp.where(key < 2 * w, acc, masked)


def _bias_tables(rel_bias):
    w = SWA_BLOCK
    pairs = SWA_Q_HEADS // 2
    return pl.pallas_call(
        _bias_kernel,
        grid=(pairs,),
        in_specs=[pl.BlockSpec((w, 3 * w), lambda p: (0, 0)),
                  pl.BlockSpec(memory_space=pltpu.SMEM)],
        out_specs=pl.BlockSpec((3, 1, w, 6 * w), lambda p: (0, p, 0, 0)),
        out_shape=jax.ShapeDtypeStruct((3, pairs, w, 6 * w), F32),
        compiler_params=pltpu.CompilerParams(dimension_semantics=("arbitrary",)),
        name="swa_bias",
    )(jnp.asarray(_bucket_map()), rel_bias.astype(F32))


def _swa_kernel(q_ref, k_ref, v_ref, bias_ref, sink_ref, o_ref, ka_s, kb_s, va_s, vb_s,
                lg_s, p_s, st_s, *, seq):
    w = SWA_BLOCK
    nb = seq // w
    kvh = pl.program_id(1)
    pairs = SWA_Q_HEADS // SWA_KV_HEADS // 2
    stage_rows = 512
    lane_s = lax.broadcasted_iota(jnp.int16, (stage_rows, LANES), 1)
    lo = lane_s < HALF
    own = lo == (kvh == 0)
    for s_ref in (ka_s, kb_s, va_s, vb_s):
        zpad = jnp.zeros((w, s_ref.shape[1]), BF16)
        s_ref[0:w, :] = zpad
        s_ref[w + seq:2 * w + seq, :] = zpad

    def stage(i, carry):
        src = pl.ds(pl.multiple_of(i * stage_rows, stage_rows), stage_rows)
        dst = pl.ds(pl.multiple_of(i * stage_rows + w, w), stage_rows)
        k2 = k_ref[src, :]
        v2 = v_ref[src, :]
        kk = jnp.where(own, k2, pltpu.roll(k2, HALF, axis=1))
        vv = jnp.where(own, v2, pltpu.roll(v2, HALF, axis=1))
        zero = jnp.zeros_like(kk)
        one = jnp.ones_like(kk)
        ka_s[dst, :] = jnp.where(lo, kk, zero)
        kb_s[dst, :] = jnp.where(lo, zero, kk)
        va_s[dst, 0:LANES] = jnp.where(lo, vv, zero)
        va_s[dst, LANES:2 * LANES] = jnp.where(lo, one, zero)
        vb_s[dst, 0:LANES] = jnp.where(lo, zero, vv)
        vb_s[dst, LANES:2 * LANES] = jnp.where(lo, zero, one)
        return carry

    lax.fori_loop(0, seq // stage_rows, stage, 0)

    lo_w = lax.broadcasted_iota(jnp.int32, (w, LANES), 1) < HALF

    def logits_stage(n, variant):
        qrows = pl.ds(pl.multiple_of(n * w, w), w)
        band = pl.ds(pl.multiple_of(n * w, w), 3 * w)
        kt = jnp.concatenate([ka_s[band, :], kb_s[band, :]], axis=0)
        for pair in range(pairs):
            cols = slice(pair * LANES, (pair + 1) * LANES)
            lg_s[pair] = _dot_nt(q_ref[qrows, cols], kt) + bias_ref[variant, pair]

    def softmax_stage():
        for pair in range(pairs):
            sink_terms = []
            for hh in range(2):
                hcols = slice(hh * 3 * w, (hh + 1) * 3 * w)
                lg = lg_s[pair, :, hcols]
                sink = sink_ref[kvh * 2 * pairs + 2 * pair + hh]
                m = jnp.maximum(jnp.max(lg, axis=-1, keepdims=True), sink)
                p_s[pair, :, hcols] = jnp.exp(lg - m).astype(BF16)
                sink_terms.append(jnp.exp(sink - m))
            st_s[pair] = jnp.where(lo_w, sink_terms[0], sink_terms[1])

    def output_stage(n):
        qrows = pl.ds(pl.multiple_of(n * w, w), w)
        band = pl.ds(pl.multiple_of(n * w, w), 3 * w)
        vx = jnp.concatenate([va_s[band, :], vb_s[band, :]], axis=0)
        for pair in range(pairs):
            ox = _dot(p_s[pair], vx)
            o = ox[:, :LANES] / (ox[:, LANES:] + st_s[pair])
            o_ref[qrows, pair * LANES:(pair + 1) * LANES] = o.astype(o_ref.dtype)

    logits_stage(0, 0)
    softmax_stage()
    logits_stage(1, 1)

    def steady(n, carry):
        output_stage(n - 1)
        softmax_stage()
        logits_stage(n + 1, 1)
        return carry

    lax.fori_loop(1, nb - 2, steady, 0, unroll=SWA_UNROLL)
    output_stage(nb - 3)
    softmax_stage()
    logits_stage(nb - 1, 2)
    output_stage(nb - 2)
    softmax_stage()
    output_stage(nb - 1)


def _swa(q, k, v, bias, sink, batch, seq):
    t = batch * seq
    w = SWA_BLOCK
    qw = Q_SWA_W // SWA_KV_HEADS
    pairs = SWA_Q_HEADS // SWA_KV_HEADS // 2
    staged_k = pltpu.VMEM((seq + 2 * w, LANES), BF16)
    staged_v = pltpu.VMEM((seq + 2 * w, 2 * LANES), BF16)
    return pl.pallas_call(
        functools.partial(_swa_kernel, seq=seq),
        grid=(batch, SWA_KV_HEADS),
        in_specs=[pl.BlockSpec((seq, qw), lambda b, j: (b, j)),
                  pl.BlockSpec((seq, LANES), lambda b, j: (b, 0)),
                  pl.BlockSpec((seq, LANES), lambda b, j: (b, 0)),
                  pl.BlockSpec((3, pairs, w, 6 * w), lambda b, j: (0, j, 0, 0)),
                  pl.BlockSpec(memory_space=pltpu.SMEM)],
        out_specs=pl.BlockSpec((seq, qw), lambda b, j: (b, j)),
        out_shape=jax.ShapeDtypeStruct((t, Q_SWA_W), BF16),
        scratch_shapes=[staged_k, staged_k, staged_v, staged_v,
                        pltpu.VMEM((pairs, w, 6 * w), F32),
                        pltpu.VMEM((pairs, w, 6 * w), BF16),
                        pltpu.VMEM((pairs, w, LANES), F32)],
        compiler_params=pltpu.CompilerParams(
            dimension_semantics=("arbitrary", "arbitrary"), vmem_limit_bytes=VMEM_LIMIT),
        name="swa",
    )(q, k, v, bias, sink)


def _ffn_kernel(x_ref, oa_ref, os_ref, wout_ref, gpost_ref, gpre_ref, wup_ref, wdn_ref,
                gout_ref, o_ref):
    slab = FFN_TOKEN_BLOCK // FFN_SLABS
    slabs = [slice(s * slab, (s + 1) * slab) for s in range(FFN_SLABS)]
    mixes = [_dot(oa_ref[rows, :], wout_ref[0:V_GLA_W, :])
             + _dot(os_ref[rows, :], wout_ref[V_GLA_W:V_GLA_W + Q_SWA_W, :]) for rows in slabs]
    for rows, mix in zip(slabs, mixes):
        h = x_ref[rows, :] + _rms(mix) * gpost_ref[...]
        hn = (_rms(h) * gpre_ref[...]).astype(BF16)
        acc = jnp.zeros(h.shape, F32)
        for j in range(D_FF // FF_BLOCK):
            cols = slice(j * FF_BLOCK, (j + 1) * FF_BLOCK)
            zj = jnp.maximum(_dot(hn, wup_ref[:, cols]), 0.0)
            acc = acc + _dot((zj * zj).astype(BF16), wdn_ref[cols, :])
        o_ref[rows, :] = h + _rms(acc) * gout_ref[...]


def _ffn(x2, oa, os_, wout, gpost, gpre, wup, wdn, gout):
    t = x2.shape[0]
    row = lambda width: pl.BlockSpec((FFN_TOKEN_BLOCK, width), lambda i: (i, 0))
    return pl.pallas_call(
        _ffn_kernel,
        grid=(t // FFN_TOKEN_BLOCK,),
        in_specs=[row(D_MODEL), row(V_GLA_W), row(Q_SWA_W),
                  _const_spec((V_GLA_W + Q_SWA_W, D_MODEL)),
                  _const_spec((1, D_MODEL)), _const_spec((1, D_MODEL)),
                  _const_spec((D_MODEL, D_FF)), _const_spec((D_FF, D_MODEL)),
                  _const_spec((1, D_MODEL))],
        out_specs=row(D_MODEL),
        out_shape=jax.ShapeDtypeStruct((t, D_MODEL), F32),
        compiler_params=pltpu.CompilerParams(
            dimension_semantics=("arbitrary",), vmem_limit_bytes=VMEM_LIMIT),
        name="outproj_ffn",
    )(x2, oa, os_, wout, gpost, gpre, wup, wdn, gout)


def _layer(h, norm_mix_pre, w_in, w_gu_f, b_g_f, w_gu_b, b_g_b, gla_norm, swa_sink, bias_tab,
           w_out, norm_mix_post, norm_mlp_pre, w_up, w_down, norm_mlp_post):
    batch, seq, _ = h.shape
    x2 = h.reshape(batch * seq, D_MODEL)
    row = lambda g: g.reshape(1, -1).astype(F32)

    later = [w.astype(F32) for w in (w_out, w_up, w_down)]
    qa, ka, va, ga, qs, ks, vs, z, w_out_b, w_up_b, w_down_b = _inproj(
        x2, row(norm_mix_pre), w_in.astype(F32).T, later)

    wf = w_gu_f.reshape(GLA_RANK, GLA_HEADS, GLA_DK).transpose(1, 0, 2)
    wb = w_gu_b.reshape(GLA_RANK, GLA_HEADS, GLA_DK).transpose(1, 0, 2)
    zeros = jnp.zeros_like(wf)
    wgu = jnp.concatenate([jnp.concatenate([wf, zeros], axis=2),
                           jnp.concatenate([zeros, wb], axis=2)], axis=1).astype(BF16)
    bg = jnp.concatenate([b_g_f.reshape(GLA_HEADS, 1, GLA_DK),
                          b_g_b.reshape(GLA_HEADS, 1, GLA_DK)], axis=2).astype(F32)
    o_a = _gla(qa, ka, va, ga, z, wgu, bg, row(gla_norm), batch, seq)

    o_s = _swa(qs, ks, vs, bias_tab, swa_sink.astype(F32), batch, seq)

    out = _ffn(x2, o_a, o_s, w_out_b, row(norm_mix_post), row(norm_mlp_pre), w_up_b, w_down_b,
               row(norm_mlp_post))
    return out.reshape(batch, seq, D_MODEL)


def kernel(x, norm_mix_pre, w_in, w_gate_up_fwd, b_gate_fwd, w_gate_up_bwd, b_gate_bwd, gla_norm,
           swa_sink, rel_bias, w_out, norm_mix_post, norm_mlp_pre, w_up, w_down, norm_mlp_post):
    bias_tab = _bias_tables(rel_bias)
    h = x
    for l in range(w_in.shape[0]):
        h = _layer(h, norm_mix_pre[l], w_in[l], w_gate_up_fwd[l], b_gate_fwd[l], w_gate_up_bwd[l],
                   b_gate_bwd[l], gla_norm[l], swa_sink[l], bias_tab, w_out[l], norm_mix_post[l],
                   norm_mlp_pre[l], w_up[l], w_down[l], norm_mlp_post[l])
    return h
```

```python
import functools
import math

import jax
import jax.numpy as jnp
import numpy as np
from jax import lax
from jax.experimental import pallas as pl
from jax.experimental.pallas import tpu as pltpu

F32 = jnp.float32
BF16 = jnp.bfloat16

D_MODEL = 1024
GLA_HEADS = 4
GLA_DK = 64
GLA_DV = 128
GLA_RANK = 16
GLA_GATE_NORM = 16.0
GLA_CHUNK = 64
SWA_DH = 64
SWA_Q_HEADS = 8
SWA_KV_HEADS = 2
SWA_BLOCK = 128
SWA_WINDOW = 128
REL_BUCKETS = 32
REL_MAX_DIST = 128
D_FF = 4 * D_MODEL
NORM_EPS = 1e-6
MASK_VALUE = -1e30

LANES = 128
HALF = LANES // 2

QK_GLA_W = GLA_HEADS * GLA_DK
V_GLA_W = GLA_HEADS * GLA_DV
G_GLA_W = GLA_HEADS * GLA_DV
Q_SWA_W = SWA_Q_HEADS * SWA_DH
KV_SWA_W = SWA_KV_HEADS * SWA_DH
Z_W = 2 * GLA_RANK
IN_WIDTHS = (QK_GLA_W, QK_GLA_W, V_GLA_W, G_GLA_W, Q_SWA_W, KV_SWA_W, KV_SWA_W, Z_W)
IN_COLS = sum(IN_WIDTHS)
IN_DOT_GROUPS = ((0, 1), (2,), (3,), (4,), (5, 6), (7,))
N_LATER_WEIGHTS = 3

GLA_UNROLL = 64
SWA_UNROLL = 29
GLA_GROUP = 256
GLA_PREP_UNROLL = 12
GLA_OUT_UNROLL = 15
TOKEN_BLOCK = 1024
FFN_TOKEN_BLOCK = 1024
FF_BLOCK = 1024
FFN_SLABS = 4
VMEM_LIMIT = 56 * 1024 * 1024


def _rms(x):
    return x * lax.rsqrt(jnp.mean(x * x, axis=-1, keepdims=True) + NORM_EPS)


def _dot(a, b):
    return jnp.dot(a, b, preferred_element_type=F32)


def _dot_nt(a, b):
    return lax.dot_general(a, b, (((1,), (1,)), ((), ())), preferred_element_type=F32)


def _dot_tn(a, b):
    return lax.dot_general(a, b, (((0,), (0,)), ((), ())), preferred_element_type=F32)


def _const_spec(shape):
    nd = len(shape)
    return pl.BlockSpec(shape, lambda *_: (0,) * nd, pipeline_mode=pl.Buffered(1))


def _rows(item, size):
    start = item * size
    if not isinstance(start, int):
        start = pl.multiple_of(start, size)
    return pl.ds(start, size)


def _software_pipeline(n_items, stages, unroll=1):
    depth = len(stages)

    def trip(t, static):
        for k in reversed(range(depth)):
            if not static or 0 <= t - k < n_items:
                stages[k](t - k)

    for t in range(depth - 1):
        trip(t, True)

    def steady(t, carry):
        trip(t, False)
        return carry

    lax.fori_loop(depth - 1, n_items, steady, 0, unroll=unroll)
    for t in range(n_items, n_items + depth - 1):
        trip(t, True)


def _stage_w_in(wt_ref, w_s):
    z_src = 2 * QK_GLA_W + V_GLA_W + G_GLA_W
    tail = z_src + Z_W
    step = 128

    def copy_rows(src, dst, n_rows, scale):
        def body(i, carry):
            w = wt_ref[pl.ds(src + i * step, step), :]
            w_s[pl.ds(dst + i * step, step), :] = (w if scale == 1.0 else w * scale).astype(BF16)
            return carry
        lax.fori_loop(0, n_rows // step, body, 0)

    copy_rows(0, 0, QK_GLA_W, GLA_DK ** -0.5)
    copy_rows(QK_GLA_W, QK_GLA_W, z_src - QK_GLA_W, 1.0)
    copy_rows(tail, z_src, Q_SWA_W, SWA_DH ** -0.5)
    copy_rows(tail + Q_SWA_W, z_src + Q_SWA_W, 2 * KV_SWA_W, 1.0)
    w_s[IN_COLS - Z_W:IN_COLS, :] = wt_ref[z_src:tail, :].astype(BF16)


def _inproj_kernel(x_ref, g_ref, w_ref, *refs):
    n_proj = len(IN_WIDTHS)
    later_f32, refs = refs[:N_LATER_WEIGHTS], refs[N_LATER_WEIGHTS:]
    out_refs, later_bf16, w_s = refs[:n_proj], refs[n_proj:-1], refs[-1]

    @pl.when(pl.program_id(0) == 0)
    def _():
        _stage_w_in(w_ref, w_s)

    for src, dst in zip(later_f32, later_bf16):
        dst[...] = src[...].astype(BF16)

    u = (_rms(x_ref[...]) * g_ref[...]).astype(BF16)
    off = 0
    for group in IN_DOT_GROUPS:
        width = sum(IN_WIDTHS[i] for i in group)
        y = _dot_nt(u, w_s[off:off + width, :])
        off += width
        col = 0
        for i in group:
            out_refs[i][...] = y[:, col:col + IN_WIDTHS[i]].astype(out_refs[i].dtype)
            col += IN_WIDTHS[i]


def _inproj(x2, g, w, later_weights):
    t = x2.shape[0]
    steps = t // TOKEN_BLOCK
    row = lambda width: pl.BlockSpec((TOKEN_BLOCK, width), lambda i: (i, 0))
    widths = IN_WIDTHS
    dtypes = (BF16,) * (len(IN_WIDTHS) - 1) + (F32,)
    assert len(later_weights) == N_LATER_WEIGHTS
    slices = [pl.BlockSpec((lw.shape[0] // steps, lw.shape[1]), lambda i: (i, 0))
              for lw in later_weights]
    return pl.pallas_call(
        _inproj_kernel,
        grid=(steps,),
        in_specs=[row(D_MODEL), _const_spec((1, D_MODEL)), _const_spec((IN_COLS, D_MODEL))] + slices,
        out_specs=[row(wd) for wd in widths] + slices,
        out_shape=([jax.ShapeDtypeStruct((t, wd), dt) for wd, dt in zip(widths, dtypes)]
                   + [jax.ShapeDtypeStruct(lw.shape, BF16) for lw in later_weights]),
        scratch_shapes=[pltpu.VMEM((IN_COLS, D_MODEL), BF16)],
        compiler_params=pltpu.CompilerParams(
            dimension_semantics=("arbitrary",), vmem_limit_bytes=VMEM_LIMIT),
        name="inproj",
    )(x2, g, w, *later_weights)


def _cumsum_matrix():
    r = np.arange(GLA_GROUP)[:, None]
    s = np.arange(GLA_GROUP)[None, :]
    return (((r // GLA_CHUNK) == (s // GLA_CHUNK)) & (s <= r)).astype(np.float32)


def _gla_kernel(q_ref, k_ref, v_ref, g_ref, z_ref, wgu_ref, bg_ref, nrm_ref, cm_ref, o_ref,
                *scratch, seq):
    for hh in range(2):
        cols = slice(hh * LANES, (hh + 1) * LANES)
        _gla_head(hh, q_ref, k_ref, v_ref.at[:, cols], g_ref.at[:, cols], z_ref, wgu_ref[hh],
                  bg_ref[hh], nrm_ref, cm_ref, o_ref.at[:, cols], *scratch, seq=seq)


def _gla_head(hh, q_ref, k_ref, v_ref, g_ref, z_ref, wgu, bg, nrm_ref, cm_ref, o_ref,
              la_s, hl_s, cs_s, qdec_s, kina_s, kinb_s, kst_s, am_s, ds_s, dec_s, sc_s, *, seq):
    c = GLA_CHUNK
    nc = seq // c
    grp = GLA_GROUP
    cpg = grp // c
    n_groups = seq // grp
    fwd_p = lax.broadcasted_iota(jnp.int16, (grp, LANES), 1) < HALF
    zero_p = jnp.zeros((grp, LANES), BF16)

    def own_head_twice(tile):
        rolled = pltpu.roll(tile, HALF, axis=1)
        both = jnp.where(fwd_p, tile, rolled) if hh == 0 else jnp.where(fwd_p, rolled, tile)
        return both.astype(F32)

    fwd_3 = lax.broadcasted_iota(jnp.int32, (cpg, c, LANES), 2) < HALF
    lane_c = lax.broadcasted_iota(jnp.int32, (c, LANES), 1)
    row_c = lax.broadcasted_iota(jnp.int32, (c, LANES), 0)
    tri = ((lane_c < HALF) & (lane_c <= row_c)) | ((lane_c >= HALF) & ((lane_c - HALF) >= row_c))
    zero_c = jnp.zeros((c, LANES), F32)

    def gate_preact(i):
        rows = _rows(i, grp)
        la_s[rows, :] = _dot(z_ref[rows, :].astype(BF16), wgu)

    def decay_logs(i):
        rows = _rows(i, grp)
        pre = la_s[rows, :] + bg
        la = ((jnp.minimum(pre, 0.0) - jnp.log(1.0 + jnp.exp(-jnp.abs(pre))))
              * (1.0 / GLA_GATE_NORM))
        la_hi = la.astype(BF16)
        la_s[rows, :] = la
        hl_s[rows, 0:LANES] = la_hi
        hl_s[rows, LANES:2 * LANES] = (la - la_hi.astype(F32)).astype(BF16)

    def chunk_cumsum(i):
        rows = _rows(i, grp)
        cs = _dot(cm_ref[...], hl_s[rows, :])
        cs_s[rows, :] = cs[:, :LANES] + cs[:, LANES:]

    def decayed_qk(i):
        rows = _rows(i, grp)
        la3 = la_s[rows, :].reshape(cpg, c, LANES)
        cum3 = cs_s[rows, :].reshape(cpg, c, LANES)
        tot = jnp.sum(la3, axis=1, keepdims=True)
        b3 = jnp.where(fwd_3, cum3, tot - cum3 + la3)
        b = b3.reshape(grp, LANES)
        rest = jnp.exp(tot - b3).reshape(grp, LANES)
        qq = own_head_twice(q_ref[rows, :])
        kk = own_head_twice(k_ref[rows, :])
        grow = jnp.exp(-b)
        kin = (kk * grow).astype(BF16)
        qdec_s[rows, :] = (qq / grow).astype(BF16)
        kina_s[rows, :] = jnp.where(fwd_p, kin, zero_p)
        kinb_s[rows, :] = jnp.where(fwd_p, zero_p, kin)
        kst_s[rows, :] = (kk * rest).astype(BF16)
        dec_s[_rows(i, cpg)] = jnp.broadcast_to(jnp.exp(tot), (cpg, 8, LANES))

    def chunk_products(i):
        for cc in range(cpg):
            n = i * cpg + cc
            rows = _rows(n, c)
            kin_bd = jnp.concatenate([kina_s[rows, :], kinb_s[rows, :]], axis=0)
            a2 = _dot_nt(qdec_s[rows, :], kin_bd)
            am_s[rows, :] = jnp.where(tri, a2, zero_c).astype(BF16)
            ds_s[n] = _dot_tn(v_ref[rows, :], kst_s[rows, :])

    _software_pipeline(n_groups, [gate_preact, decay_logs, chunk_cumsum, decayed_qk,
                                  chunk_products], unroll=GLA_PREP_UNROLL)

    fwd_sq = lax.broadcasted_iota(jnp.int32, (LANES, LANES), 1) < HALF
    fwd_row = lax.broadcasted_iota(jnp.int32, (8, LANES), 1) < HALF

    def scan_step(i, s):
        j = nc - 1 - i
        sb = s.astype(BF16)
        sc_s[i, :, 0:HALF] = sb[:, 0:HALF]
        sc_s[j, :, HALF:LANES] = sb[:, HALF:LANES]
        dec = jnp.where(fwd_row, dec_s[i], dec_s[j])[0:1]
        return dec * s + jnp.where(fwd_sq, ds_s[i], ds_s[j])

    lax.fori_loop(0, nc, scan_step, jnp.zeros((LANES, LANES), F32), unroll=GLA_UNROLL)

    nrm = nrm_ref[...]

    def mix(i):
        for cc in range(cpg):
            n = i * cpg + cc
            rows = _rows(n, c)
            vc = v_ref[rows, :]
            vvt = jnp.concatenate([vc, vc], axis=0).T
            la_s[rows, :] = _dot_nt(jnp.concatenate([qdec_s[rows, :], am_s[rows, :]], axis=1),
                                    jnp.concatenate([sc_s[n], vvt], axis=1))

    def norm_gate(i):
        rows = _rows(i, grp)
        gate = g_ref[rows, :].astype(F32)
        half = 0.5 * gate
        swish = half + half * jnp.tanh(half)
        o = (_rms(la_s[rows, :]) * nrm) * swish
        o_ref[rows, :] = o.astype(o_ref.dtype)

    _software_pipeline(n_groups, [mix, norm_gate], unroll=GLA_OUT_UNROLL)


def _gla(q, k, v, g, z, wgu, bg, nrm, batch, seq):
    t = batch * seq
    nc = seq // GLA_CHUNK
    qk_tile = pl.BlockSpec((seq, LANES), lambda b, p: (b, p))
    pair_tile = pl.BlockSpec((seq, 2 * LANES), lambda b, p: (b, p))
    tok_bf = pltpu.VMEM((seq, LANES), BF16)
    cm = jnp.asarray(_cumsum_matrix(), dtype=BF16)
    return pl.pallas_call(
        functools.partial(_gla_kernel, seq=seq),
        grid=(batch, GLA_HEADS // 2),
        in_specs=[qk_tile, qk_tile, pair_tile, pair_tile,
                  pl.BlockSpec((seq, Z_W), lambda b, p: (b, 0)),
                  pl.BlockSpec((2, Z_W, LANES), lambda b, p: (p, 0, 0)),
                  pl.BlockSpec((2, 1, LANES), lambda b, p: (p, 0, 0)),
                  pl.BlockSpec((1, LANES), lambda b, p: (0, 0)),
                  _const_spec(cm.shape)],
        out_specs=pair_tile,
        out_shape=jax.ShapeDtypeStruct((t, V_GLA_W), BF16),
        scratch_shapes=[pltpu.VMEM((seq, LANES), F32),
                        pltpu.VMEM((seq, 2 * LANES), BF16),
                        pltpu.VMEM((seq, LANES), F32),
                        tok_bf, tok_bf, tok_bf, tok_bf, tok_bf,
                        pltpu.VMEM((nc, LANES, LANES), F32),
                        pltpu.VMEM((nc, 8, LANES), F32),
                        pltpu.VMEM((nc, LANES, LANES), BF16)],
        compiler_params=pltpu.CompilerParams(
            dimension_semantics=("arbitrary", "arbitrary"), vmem_limit_bytes=VMEM_LIMIT),
        name="gla",
    )(q, k, v, g, z, wgu, bg, nrm, cm)


def _t5_buckets(rel):
    nb = REL_BUCKETS // 2
    ret = (rel > 0).astype(np.int32) * nb
    n = np.abs(rel)
    max_exact = nb // 2
    large = max_exact + (np.log(np.maximum(n, 1).astype(np.float32) / max_exact)
                         / math.log(REL_MAX_DIST / max_exact) * (nb - max_exact)).astype(np.int32)
    large = np.minimum(large, nb - 1)
    return ret + np.where(n < max_exact, n, large)


def _bucket_map():
    w = SWA_BLOCK
    cq = np.arange(w)[:, None]
    s = np.arange(3 * w)[None, :]
    rel = s - w - cq
    return np.where(np.abs(rel) <= SWA_WINDOW, _t5_buckets(rel), -1).astype(np.int32)


def _bias_kernel(bm_ref, tab_ref, o_ref):
    w = SWA_BLOCK
    pair = pl.program_id(0)
    bm = bm_ref[...]
    key = lax.broadcasted_iota(jnp.int32, bm.shape, 1)
    masked = jnp.full(bm.shape, MASK_VALUE, F32)
    for hh in range(2):
        head = 2 * pair + hh
        acc = masked
        for bkt in range(REL_BUCKETS):
            acc = jnp.where(bm == bkt, tab_ref[bkt, head], acc)
        hcols = slice(hh * 3 * w, (hh + 1) * 3 * w)
        o_ref[0, 0, :, hcols] = jnp.where(key >= w, acc, masked)
        o_ref[1, 0, :, hcols] = acc
        o_ref[2, 0, :, hcols] = jnp.where(key < 2 * w, acc, masked)


def _bias_tables(rel_bias):
    w = SWA_BLOCK
    pairs = SWA_Q_HEADS // 2
    return pl.pallas_call(
        _bias_kernel,
        grid=(pairs,),
        in_specs=[pl.BlockSpec((w, 3 * w), lambda p: (0, 0)),
                  pl.BlockSpec(memory_space=pltpu.SMEM)],
        out_specs=pl.BlockSpec((3, 1, w, 6 * w), lambda p: (0, p, 0, 0)),
        out_shape=jax.ShapeDtypeStruct((3, pairs, w, 6 * w), F32),
        compiler_params=pltpu.CompilerParams(dimension_semantics=("arbitrary",)),
        name="swa_bias",
    )(jnp.asarray(_bucket_map()), rel_bias.astype(F32))


def _swa_kernel(q_ref, k_ref, v_ref, bias_ref, sink_ref, o_ref, ka_s, kb_s, va_s, vb_s,
                lg_s, p_s, st_s, *, seq):
    w = SWA_BLOCK
    nb = seq // w
    kvh = pl.program_id(1)
    pairs = SWA_Q_HEADS // SWA_KV_HEADS // 2
    stage_rows = 512
    lane_s = lax.broadcasted_iota(jnp.int16, (stage_rows, LANES), 1)
    lo = lane_s < HALF
    own = lo == (kvh == 0)
    for s_ref in (ka_s, kb_s, va_s, vb_s):
        zpad = jnp.zeros((w, s_ref.shape[1]), BF16)
        s_ref[0:w, :] = zpad
        s_ref[w + seq:2 * w + seq, :] = zpad

    def stage(i, carry):
        src = pl.ds(pl.multiple_of(i * stage_rows, stage_rows), stage_rows)
        dst = pl.ds(pl.multiple_of(i * stage_rows + w, w), stage_rows)
        k2 = k_ref[src, :]
        v2 = v_ref[src, :]
        kk = jnp.where(own, k2, pltpu.roll(k2, HALF, axis=1))
        vv = jnp.where(own, v2, pltpu.roll(v2, HALF, axis=1))
        zero = jnp.zeros_like(kk)
        one = jnp.ones_like(kk)
        ka_s[dst, :] = jnp.where(lo, kk, zero)
        kb_s[dst, :] = jnp.where(lo, zero, kk)
        va_s[dst, 0:LANES] = jnp.where(lo, vv, zero)
        va_s[dst, LANES:2 * LANES] = jnp.where(lo, one, zero)
        vb_s[dst, 0:LANES] = jnp.where(lo, zero, vv)
        vb_s[dst, LANES:2 * LANES] = jnp.where(lo, zero, one)
        return carry

    lax.fori_loop(0, seq // stage_rows, stage, 0)

    lo_w = lax.broadcasted_iota(jnp.int32, (w, LANES), 1) < HALF

    def logits_stage(n, variant):
        qrows = pl.ds(pl.multiple_of(n * w, w), w)
        band = pl.ds(pl.multiple_of(n * w, w), 3 * w)
        kt = jnp.concatenate([ka_s[band, :], kb_s[band, :]], axis=0)
        for pair in range(pairs):
            cols = slice(pair * LANES, (pair + 1) * LANES)
            lg_s[pair] = _dot_nt(q_ref[qrows, cols], kt) + bias_ref[variant, pair]

    def softmax_stage():
        for pair in range(pairs):
            sink_terms = []
            for hh in range(2):
                hcols = slice(hh * 3 * w, (hh + 1) * 3 * w)
                lg = lg_s[pair, :, hcols]
                sink = sink_ref[kvh * 2 * pairs + 2 * pair + hh]
                m = jnp.maximum(jnp.max(lg, axis=-1, keepdims=True), sink)
                p_s[pair, :, hcols] = jnp.exp(lg - m).astype(BF16)
                sink_terms.append(jnp.exp(sink - m))
            st_s[pair] = jnp.where(lo_w, sink_terms[0], sink_terms[1])

    def output_stage(n):
        qrows = pl.ds(pl.multiple_of(n * w, w), w)
        band = pl.ds(pl.multiple_of(n * w, w), 3 * w)
        vx = jnp.concatenate([va_s[band, :], vb_s[band, :]], axis=0)
        for pair in range(pairs):
            ox = _dot(p_s[pair], vx)
            o = ox[:, :LANES] / (ox[:, LANES:] + st_s[pair])
            o_ref[qrows, pair * LANES:(pair + 1) * LANES] = o.astype(o_ref.dtype)

    logits_stage(0, 0)
    softmax_stage()
    logits_stage(1, 1)

    def steady(n, carry):
        output_stage(n - 1)
        softmax_stage()
        logits_stage(n + 1, 1)
        return carry

    lax.fori_loop(1, nb - 2, steady, 0, unroll=SWA_UNROLL)
    output_stage(nb - 3)
    softmax_stage()
    logits_stage(nb - 1, 2)
    output_stage(nb - 2)
    softmax_stage()
    output_stage(nb - 1)


def _swa(q, k, v, bias, sink, batch, seq):
    t = batch * seq
    w = SWA_BLOCK
    qw = Q_SWA_W // SWA_KV_HEADS
    pairs = SWA_Q_HEADS // SWA_KV_HEADS // 2
    staged_k = pltpu.VMEM((seq + 2 * w, LANES), BF16)
    staged_v = pltpu.VMEM((seq + 2 * w, 2 * LANES), BF16)
    return pl.pallas_call(
        functools.partial(_swa_kernel, seq=seq),
        grid=(batch, SWA_KV_HEADS),
        in_specs=[pl.BlockSpec((seq, qw), lambda b, j: (b, j)),
                  pl.BlockSpec((seq, LANES), lambda b, j: (b, 0)),
                  pl.BlockSpec((seq, LANES), lambda b, j: (b, 0)),
                  pl.BlockSpec((3, pairs, w, 6 * w), lambda b, j: (0, j, 0, 0)),
                  pl.BlockSpec(memory_space=pltpu.SMEM)],
        out_specs=pl.BlockSpec((seq, qw), lambda b, j: (b, j)),
        out_shape=jax.ShapeDtypeStruct((t, Q_SWA_W), BF16),
        scratch_shapes=[staged_k, staged_k, staged_v, staged_v,
                        pltpu.VMEM((pairs, w, 6 * w), F32),
                        pltpu.VMEM((pairs, w, 6 * w), BF16),
                        pltpu.VMEM((pairs, w, LANES), F32)],
        compiler_params=pltpu.CompilerParams(
            dimension_semantics=("arbitrary", "arbitrary"), vmem_limit_bytes=VMEM_LIMIT),
        name="swa",
    )(q, k, v, bias, sink)


def _ffn_kernel(x_ref, oa_ref, os_ref, wout_ref, gpost_ref, gpre_ref, wup_ref, wdn_ref,
                gout_ref, o_ref):
    slab = FFN_TOKEN_BLOCK // FFN_SLABS
    slabs = [slice(s * slab, (s + 1) * slab) for s in range(FFN_SLABS)]
    mixes = [_dot(oa_ref[rows, :], wout_ref[0:V_GLA_W, :])
             + _dot(os_ref[rows, :], wout_ref[V_GLA_W:V_GLA_W + Q_SWA_W, :]) for rows in slabs]
    for rows, mix in zip(slabs, mixes):
        h = x_ref[rows, :] + _rms(mix) * gpost_ref[...]
        hn = (_rms(h) * gpre_ref[...]).astype(BF16)
        acc = jnp.zeros(h.shape, F32)
        for j in range(D_FF // FF_BLOCK):
            cols = slice(j * FF_BLOCK, (j + 1) * FF_BLOCK)
            zj = jnp.maximum(_dot(hn, wup_ref[:, cols]), 0.0)
            acc = acc + _dot((zj * zj).astype(BF16), wdn_ref[cols, :])
        o_ref[rows, :] = h + _rms(acc) * gout_ref[...]


def _ffn(x2, oa, os_, wout, gpost, gpre, wup, wdn, gout):
    t = x2.shape[0]
    row = lambda width: pl.BlockSpec((FFN_TOKEN_BLOCK, width), lambda i: (i, 0))
    return pl.pallas_call(
        _ffn_kernel,
        grid=(t // FFN_TOKEN_BLOCK,),
        in_specs=[row(D_MODEL), row(V_GLA_W), row(Q_SWA_W),
                  _const_spec((V_GLA_W + Q_SWA_W, D_MODEL)),
                  _const_spec((1, D_MODEL)), _const_spec((1, D_MODEL)),
                  _const_spec((D_MODEL, D_FF)), _const_spec((D_FF, D_MODEL)),
                  _const_spec((1, D_MODEL))],
        out_specs=row(D_MODEL),
        out_shape=jax.ShapeDtypeStruct((t, D_MODEL), F32),
        compiler_params=pltpu.CompilerParams(
            dimension_semantics=("arbitrary",), vmem_limit_bytes=VMEM_LIMIT),
        name="outproj_ffn",
    )(x2, oa, os_, wout, gpost, gpre, wup, wdn, gout)


def _layer(h, norm_mix_pre, w_in, w_gu_f, b_g_f, w_gu_b, b_g_b, gla_norm, swa_sink, bias_tab,
           w_out, norm_mix_post, norm_mlp_pre, w_up, w_down, norm_mlp_post):
    batch, seq, _ = h.shape
    x2 = h.reshape(batch * seq, D_MODEL)
    row = lambda g: g.reshape(1, -1).astype(F32)

    later = [w.astype(F32) for w in (w_out, w_up, w_down)]
    qa, ka, va, ga, qs, ks, vs, z, w_out_b, w_up_b, w_down_b = _inproj(
        x2, row(norm_mix_pre), w_in.astype(F32).T, later)

    wf = w_gu_f.reshape(GLA_RANK, GLA_HEADS, GLA_DK).transpose(1, 0, 2)
    wb = w_gu_b.reshape(GLA_RANK, GLA_HEADS, GLA_DK).transpose(1, 0, 2)
    zeros = jnp.zeros_like(wf)
    wgu = jnp.concatenate([jnp.concatenate([wf, zeros], axis=2),
                           jnp.concatenate([zeros, wb], axis=2)], axis=1).astype(BF16)
    bg = jnp.concatenate([b_g_f.reshape(GLA_HEADS, 1, GLA_DK),
                          b_g_b.reshape(GLA_HEADS, 1, GLA_DK)], axis=2).astype(F32)
    o_a = _gla(qa, ka, va, ga, z, wgu, bg, row(gla_norm), batch, seq)

    o_s = _swa(qs, ks, vs, bias_tab, swa_sink.astype(F32), batch, seq)

    out = _ffn(x2, o_a, o_s, w_out_b, row(norm_mix_post), row(norm_mlp_pre), w_up_b, w_down_b,
               row(norm_mlp_post))
    return out.reshape(batch, seq, D_MODEL)


def kernel(x, norm_mix_pre, w_in, w_gate_up_fwd, b_gate_fwd, w_gate_up_bwd, b_gate_bwd, gla_norm,
           swa_sink, rel_bias, w_out, norm_mix_post, norm_mlp_pre, w_up, w_down, norm_mlp_post):
    bias_tab = _bias_tables(rel_bias)
    h = x
    for l in range(w_in.shape[0]):
        h = _layer(h, norm_mix_pre[l], w_in[l], w_gate_up_fwd[l], b_gate_fwd[l], w_gate_up_bwd[l],
                   b_gate_bwd[l], gla_norm[l], swa_sink[l], bias_tab, w_out[l], norm_mix_post[l],
                   norm_mlp_pre[l], w_up[l], w_down[l], norm_mlp_post[l])
    return h
```

```python
import functools
import math

import jax
import jax.numpy as jnp
import numpy as np
from jax import lax
from jax.experimental import pallas as pl
from jax.experimental.pallas import tpu as pltpu

F32 = jnp.float32
BF16 = jnp.bfloat16

D_MODEL = 1024
GLA_HEADS = 4
GLA_DK = 64
GLA_DV = 128
GLA_RANK = 16
GLA_GATE_NORM = 16.0
GLA_CHUNK = 64
SWA_DH = 64
SWA_Q_HEADS = 8
SWA_KV_HEADS = 2
SWA_BLOCK = 128
SWA_WINDOW = 128
REL_BUCKETS = 32
REL_MAX_DIST = 128
D_FF = 4 * D_MODEL
NORM_EPS = 1e-6
MASK_VALUE = -1e30

LANES = 128
HALF = LANES // 2

QK_GLA_W = GLA_HEADS * GLA_DK
V_GLA_W = GLA_HEADS * GLA_DV
G_GLA_W = GLA_HEADS * GLA_DV
Q_SWA_W = SWA_Q_HEADS * SWA_DH
KV_SWA_W = SWA_KV_HEADS * SWA_DH
Z_W = 2 * GLA_RANK
IN_WIDTHS = (QK_GLA_W, QK_GLA_W, V_GLA_W, G_GLA_W, Q_SWA_W, KV_SWA_W, KV_SWA_W, Z_W)
IN_COLS = sum(IN_WIDTHS)
IN_DOT_GROUPS = ((0, 1), (2,), (3,), (4,), (5, 6), (7,))
N_LATER_WEIGHTS = 3

GLA_UNROLL = 64
SWA_UNROLL = 29
GLA_GROUP = 256
GLA_PREP_UNROLL = 12
GLA_OUT_UNROLL = 15
TOKEN_BLOCK = 1024
FFN_TOKEN_BLOCK = 1024
FF_BLOCK = 1024
FFN_SLABS = 4
VMEM_LIMIT = 56 * 1024 * 1024


def _rms(x):
    return x * lax.rsqrt(jnp.mean(x * x, axis=-1, keepdims=True) + NORM_EPS)


def _dot(a, b):
    return jnp.dot(a, b, preferred_element_type=F32)


def _dot_nt(a, b):
    return lax.dot_general(a, b, (((1,), (1,)), ((), ())), preferred_element_type=F32)


def _dot_tn(a, b):
    return lax.dot_general(a, b, (((0,), (0,)), ((), ())), preferred_element_type=F32)


def _const_spec(shape):
    nd = len(shape)
    return pl.BlockSpec(shape, lambda *_: (0,) * nd, pipeline_mode=pl.Buffered(1))


def _rows(item, size):
    start = item * size
    if not isinstance(start, int):
        start = pl.multiple_of(start, size)
    return pl.ds(start, size)


def _software_pipeline(n_items, stages, unroll=1):
    depth = len(stages)

    def trip(t, static):
        for k in reversed(range(depth)):
            if not static or 0 <= t - k < n_items:
                stages[k](t - k)

    for t in range(depth - 1):
        trip(t, True)

    def steady(t, carry):
        trip(t, False)
        return carry

    lax.fori_loop(depth - 1, n_items, steady, 0, unroll=unroll)
    for t in range(n_items, n_items + depth - 1):
        trip(t, True)


def _stage_w_in(wt_ref, w_s):
    z_src = 2 * QK_GLA_W + V_GLA_W + G_GLA_W
    tail = z_src + Z_W
    step = 128

    def copy_rows(src, dst, n_rows, scale):
        def body(i, carry):
            w = wt_ref[pl.ds(src + i * step, step), :]
            w_s[pl.ds(dst + i * step, step), :] = (w if scale == 1.0 else w * scale).astype(BF16)
            return carry
        lax.fori_loop(0, n_rows // step, body, 0)

    copy_rows(0, 0, QK_GLA_W, GLA_DK ** -0.5)
    copy_rows(QK_GLA_W, QK_GLA_W, z_src - QK_GLA_W, 1.0)
    copy_rows(tail, z_src, Q_SWA_W, SWA_DH ** -0.5)
    copy_rows(tail + Q_SWA_W, z_src + Q_SWA_W, 2 * KV_SWA_W, 1.0)
    w_s[IN_COLS - Z_W:IN_COLS, :] = wt_ref[z_src:tail, :].astype(BF16)


def _inproj_kernel(x_ref, g_ref, w_ref, *refs):
    n_proj = len(IN_WIDTHS)
    later_f32, refs = refs[:N_LATER_WEIGHTS], refs[N_LATER_WEIGHTS:]
    out_refs, later_bf16, w_s = refs[:n_proj], refs[n_proj:-1], refs[-1]

    @pl.when(pl.program_id(0) == 0)
    def _():
        _stage_w_in(w_ref, w_s)

    for src, dst in zip(later_f32, later_bf16):
        dst[...] = src[...].astype(BF16)

    u = (_rms(x_ref[...]) * g_ref[...]).astype(BF16)
    off = 0
    for group in IN_DOT_GROUPS:
        width = sum(IN_WIDTHS[i] for i in group)
        y = _dot_nt(u, w_s[off:off + width, :])
        off += width
        col = 0
        for i in group:
            out_refs[i][...] = y[:, col:col + IN_WIDTHS[i]].astype(out_refs[i].dtype)
            col += IN_WIDTHS[i]


def _inproj(x2, g, w, later_weights):
    t = x2.shape[0]
    steps = t // TOKEN_BLOCK
    row = lambda width: pl.BlockSpec((TOKEN_BLOCK, width), lambda i: (i, 0))
    widths = IN_WIDTHS
    dtypes = (BF16,) * (len(IN_WIDTHS) - 1) + (F32,)
    assert len(later_weights) == N_LATER_WEIGHTS
    slices = [pl.BlockSpec((lw.shape[0] // steps, lw.shape[1]), lambda i: (i, 0))
              for lw in later_weights]
    return pl.pallas_call(
        _inproj_kernel,
        grid=(steps,),
        in_specs=[row(D_MODEL), _const_spec((1, D_MODEL)), _const_spec((IN_COLS, D_MODEL))] + slices,
        out_specs=[row(wd) for wd in widths] + slices,
        out_shape=([jax.ShapeDtypeStruct((t, wd), dt) for wd, dt in zip(widths, dtypes)]
                   + [jax.ShapeDtypeStruct(lw.shape, BF16) for lw in later_weights]),
        scratch_shapes=[pltpu.VMEM((IN_COLS, D_MODEL), BF16)],
        compiler_params=pltpu.CompilerParams(
            dimension_semantics=("arbitrary",), vmem_limit_bytes=VMEM_LIMIT),
        name="inproj",
    )(x2, g, w, *later_weights)


def _cumsum_matrix():
    r = np.arange(GLA_GROUP)[:, None]
    s = np.arange(GLA_GROUP)[None, :]
    return (((r // GLA_CHUNK) == (s // GLA_CHUNK)) & (s <= r)).astype(np.float32)


def _gla_kernel(q_ref, k_ref, v_ref, g_ref, z_ref, wgu_ref, bg_ref, nrm_ref, cm_ref, o_ref,
                la_s, hl_s, cs_s, qdec_s, kina_s, kinb_s, kst_s, am_s, ds_s, dec_s, sc_s, *, seq):
    c = GLA_CHUNK
    nc = seq // c
    grp = GLA_GROUP
    cpg = grp // c
    n_groups = seq // grp
    wgu = wgu_ref[0]
    bg = bg_ref[0]
    fwd_p = lax.broadcasted_iota(jnp.int16, (grp, LANES), 1) < HALF
    own_p = fwd_p == (pl.program_id(1) % 2 == 0)
    zero_p = jnp.zeros((grp, LANES), BF16)

    def own_head_twice(tile):
        return jnp.where(own_p, tile, pltpu.roll(tile, HALF, axis=1)).astype(F32)

    fwd_3 = lax.broadcasted_iota(jnp.int32, (cpg, c, LANES), 2) < HALF
    lane_c = lax.broadcasted_iota(jnp.int32, (c, LANES), 1)
    row_c = lax.broadcasted_iota(jnp.int32, (c, LANES), 0)
    tri = ((lane_c < HALF) & (lane_c <= row_c)) | ((lane_c >= HALF) & ((lane_c - HALF) >= row_c))
    zero_c = jnp.zeros((c, LANES), F32)

    def gate_preact(i):
        rows = _rows(i, grp)
        la_s[rows, :] = _dot(z_ref[rows, :].astype(BF16), wgu)

    def decay_logs(i):
        rows = _rows(i, grp)
        pre = la_s[rows, :] + bg
        la = ((jnp.minimum(pre, 0.0) - jnp.log(1.0 + jnp.exp(-jnp.abs(pre))))
              * (1.0 / GLA_GATE_NORM))
        la_hi = la.astype(BF16)
        la_s[rows, :] = la
        hl_s[rows, 0:LANES] = la_hi
        hl_s[rows, LANES:2 * LANES] = (la - la_hi.astype(F32)).astype(BF16)

    def chunk_cumsum(i):
        rows = _rows(i, grp)
        cs = _dot(cm_ref[...], hl_s[rows, :])
        cs_s[rows, :] = cs[:, :LANES] + cs[:, LANES:]

    def decayed_qk(i):
        rows = _rows(i, grp)
        la3 = la_s[rows, :].reshape(cpg, c, LANES)
        cum3 = cs_s[rows, :].reshape(cpg, c, LANES)
        tot = cum3[:, c - 1:c, :]
        b3 = jnp.where(fwd_3, cum3, tot - cum3 + la3)
        b = b3.reshape(grp, LANES)
        rest = jnp.exp(tot - b3).reshape(grp, LANES)
        qq = own_head_twice(q_ref[rows, :])
        kk = own_head_twice(k_ref[rows, :])
        grow = jnp.exp(-b)
        kin = (kk * grow).astype(BF16)
        qdec_s[rows, :] = (qq / grow).astype(BF16)
        kina_s[rows, :] = jnp.where(fwd_p, kin, zero_p)
        kinb_s[rows, :] = jnp.where(fwd_p, zero_p, kin)
        kst_s[rows, :] = (kk * rest).astype(BF16)
        dec_s[_rows(i, cpg)] = jnp.broadcast_to(jnp.exp(tot), (cpg, 8, LANES))

    def chunk_products(i):
        for cc in range(cpg):
            n = i * cpg + cc
            rows = _rows(n, c)
            kin_bd = jnp.concatenate([kina_s[rows, :], kinb_s[rows, :]], axis=0)
            a2 = _dot_nt(qdec_s[rows, :], kin_bd)
            am_s[rows, :] = jnp.where(tri, a2, zero_c).astype(BF16)
            ds_s[n] = _dot_tn(v_ref[rows, :], kst_s[rows, :])

    _software_pipeline(n_groups, [gate_preact, decay_logs, chunk_cumsum, decayed_qk,
                                  chunk_products], unroll=GLA_PREP_UNROLL)

    fwd_sq = lax.broadcasted_iota(jnp.int32, (LANES, LANES), 1) < HALF
    fwd_row = lax.broadcasted_iota(jnp.int32, (8, LANES), 1) < HALF

    def scan_step(i, s):
        j = nc - 1 - i
        sb = s.astype(BF16)
        sc_s[i, :, 0:HALF] = sb[:, 0:HALF]
        sc_s[j, :, HALF:LANES] = sb[:, HALF:LANES]
        dec = jnp.where(fwd_row, dec_s[i], dec_s[j])[0:1]
        return dec * s + jnp.where(fwd_sq, ds_s[i], ds_s[j])

    lax.fori_loop(0, nc, scan_step, jnp.zeros((LANES, LANES), F32), unroll=GLA_UNROLL)

    nrm = nrm_ref[...]

    def mix(i):
        for cc in range(cpg):
            n = i * cpg + cc
            rows = _rows(n, c)
            vc = v_ref[rows, :]
            vvt = jnp.concatenate([vc, vc], axis=0).T
            la_s[rows, :] = _dot_nt(jnp.concatenate([qdec_s[rows, :], am_s[rows, :]], axis=1),
                                    jnp.concatenate([sc_s[n], vvt], axis=1))

    def norm_gate(i):
        rows = _rows(i, grp)
        gate = g_ref[rows, :].astype(F32)
        half = 0.5 * gate
        swish = half + half * jnp.tanh(half)
        o = (_rms(la_s[rows, :]) * nrm) * swish
        o_ref[rows, :] = o.astype(o_ref.dtype)

    _software_pipeline(n_groups, [mix, norm_gate], unroll=GLA_OUT_UNROLL)


def _gla(q, k, v, g, z, wgu, bg, nrm, batch, seq):
    t = batch * seq
    nc = seq // GLA_CHUNK
    tile = pl.BlockSpec((seq, LANES), lambda b, h: (b, h))
    pair_tile = pl.BlockSpec((seq, LANES), lambda b, h: (b, h // 2))
    tok_bf = pltpu.VMEM((seq, LANES), BF16)
    cm = jnp.asarray(_cumsum_matrix(), dtype=BF16)
    return pl.pallas_call(
        functools.partial(_gla_kernel, seq=seq),
        grid=(batch, GLA_HEADS),
        in_specs=[pair_tile, pair_tile, tile, tile,
                  pl.BlockSpec((seq, Z_W), lambda b, h: (b, 0)),
                  pl.BlockSpec((1, Z_W, LANES), lambda b, h: (h, 0, 0)),
                  pl.BlockSpec((1, 1, LANES), lambda b, h: (h, 0, 0)),
                  pl.BlockSpec((1, LANES), lambda b, h: (0, 0)),
                  _const_spec(cm.shape)],
        out_specs=tile,
        out_shape=jax.ShapeDtypeStruct((t, V_GLA_W), BF16),
        scratch_shapes=[pltpu.VMEM((seq, LANES), F32),
                        pltpu.VMEM((seq, 2 * LANES), BF16),
                        pltpu.VMEM((seq, LANES), F32),
                        tok_bf, tok_bf, tok_bf, tok_bf, tok_bf,
                        pltpu.VMEM((nc, LANES, LANES), F32),
                        pltpu.VMEM((nc, 8, LANES), F32),
                        pltpu.VMEM((nc, LANES, LANES), BF16)],
        compiler_params=pltpu.CompilerParams(
            dimension_semantics=("arbitrary", "arbitrary"), vmem_limit_bytes=VMEM_LIMIT),
        name="gla",
    )(q, k, v, g, z, wgu, bg, nrm, cm)


def _t5_buckets(rel):
    nb = REL_BUCKETS // 2
    ret = (rel > 0).astype(np.int32) * nb
    n = np.abs(rel)
    max_exact = nb // 2
    large = max_exact + (np.log(np.maximum(n, 1).astype(np.float32) / max_exact)
                         / math.log(REL_MAX_DIST / max_exact) * (nb - max_exact)).astype(np.int32)
    large = np.minimum(large, nb - 1)
    return ret + np.where(n < max_exact, n, large)


def _bucket_map():
    w = SWA_BLOCK
    cq = np.arange(w)[:, None]
    s = np.arange(3 * w)[None, :]
    rel = s - w - cq
    return np.where(np.abs(rel) <= SWA_WINDOW, _t5_buckets(rel), -1).astype(np.int32)


def _bias_kernel(bm_ref, tab_ref, o_ref):
    w = SWA_BLOCK
    pair = pl.program_id(0)
    bm = bm_ref[...]
    key = lax.broadcasted_iota(jnp.int32, bm.shape, 1)
    masked = jnp.full(bm.shape, MASK_VALUE, F32)
    for hh in range(2):
        head = 2 * pair + hh
        acc = masked
        for bkt in range(REL_BUCKETS):
            acc = jnp.where(bm == bkt, tab_ref[bkt, head], acc)
        hcols = slice(hh * 3 * w, (hh + 1) * 3 * w)
        o_ref[0, 0, :, hcols] = jnp.where(key >= w, acc, masked)
        o_ref[1, 0, :, hcols] = acc
        o_ref[2, 0, :, hcols] = jnp.where(key < 2 * w, acc, masked)


def _bias_tables(rel_bias):
    w = SWA_BLOCK
    pairs = SWA_Q_HEADS // 2
    return pl.pallas_call(
        _bias_kernel,
        grid=(pairs,),
        in_specs=[pl.BlockSpec((w, 3 * w), lambda p: (0, 0)),
                  pl.BlockSpec(memory_space=pltpu.SMEM)],
        out_specs=pl.BlockSpec((3, 1, w, 6 * w), lambda p: (0, p, 0, 0)),
        out_shape=jax.ShapeDtypeStruct((3, pairs, w, 6 * w), F32),
        compiler_params=pltpu.CompilerParams(dimension_semantics=("arbitrary",)),
        name="swa_bias",
    )(jnp.asarray(_bucket_map()), rel_bias.astype(F32))


def _swa_kernel(q_ref, k_ref, v_ref, bias_ref, sink_ref, o_ref, ka_s, kb_s, va_s, vb_s,
                lg_s, p_s, st_s, *, seq):
    w = SWA_BLOCK
    nb = seq // w
    kvh = pl.program_id(1)
    pairs = SWA_Q_HEADS // SWA_KV_HEADS // 2
    stage_rows = 512
    lane_s = lax.broadcasted_iota(jnp.int16, (stage_rows, LANES), 1)
    lo = lane_s < HALF
    own = lo == (kvh == 0)
    for s_ref in (ka_s, kb_s, va_s, vb_s):
        zpad = jnp.zeros((w, s_ref.shape[1]), BF16)
        s_ref[0:w, :] = zpad
        s_ref[w + seq:2 * w + seq, :] = zpad

    def stage(i, carry):
        src = pl.ds(pl.multiple_of(i * stage_rows, stage_rows), stage_rows)
        dst = pl.ds(pl.multiple_of(i * stage_rows + w, w), stage_rows)
        k2 = k_ref[src, :]
        v2 = v_ref[src, :]
        kk = jnp.where(own, k2, pltpu.roll(k2, HALF, axis=1))
        vv = jnp.where(own, v2, pltpu.roll(v2, HALF, axis=1))
        zero = jnp.zeros_like(kk)
        one = jnp.ones_like(kk)
        ka_s[dst, :] = jnp.where(lo, kk, zero)
        kb_s[dst, :] = jnp.where(lo, zero, kk)
        va_s[dst, 0:LANES] = jnp.where(lo, vv, zero)
        va_s[dst, LANES:2 * LANES] = jnp.where(lo, one, zero)
        vb_s[dst, 0:LANES] = jnp.where(lo, zero, vv)
        vb_s[dst, LANES:2 * LANES] = jnp.where(lo, zero, one)
        return carry

    lax.fori_loop(0, seq // stage_rows, stage, 0)

    lo_w = lax.broadcasted_iota(jnp.int32, (w, LANES), 1) < HALF

    def logits_stage(n, variant):
        qrows = pl.ds(pl.multiple_of(n * w, w), w)
        band = pl.ds(pl.multiple_of(n * w, w), 3 * w)
        kt = jnp.concatenate([ka_s[band, :], kb_s[band, :]], axis=0)
        for pair in range(pairs):
            cols = slice(pair * LANES, (pair + 1) * LANES)
            lg_s[pair] = _dot_nt(q_ref[qrows, cols], kt) + bias_ref[variant, pair]

    def softmax_stage():
        for pair in range(pairs):
            sink_terms = []
            for hh in range(2):
                hcols = slice(hh * 3 * w, (hh + 1) * 3 * w)
                lg = lg_s[pair, :, hcols]
                sink = sink_ref[kvh * 2 * pairs + 2 * pair + hh]
                m = jnp.maximum(jnp.max(lg, axis=-1, keepdims=True), sink)
                p_s[pair, :, hcols] = jnp.exp(lg - m).astype(BF16)
                sink_terms.append(jnp.exp(sink - m))
            st_s[pair] = jnp.where(lo_w, sink_terms[0], sink_terms[1])

    def output_stage(n):
        qrows = pl.ds(pl.multiple_of(n * w, w), w)
        band = pl.ds(pl.multiple_of(n * w, w), 3 * w)
        vx = jnp.concatenate([va_s[band, :], vb_s[band, :]], axis=0)
        for pair in range(pairs):
            ox = _dot(p_s[pair], vx)
            o = ox[:, :LANES] / (ox[:, LANES:] + st_s[pair])
            o_ref[qrows, pair * LANES:(pair + 1) * LANES] = o.astype(o_ref.dtype)

    logits_stage(0, 0)
    softmax_stage()
    logits_stage(1, 1)

    def steady(n, carry):
        output_stage(n - 1)
        softmax_stage()
        logits_stage(n + 1, 1)
        return carry

    lax.fori_loop(1, nb - 2, steady, 0, unroll=SWA_UNROLL)
    output_stage(nb - 3)
    softmax_stage()
    logits_stage(nb - 1, 2)
    output_stage(nb - 2)
    softmax_stage()
    output_stage(nb - 1)


def _swa(q, k, v, bias, sink, batch, seq):
    t = batch * seq
    w = SWA_BLOCK
    qw = Q_SWA_W // SWA_KV_HEADS
    pairs = SWA_Q_HEADS // SWA_KV_HEADS // 2
    staged_k = pltpu.VMEM((seq + 2 * w, LANES), BF16)
    staged_v = pltpu.VMEM((seq + 2 * w, 2 * LANES), BF16)
    return pl.pallas_call(
        functools.partial(_swa_kernel, seq=seq),
        grid=(batch, SWA_KV_HEADS),
        in_specs=[pl.BlockSpec((seq, qw), lambda b, j: (b, j)),
                  pl.BlockSpec((seq, LANES), lambda b, j: (b, 0)),
                  pl.BlockSpec((seq, LANES), lambda b, j: (b, 0)),
                  pl.BlockSpec((3, pairs, w, 6 * w), lambda b, j: (0, j, 0, 0)),
                  pl.BlockSpec(memory_space=pltpu.SMEM)],
        out_specs=pl.BlockSpec((seq, qw), lambda b, j: (b, j)),
        out_shape=jax.ShapeDtypeStruct((t, Q_SWA_W), BF16),
        scratch_shapes=[staged_k, staged_k, staged_v, staged_v,
                        pltpu.VMEM((pairs, w, 6 * w), F32),
                        pltpu.VMEM((pairs, w, 6 * w), BF16),
                        pltpu.VMEM((pairs, w, LANES), F32)],
        compiler_params=pltpu.CompilerParams(
            dimension_semantics=("arbitrary", "arbitrary"), vmem_limit_bytes=VMEM_LIMIT),
        name="swa",
    )(q, k, v, bias, sink)


def _ffn_kernel(x_ref, oa_ref, os_ref, wout_ref, gpost_ref, gpre_ref, wup_ref, wdn_ref,
                gout_ref, o_ref):
    slab = FFN_TOKEN_BLOCK // FFN_SLABS
    slabs = [slice(s * slab, (s + 1) * slab) for s in range(FFN_SLABS)]
    mixes = [_dot(oa_ref[rows, :], wout_ref[0:V_GLA_W, :])
             + _dot(os_ref[rows, :], wout_ref[V_GLA_W:V_GLA_W + Q_SWA_W, :]) for rows in slabs]
    for rows, mix in zip(slabs, mixes):
        h = x_ref[rows, :] + _rms(mix) * gpost_ref[...]
        hn = (_rms(h) * gpre_ref[...]).astype(BF16)
        acc = jnp.zeros(h.shape, F32)
        for j in range(D_FF // FF_BLOCK):
            cols = slice(j * FF_BLOCK, (j + 1) * FF_BLOCK)
            zj = jnp.maximum(_dot(hn, wup_ref[:, cols]), 0.0)
            acc = acc + _dot((zj * zj).astype(BF16), wdn_ref[cols, :])
        o_ref[rows, :] = h + _rms(acc) * gout_ref[...]


def _ffn(x2, oa, os_, wout, gpost, gpre, wup, wdn, gout):
    t = x2.shape[0]
    row = lambda width: pl.BlockSpec((FFN_TOKEN_BLOCK, width), lambda i: (i, 0))
    return pl.pallas_call(
        _ffn_kernel,
        grid=(t // FFN_TOKEN_BLOCK,),
        in_specs=[row(D_MODEL), row(V_GLA_W), row(Q_SWA_W),
                  _const_spec((V_GLA_W + Q_SWA_W, D_MODEL)),
                  _const_spec((1, D_MODEL)), _const_spec((1, D_MODEL)),
                  _const_spec((D_MODEL, D_FF)), _const_spec((D_FF, D_MODEL)),
                  _const_spec((1, D_MODEL))],
        out_specs=row(D_MODEL),
        out_shape=jax.ShapeDtypeStruct((t, D_MODEL), F32),
        compiler_params=pltpu.CompilerParams(
            dimension_semantics=("arbitrary",), vmem_limit_bytes=VMEM_LIMIT),
        name="outproj_ffn",
    )(x2, oa, os_, wout, gpost, gpre, wup, wdn, gout)


def _layer(h, norm_mix_pre, w_in, w_gu_f, b_g_f, w_gu_b, b_g_b, gla_norm, swa_sink, bias_tab,
           w_out, norm_mix_post, norm_mlp_pre, w_up, w_down, norm_mlp_post):
    batch, seq, _ = h.shape
    x2 = h.reshape(batch * seq, D_MODEL)
    row = lambda g: g.reshape(1, -1).astype(F32)

    later = [w.astype(F32) for w in (w_out, w_up, w_down)]
    qa, ka, va, ga, qs, ks, vs, z, w_out_b, w_up_b, w_down_b = _inproj(
        x2, row(norm_mix_pre), w_in.astype(F32).T, later)

    wf = w_gu_f.reshape(GLA_RANK, GLA_HEADS, GLA_DK).transpose(1, 0, 2)
    wb = w_gu_b.reshape(GLA_RANK, GLA_HEADS, GLA_DK).transpose(1, 0, 2)
    zeros = jnp.zeros_like(wf)
    wgu = jnp.concatenate([jnp.concatenate([wf, zeros], axis=2),
                           jnp.concatenate([zeros, wb], axis=2)], axis=1).astype(BF16)
    bg = jnp.concatenate([b_g_f.reshape(GLA_HEADS, 1, GLA_DK),
                          b_g_b.reshape(GLA_HEADS, 1, GLA_DK)], axis=2).astype(F32)
    o_a = _gla(qa, ka, va, ga, z, wgu, bg, row(gla_norm), batch, seq)

    o_s = _swa(qs, ks, vs, bias_tab, swa_sink.astype(F32), batch, seq)

    out = _ffn(x2, o_a, o_s, w_out_b, row(norm_mix_post), row(norm_mlp_pre), w_up_b, w_down_b,
               row(norm_mlp_post))
    return out.reshape(batch, seq, D_MODEL)


def kernel(x, norm_mix_pre, w_in, w_gate_up_fwd, b_gate_fwd, w_gate_up_bwd, b_gate_bwd, gla_norm,
           swa_sink, rel_bias, w_out, norm_mix_post, norm_mlp_pre, w_up, w_down, norm_mlp_post):
    bias_tab = _bias_tables(rel_bias)
    h = x
    for l in range(w_in.shape[0]):
        h = _layer(h, norm_mix_pre[l], w_in[l], w_gate_up_fwd[l], b_gate_fwd[l], w_gate_up_bwd[l],
                   b_gate_bwd[l], gla_norm[l], swa_sink[l], bias_tab, w_out[l], norm_mix_post[l],
                   norm_mlp_pre[l], w_up[l], w_down[l], norm_mlp_post[l])
    return h
```

```python
import functools
import math

import jax
import jax.numpy as jnp
import numpy as np
from jax import lax
from jax.experimental import pallas as pl
from jax.experimental.pallas import tpu as pltpu

F32 = jnp.float32
BF16 = jnp.bfloat16

D_MODEL = 1024
GLA_HEADS = 4
GLA_DK = 64
GLA_DV = 128
GLA_RANK = 16
GLA_GATE_NORM = 16.0
GLA_CHUNK = 64
SWA_DH = 64
SWA_Q_HEADS = 8
SWA_KV_HEADS = 2
SWA_BLOCK = 128
SWA_WINDOW = 128
REL_BUCKETS = 32
REL_MAX_DIST = 128
D_FF = 4 * D_MODEL
NORM_EPS = 1e-6
MASK_VALUE = -1e30

LANES = 128
HALF = LANES // 2

QK_GLA_W = GLA_HEADS * GLA_DK
V_GLA_W = GLA_HEADS * GLA_DV
G_GLA_W = GLA_HEADS * GLA_DV
Q_SWA_W = SWA_Q_HEADS * SWA_DH
KV_SWA_W = SWA_KV_HEADS * SWA_DH
Z_W = 2 * GLA_RANK
IN_WIDTHS = (QK_GLA_W, QK_GLA_W, V_GLA_W, G_GLA_W, Q_SWA_W, KV_SWA_W, KV_SWA_W, Z_W)
IN_COLS = sum(IN_WIDTHS)
IN_DOT_GROUPS = ((0, 1), (2,), (3,), (4,), (5, 6), (7,))
N_LATER_WEIGHTS = 3

GLA_UNROLL = 64
SWA_UNROLL = 29
GLA_GROUP = 256
GLA_PREP_UNROLL = 12
GLA_OUT_UNROLL = 15
TOKEN_BLOCK = 1024
FFN_TOKEN_BLOCK = 1024
FF_BLOCK = 1024
FFN_SLABS = 4
VMEM_LIMIT = 56 * 1024 * 1024


def _rms(x):
    return x * lax.rsqrt(jnp.mean(x * x, axis=-1, keepdims=True) + NORM_EPS)


def _dot(a, b):
    return jnp.dot(a, b, preferred_element_type=F32)


def _dot_nt(a, b):
    return lax.dot_general(a, b, (((1,), (1,)), ((), ())), preferred_element_type=F32)


def _dot_tn(a, b):
    return lax.dot_general(a, b, (((0,), (0,)), ((), ())), preferred_element_type=F32)


def _const_spec(shape):
    nd = len(shape)
    return pl.BlockSpec(shape, lambda *_: (0,) * nd, pipeline_mode=pl.Buffered(1))


def _rows(item, size):
    start = item * size
    if not isinstance(start, int):
        start = pl.multiple_of(start, size)
    return pl.ds(start, size)


def _software_pipeline(n_items, stages, unroll=1):
    depth = len(stages)

    def trip(t, static):
        for k in reversed(range(depth)):
            if not static or 0 <= t - k < n_items:
                stages[k](t - k)

    for t in range(depth - 1):
        trip(t, True)

    def steady(t, carry):
        trip(t, False)
        return carry

    lax.fori_loop(depth - 1, n_items, steady, 0, unroll=unroll)
    for t in range(n_items, n_items + depth - 1):
        trip(t, True)


def _stage_w_in(wt_ref, w_s):
    z_src = 2 * QK_GLA_W + V_GLA_W + G_GLA_W
    tail = z_src + Z_W
    step = 128

    def copy_rows(src, dst, n_rows, scale):
        def body(i, carry):
            w = wt_ref[pl.ds(src + i * step, step), :]
            w_s[pl.ds(dst + i * step, step), :] = (w if scale == 1.0 else w * scale).astype(BF16)
            return carry
        lax.fori_loop(0, n_rows // step, body, 0)

    copy_rows(0, 0, QK_GLA_W, GLA_DK ** -0.5)
    copy_rows(QK_GLA_W, QK_GLA_W, z_src - QK_GLA_W, 1.0)
    copy_rows(tail, z_src, Q_SWA_W, SWA_DH ** -0.5)
    copy_rows(tail + Q_SWA_W, z_src + Q_SWA_W, 2 * KV_SWA_W, 1.0)
    w_s[IN_COLS - Z_W:IN_COLS, :] = wt_ref[z_src:tail, :].astype(BF16)


def _inproj_kernel(x_ref, g_ref, w_ref, *refs):
    n_proj = len(IN_WIDTHS)
    later_f32, refs = refs[:N_LATER_WEIGHTS], refs[N_LATER_WEIGHTS:]
    out_refs, later_bf16, w_s = refs[:n_proj], refs[n_proj:-1], refs[-1]

    @pl.when(pl.program_id(0) == 0)
    def _():
        _stage_w_in(w_ref, w_s)

    for src, dst in zip(later_f32, later_bf16):
        dst[...] = src[...].astype(BF16)

    u = (_rms(x_ref[...]) * g_ref[...]).astype(BF16)
    off = 0
    for group in IN_DOT_GROUPS:
        width = sum(IN_WIDTHS[i] for i in group)
        y = _dot_nt(u, w_s[off:off + width, :])
        off += width
        col = 0
        for i in group:
            out_refs[i][...] = y[:, col:col + IN_WIDTHS[i]].astype(out_refs[i].dtype)
            col += IN_WIDTHS[i]


def _inproj(x2, g, w, later_weights):
    t = x2.shape[0]
    steps = t // TOKEN_BLOCK
    row = lambda width: pl.BlockSpec((TOKEN_BLOCK, width), lambda i: (i, 0))
    widths = IN_WIDTHS
    dtypes = (BF16,) * (len(IN_WIDTHS) - 1) + (F32,)
    assert len(later_weights) == N_LATER_WEIGHTS
    slices = [pl.BlockSpec((lw.shape[0] // steps, lw.shape[1]), lambda i: (i, 0))
              for lw in later_weights]
    return pl.pallas_call(
        _inproj_kernel,
        grid=(steps,),
        in_specs=[row(D_MODEL), _const_spec((1, D_MODEL)), _const_spec((IN_COLS, D_MODEL))] + slices,
        out_specs=[row(wd) for wd in widths] + slices,
        out_shape=([jax.ShapeDtypeStruct((t, wd), dt) for wd, dt in zip(widths, dtypes)]
                   + [jax.ShapeDtypeStruct(lw.shape, BF16) for lw in later_weights]),
        scratch_shapes=[pltpu.VMEM((IN_COLS, D_MODEL), BF16)],
        compiler_params=pltpu.CompilerParams(
            dimension_semantics=("arbitrary",), vmem_limit_bytes=VMEM_LIMIT),
        name="inproj",
    )(x2, g, w, *later_weights)


def _cumsum_matrix():
    r = np.arange(GLA_GROUP)[:, None]
    s = np.arange(GLA_GROUP)[None, :]
    return (((r // GLA_CHUNK) == (s // GLA_CHUNK)) & (s <= r)).astype(np.float32)


def _gla_kernel(q_ref, k_ref, v_ref, g_ref, z_ref, wgu_ref, bg_ref, nrm_ref, cm_ref, o_ref,
                la_s, hl_s, cs_s, qdec_s, kina_s, kinb_s, kst_s, am_s, ds_s, dec_s, sc_s, *, seq):
    c = GLA_CHUNK
    nc = seq // c
    grp = GLA_GROUP
    cpg = grp // c
    n_groups = seq // grp
    wgu = wgu_ref[0]
    bg = bg_ref[0]
    fwd_p = lax.broadcasted_iota(jnp.int16, (grp, LANES), 1) < HALF
    own_p = fwd_p == (pl.program_id(1) % 2 == 0)
    zero_p = jnp.zeros((grp, LANES), BF16)

    def own_head_twice(tile):
        return jnp.where(own_p, tile, pltpu.roll(tile, HALF, axis=1)).astype(F32)

    fwd_3 = lax.broadcasted_iota(jnp.int32, (cpg, c, LANES), 2) < HALF
    lane_c = lax.broadcasted_iota(jnp.int32, (c, LANES), 1)
    row_c = lax.broadcasted_iota(jnp.int32, (c, LANES), 0)
    tri = ((lane_c < HALF) & (lane_c <= row_c)) | ((lane_c >= HALF) & ((lane_c - HALF) >= row_c))
    zero_c = jnp.zeros((c, LANES), F32)

    def gate_preact(i):
        rows = _rows(i, grp)
        la_s[rows, :] = _dot(z_ref[rows, :].astype(BF16), wgu)

    def decay_logs(i):
        rows = _rows(i, grp)
        pre = la_s[rows, :] + bg
        la = ((jnp.minimum(pre, 0.0) - jnp.log(1.0 + jnp.exp(-jnp.abs(pre))))
              * (1.0 / GLA_GATE_NORM))
        la_hi = la.astype(BF16)
        la_s[rows, :] = la
        hl_s[rows, 0:LANES] = la_hi
        hl_s[rows, LANES:2 * LANES] = (la - la_hi.astype(F32)).astype(BF16)

    def chunk_cumsum(i):
        rows = _rows(i, grp)
        cs = _dot(cm_ref[...], hl_s[rows, :])
        cs_s[rows, :] = cs[:, :LANES] + cs[:, LANES:]

    def decayed_qk(i):
        rows = _rows(i, grp)
        la3 = la_s[rows, :].reshape(cpg, c, LANES)
        cum3 = cs_s[rows, :].reshape(cpg, c, LANES)
        tot = cum3[:, c - 1:c, :]
        b3 = jnp.where(fwd_3, cum3, tot - cum3 + la3)
        b = b3.reshape(grp, LANES)
        rest = jnp.exp(tot - b3).reshape(grp, LANES)
        qq = own_head_twice(q_ref[rows, :])
        kk = own_head_twice(k_ref[rows, :])
        grow = jnp.exp(-b)
        kin = (kk * grow).astype(BF16)
        qdec_s[rows, :] = (qq / grow).astype(BF16)
        kina_s[rows, :] = jnp.where(fwd_p, kin, zero_p)
        kinb_s[rows, :] = jnp.where(fwd_p, zero_p, kin)
        kst_s[rows, :] = (kk * rest).astype(BF16)
        dec_s[_rows(i, cpg)] = jnp.broadcast_to(jnp.exp(tot), (cpg, 8, LANES))

    def chunk_products(i):
        for cc in range(cpg):
            n = i * cpg + cc
            rows = _rows(n, c)
            kin_bd = jnp.concatenate([kina_s[rows, :], kinb_s[rows, :]], axis=0)
            a2 = _dot_nt(qdec_s[rows, :], kin_bd)
            am_s[rows, :] = jnp.where(tri, a2, zero_c).astype(BF16)
            ds_s[n] = _dot_tn(v_ref[rows, :], kst_s[rows, :])

    _software_pipeline(n_groups, [gate_preact, decay_logs, chunk_cumsum, decayed_qk,
                                  chunk_products], unroll=GLA_PREP_UNROLL)

    fwd_sq = lax.broadcasted_iota(jnp.int32, (LANES, LANES), 1) < HALF
    fwd_row = lax.broadcasted_iota(jnp.int32, (8, LANES), 1) < HALF

    def scan_step(i, s):
        j = nc - 1 - i
        sb = s.astype(BF16)
        sc_s[i, :, 0:HALF] = sb[:, 0:HALF]
        sc_s[j, :, HALF:LANES] = sb[:, HALF:LANES]
        dec = jnp.where(fwd_row, dec_s[i], dec_s[j])[0:1]
        return dec * s + jnp.where(fwd_sq, ds_s[i], ds_s[j])

    lax.fori_loop(0, nc, scan_step, jnp.zeros((LANES, LANES), F32), unroll=GLA_UNROLL)

    nrm = nrm_ref[...]

    def mix(i):
        for cc in range(cpg):
            n = i * cpg + cc
            rows = _rows(n, c)
            vc = v_ref[rows, :]
            vvt = jnp.concatenate([vc, vc], axis=0).T
            la_s[rows, :] = _dot_nt(jnp.concatenate([qdec_s[rows, :], am_s[rows, :]], axis=1),
                                    jnp.concatenate([sc_s[n], vvt], axis=1))

    def norm_gate(i):
        rows = _rows(i, grp)
        gate = g_ref[rows, :].astype(F32)
        half = 0.5 * gate
        swish = half + half * jnp.tanh(half)
        o = (_rms(la_s[rows, :]) * nrm) * swish
        o_ref[rows, :] = o.astype(o_ref.dtype)

    _software_pipeline(n_groups, [mix, norm_gate], unroll=GLA_OUT_UNROLL)


def _gla(q, k, v, g, z, wgu, bg, nrm, batch, seq):
    t = batch * seq
    nc = seq // GLA_CHUNK
    tile = pl.BlockSpec((seq, LANES), lambda b, h: (b, h))
    pair_tile = pl.BlockSpec((seq, LANES), lambda b, h: (b, h // 2))
    tok_bf = pltpu.VMEM((seq, LANES), BF16)
    cm = jnp.asarray(_cumsum_matrix(), dtype=BF16)
    return pl.pallas_call(
        functools.partial(_gla_kernel, seq=seq),
        grid=(batch, GLA_HEADS),
        in_specs=[pair_tile, pair_tile, tile, tile,
                  pl.BlockSpec((seq, Z_W), lambda b, h: (b, 0)),
                  pl.BlockSpec((1, Z_W, LANES), lambda b, h: (h, 0, 0)),
                  pl.BlockSpec((1, 1, LANES), lambda b, h: (h, 0, 0)),
                  pl.BlockSpec((1, LANES), lambda b, h: (0, 0)),
                  _const_spec(cm.shape)],
        out_specs=tile,
        out_shape=jax.ShapeDtypeStruct((t, V_GLA_W), BF16),
        scratch_shapes=[pltpu.VMEM((seq, LANES), F32),
                        pltpu.VMEM((seq, 2 * LANES), BF16),
                        pltpu.VMEM((seq, LANES), F32),
                        tok_bf, tok_bf, tok_bf, tok_bf, tok_bf,
                        pltpu.VMEM((nc, LANES, LANES), F32),
                        pltpu.VMEM((nc, 8, LANES), F32),
                        pltpu.VMEM((nc, LANES, LANES), BF16)],
        compiler_params=pltpu.CompilerParams(
            dimension_semantics=("arbitrary", "arbitrary"), vmem_limit_bytes=VMEM_LIMIT),
        name="gla",
    )(q, k, v, g, z, wgu, bg, nrm, cm)


def _t5_buckets(rel):
    nb = REL_BUCKETS // 2
    ret = (rel > 0).astype(np.int32) * nb
    n = np.abs(rel)
    max_exact = nb // 2
    large = max_exact + (np.log(np.maximum(n, 1).astype(np.float32) / max_exact)
                         / math.log(REL_MAX_DIST / max_exact) * (nb - max_exact)).astype(np.int32)
    large = np.minimum(large, nb - 1)
    return ret + np.where(n < max_exact, n, large)


def _bucket_map():
    w = SWA_BLOCK
    cq = np.arange(w)[:, None]
    s = np.arange(3 * w)[None, :]
    rel = s - w - cq
    return np.where(np.abs(rel) <= SWA_WINDOW, _t5_buckets(rel), -1).astype(np.int32)


def _bias_kernel(bm_ref, tab_ref, o_ref):
    w = SWA_BLOCK
    pair = pl.program_id(0)
    bm = bm_ref[...]
    key = lax.broadcasted_iota(jnp.int32, bm.shape, 1)
    masked = jnp.full(bm.shape, MASK_VALUE, F32)
    for hh in range(2):
        head = 2 * pair + hh
        acc = masked
        for bkt in range(REL_BUCKETS):
            acc = jnp.where(bm == bkt, tab_ref[bkt, head], acc)
        hcols = slice(hh * 3 * w, (hh + 1) * 3 * w)
        o_ref[0, 0, :, hcols] = jnp.where(key >= w, acc, masked)
        o_ref[1, 0, :, hcols] = acc
        o_ref[2, 0, :, hcols] = jnp.where(key < 2 * w, acc, masked)


def _bias_tables(rel_bias):
    w = SWA_BLOCK
    pairs = SWA_Q_HEADS // 2
    return pl.pallas_call(
        _bias_kernel,
        grid=(pairs,),
        in_specs=[pl.BlockSpec((w, 3 * w), lambda p: (0, 0)),
                  pl.BlockSpec(memory_space=pltpu.SMEM)],
        out_specs=pl.BlockSpec((3, 1, w, 6 * w), lambda p: (0, p, 0, 0)),
        out_shape=jax.ShapeDtypeStruct((3, pairs, w, 6 * w), F32),
        compiler_params=pltpu.CompilerParams(dimension_semantics=("arbitrary",)),
        name="swa_bias",
    )(jnp.asarray(_bucket_map()), rel_bias.astype(F32))


def _swa_kernel(q_ref, k_ref, v_ref, bias_ref, sink_ref, o_ref, ka_s, kb_s, va_s, vb_s,
                lg_s, p_s, st_s, *, seq):
    w = SWA_BLOCK
    nb = seq // w
    kvh = pl.program_id(1)
    pairs = SWA_Q_HEADS // SWA_KV_HEADS // 2
    stage_rows = 512
    lane_s = lax.broadcasted_iota(jnp.int16, (stage_rows, LANES), 1)
    lo = lane_s < HALF
    own = lo == (kvh == 0)
    for s_ref in (ka_s, kb_s, va_s, vb_s):
        zpad = jnp.zeros((w, s_ref.shape[1]), BF16)
        s_ref[0:w, :] = zpad
        s_ref[w + seq:2 * w + seq, :] = zpad

    def stage(i, carry):
        src = pl.ds(pl.multiple_of(i * stage_rows, stage_rows), stage_rows)
        dst = pl.ds(pl.multiple_of(i * stage_rows + w, w), stage_rows)
        k2 = k_ref[src, :]
        v2 = v_ref[src, :]
        kk = jnp.where(own, k2, pltpu.roll(k2, HALF, axis=1))
        vv = jnp.where(own, v2, pltpu.roll(v2, HALF, axis=1))
        zero = jnp.zeros_like(kk)
        one = jnp.ones_like(kk)
        ka_s[dst, :] = jnp.where(lo, kk, zero)
        kb_s[dst, :] = jnp.where(lo, zero, kk)
        va_s[dst, 0:LANES] = jnp.where(lo, vv, zero)
        va_s[dst, LANES:2 * LANES] = jnp.where(lo, one, zero)
        vb_s[dst, 0:LANES] = jnp.where(lo, zero, vv)
        vb_s[dst, LANES:2 * LANES] = jnp.where(lo, zero, one)
        return carry

    lax.fori_loop(0, seq // stage_rows, stage, 0)

    lo_w = lax.broadcasted_iota(jnp.int32, (w, LANES), 1) < HALF

    def logits_stage(n, variant):
        qrows = pl.ds(pl.multiple_of(n * w, w), w)
        band = pl.ds(pl.multiple_of(n * w, w), 3 * w)
        kt = jnp.concatenate([ka_s[band, :], kb_s[band, :]], axis=0)
        for pair in range(pairs):
            cols = slice(pair * LANES, (pair + 1) * LANES)
            lg_s[pair] = _dot_nt(q_ref[qrows, cols], kt) + bias_ref[variant, pair]

    def softmax_stage():
        for pair in range(pairs):
            sinks, maxes = [], []
            for hh in range(2):
                hcols = slice(hh * 3 * w, (hh + 1) * 3 * w)
                lg = lg_s[pair, :, hcols]
                sink = sink_ref[kvh * 2 * pairs + 2 * pair + hh]
                m = jnp.maximum(jnp.max(lg, axis=-1, keepdims=True), sink)
                p_s[pair, :, hcols] = jnp.exp(lg - m).astype(BF16)
                sinks.append(sink)
                maxes.append(m)
            st_s[pair] = jnp.exp(jnp.where(lo_w, sinks[0], sinks[1])
                                 - jnp.where(lo_w, maxes[0], maxes[1]))

    def output_stage(n):
        qrows = pl.ds(pl.multiple_of(n * w, w), w)
        band = pl.ds(pl.multiple_of(n * w, w), 3 * w)
        vx = jnp.concatenate([va_s[band, :], vb_s[band, :]], axis=0)
        for pair in range(pairs):
            ox = _dot(p_s[pair], vx)
            o = ox[:, :LANES] / (ox[:, LANES:] + st_s[pair])
            o_ref[qrows, pair * LANES:(pair + 1) * LANES] = o.astype(o_ref.dtype)

    logits_stage(0, 0)
    softmax_stage()
    logits_stage(1, 1)

    def steady(n, carry):
        output_stage(n - 1)
        softmax_stage()
        logits_stage(n + 1, 1)
        return carry

    lax.fori_loop(1, nb - 2, steady, 0, unroll=SWA_UNROLL)
    output_stage(nb - 3)
    softmax_stage()
    logits_stage(nb - 1, 2)
    output_stage(nb - 2)
    softmax_stage()
    output_stage(nb - 1)


def _swa(q, k, v, bias, sink, batch, seq):
    t = batch * seq
    w = SWA_BLOCK
    qw = Q_SWA_W // SWA_KV_HEADS
    pairs = SWA_Q_HEADS // SWA_KV_HEADS // 2
    staged_k = pltpu.VMEM((seq + 2 * w, LANES), BF16)
    staged_v = pltpu.VMEM((seq + 2 * w, 2 * LANES), BF16)
    return pl.pallas_call(
        functools.partial(_swa_kernel, seq=seq),
        grid=(batch, SWA_KV_HEADS),
        in_specs=[pl.BlockSpec((seq, qw), lambda b, j: (b, j)),
                  pl.BlockSpec((seq, LANES), lambda b, j: (b, 0)),
                  pl.BlockSpec((seq, LANES), lambda b, j: (b, 0)),
                  pl.BlockSpec((3, pairs, w, 6 * w), lambda b, j: (0, j, 0, 0)),
                  pl.BlockSpec(memory_space=pltpu.SMEM)],
        out_specs=pl.BlockSpec((seq, qw), lambda b, j: (b, j)),
        out_shape=jax.ShapeDtypeStruct((t, Q_SWA_W), BF16),
        scratch_shapes=[staged_k, staged_k, staged_v, staged_v,
                        pltpu.VMEM((pairs, w, 6 * w), F32),
                        pltpu.VMEM((pairs, w, 6 * w), BF16),
                        pltpu.VMEM((pairs, w, LANES), F32)],
        compiler_params=pltpu.CompilerParams(
            dimension_semantics=("arbitrary", "arbitrary"), vmem_limit_bytes=VMEM_LIMIT),
        name="swa",
    )(q, k, v, bias, sink)


def _ffn_kernel(x_ref, oa_ref, os_ref, wout_ref, gpost_ref, gpre_ref, wup_ref, wdn_ref,
                gout_ref, o_ref):
    slab = FFN_TOKEN_BLOCK // FFN_SLABS
    slabs = [slice(s * slab, (s + 1) * slab) for s in range(FFN_SLABS)]
    mixes = [_dot(oa_ref[rows, :], wout_ref[0:V_GLA_W, :])
             + _dot(os_ref[rows, :], wout_ref[V_GLA_W:V_GLA_W + Q_SWA_W, :]) for rows in slabs]
    for rows, mix in zip(slabs, mixes):
        h = x_ref[rows, :] + _rms(mix) * gpost_ref[...]
        hn = (_rms(h) * gpre_ref[...]).astype(BF16)
        acc = jnp.zeros(h.shape, F32)
        for j in range(D_FF // FF_BLOCK):
            cols = slice(j * FF_BLOCK, (j + 1) * FF_BLOCK)
            zj = jnp.maximum(_dot(hn, wup_ref[:, cols]), 0.0)
            acc = acc + _dot((zj * zj).astype(BF16), wdn_ref[cols, :])
        o_ref[rows, :] = h + _rms(acc) * gout_ref[...]


def _ffn(x2, oa, os_, wout, gpost, gpre, wup, wdn, gout):
    t = x2.shape[0]
    row = lambda width: pl.BlockSpec((FFN_TOKEN_BLOCK, width), lambda i: (i, 0))
    return pl.pallas_call(
        _ffn_kernel,
        grid=(t // FFN_TOKEN_BLOCK,),
        in_specs=[row(D_MODEL), row(V_GLA_W), row(Q_SWA_W),
                  _const_spec((V_GLA_W + Q_SWA_W, D_MODEL)),
                  _const_spec((1, D_MODEL)), _const_spec((1, D_MODEL)),
                  _const_spec((D_MODEL, D_FF)), _const_spec((D_FF, D_MODEL)),
                  _const_spec((1, D_MODEL))],
        out_specs=row(D_MODEL),
        out_shape=jax.ShapeDtypeStruct((t, D_MODEL), F32),
        compiler_params=pltpu.CompilerParams(
            dimension_semantics=("arbitrary",), vmem_limit_bytes=VMEM_LIMIT),
        name="outproj_ffn",
    )(x2, oa, os_, wout, gpost, gpre, wup, wdn, gout)


def _layer(h, norm_mix_pre, w_in, w_gu_f, b_g_f, w_gu_b, b_g_b, gla_norm, swa_sink, bias_tab,
           w_out, norm_mix_post, norm_mlp_pre, w_up, w_down, norm_mlp_post):
    batch, seq, _ = h.shape
    x2 = h.reshape(batch * seq, D_MODEL)
    row = lambda g: g.reshape(1, -1).astype(F32)

    later = [w.astype(F32) for w in (w_out, w_up, w_down)]
    qa, ka, va, ga, qs, ks, vs, z, w_out_b, w_up_b, w_down_b = _inproj(
        x2, row(norm_mix_pre), w_in.astype(F32).T, later)

    wf = w_gu_f.reshape(GLA_RANK, GLA_HEADS, GLA_DK).transpose(1, 0, 2)
    wb = w_gu_b.reshape(GLA_RANK, GLA_HEADS, GLA_DK).transpose(1, 0, 2)
    zeros = jnp.zeros_like(wf)
    wgu = jnp.concatenate([jnp.concatenate([wf, zeros], axis=2),
                           jnp.concatenate([zeros, wb], axis=2)], axis=1).astype(BF16)
    bg = jnp.concatenate([b_g_f.reshape(GLA_HEADS, 1, GLA_DK),
                          b_g_b.reshape(GLA_HEADS, 1, GLA_DK)], axis=2).astype(F32)
    o_a = _gla(qa, ka, va, ga, z, wgu, bg, row(gla_norm), batch, seq)

    o_s = _swa(qs, ks, vs, bias_tab, swa_sink.astype(F32), batch, seq)

    out = _ffn(x2, o_a, o_s, w_out_b, row(norm_mix_post), row(norm_mlp_pre), w_up_b, w_down_b,
               row(norm_mlp_post))
    return out.reshape(batch, seq, D_MODEL)


def kernel(x, norm_mix_pre, w_in, w_gate_up_fwd, b_gate_fwd, w_gate_up_bwd, b_gate_bwd, gla_norm,
           swa_sink, rel_bias, w_out, norm_mix_post, norm_mlp_pre, w_up, w_down, norm_mlp_post):
    bias_tab = _bias_tables(rel_bias)
    h = x
    for l in range(w_in.shape[0]):
        h = _layer(h, norm_mix_pre[l], w_in[l], w_gate_up_fwd[l], b_gate_fwd[l], w_gate_up_bwd[l],
                   b_gate_bwd[l], gla_norm[l], swa_sink[l], bias_tab, w_out[l], norm_mix_post[l],
                   norm_mlp_pre[l], w_up[l], w_down[l], norm_mlp_post[l])
    return h
```

```python
import functools
import math

import jax
import jax.numpy as jnp
import numpy as np
from jax import lax
from jax.experimental import pallas as pl
from jax.experimental.pallas import tpu as pltpu

F32 = jnp.float32
BF16 = jnp.bfloat16

D_MODEL = 1024
GLA_HEADS = 4
GLA_DK = 64
GLA_DV = 128
GLA_RANK = 16
GLA_GATE_NORM = 16.0
GLA_CHUNK = 64
SWA_DH = 64
SWA_Q_HEADS = 8
SWA_KV_HEADS = 2
SWA_BLOCK = 128
SWA_WINDOW = 128
REL_BUCKETS = 32
REL_MAX_DIST = 128
D_FF = 4 * D_MODEL
NORM_EPS = 1e-6
MASK_VALUE = -1e30

LANES = 128
HALF = LANES // 2

QK_GLA_W = GLA_HEADS * GLA_DK
V_GLA_W = GLA_HEADS * GLA_DV
G_GLA_W = GLA_HEADS * GLA_DV
Q_SWA_W = SWA_Q_HEADS * SWA_DH
KV_SWA_W = SWA_KV_HEADS * SWA_DH
Z_W = 2 * GLA_RANK
IN_WIDTHS = (QK_GLA_W, QK_GLA_W, V_GLA_W, G_GLA_W, Q_SWA_W, KV_SWA_W, KV_SWA_W, Z_W)
IN_COLS = sum(IN_WIDTHS)
IN_DOT_GROUPS = ((0, 1), (2,), (3,), (4,), (5, 6), (7,))
N_LATER_WEIGHTS = 3

GLA_UNROLL = 64
SWA_UNROLL = 29
GLA_GROUP = 256
GLA_PREP_UNROLL = 12
GLA_OUT_UNROLL = 15
TOKEN_BLOCK = 1024
FFN_TOKEN_BLOCK = 1024
FF_BLOCK = 1024
FFN_SLABS = 4
VMEM_LIMIT = 56 * 1024 * 1024


def _rms(x):
    return x * lax.rsqrt(jnp.mean(x * x, axis=-1, keepdims=True) + NORM_EPS)


def _dot(a, b):
    return jnp.dot(a, b, preferred_element_type=F32)


def _dot_nt(a, b):
    return lax.dot_general(a, b, (((1,), (1,)), ((), ())), preferred_element_type=F32)


def _dot_tn(a, b):
    return lax.dot_general(a, b, (((0,), (0,)), ((), ())), preferred_element_type=F32)


def _const_spec(shape):
    nd = len(shape)
    return pl.BlockSpec(shape, lambda *_: (0,) * nd, pipeline_mode=pl.Buffered(1))


def _rows(item, size):
    start = item * size
    if not isinstance(start, int):
        start = pl.multiple_of(start, size)
    return pl.ds(start, size)


def _software_pipeline(n_items, stages, unroll=1):
    depth = len(stages)

    def trip(t, static):
        for k in reversed(range(depth)):
            if not static or 0 <= t - k < n_items:
                stages[k](t - k)

    for t in range(depth - 1):
        trip(t, True)

    def steady(t, carry):
        trip(t, False)
        return carry

    lax.fori_loop(depth - 1, n_items, steady, 0, unroll=unroll)
    for t in range(n_items, n_items + depth - 1):
        trip(t, True)


def _stage_w_in(wt_ref, w_s):
    z_src = 2 * QK_GLA_W + V_GLA_W + G_GLA_W
    tail = z_src + Z_W
    step = 128

    def copy_rows(src, dst, n_rows, scale):
        def body(i, carry):
            w = wt_ref[pl.ds(src + i * step, step), :]
            w_s[pl.ds(dst + i * step, step), :] = (w if scale == 1.0 else w * scale).astype(BF16)
            return carry
        lax.fori_loop(0, n_rows // step, body, 0)

    copy_rows(0, 0, QK_GLA_W, GLA_DK ** -0.5)
    copy_rows(QK_GLA_W, QK_GLA_W, z_src - QK_GLA_W, 1.0)
    copy_rows(tail, z_src, Q_SWA_W, SWA_DH ** -0.5)
    copy_rows(tail + Q_SWA_W, z_src + Q_SWA_W, 2 * KV_SWA_W, 1.0)
    w_s[IN_COLS - Z_W:IN_COLS, :] = wt_ref[z_src:tail, :].astype(BF16)


def _inproj_kernel(x_ref, g_ref, w_ref, *refs):
    n_proj = len(IN_WIDTHS)
    later_f32, refs = refs[:N_LATER_WEIGHTS], refs[N_LATER_WEIGHTS:]
    out_refs, later_bf16, w_s = refs[:n_proj], refs[n_proj:-1], refs[-1]

    @pl.when(pl.program_id(0) == 0)
    def _():
        _stage_w_in(w_ref, w_s)

    for src, dst in zip(later_f32, later_bf16):
        dst[...] = src[...].astype(BF16)

    u = (_rms(x_ref[...]) * g_ref[...]).astype(BF16)
    off = 0
    for group in IN_DOT_GROUPS:
        width = sum(IN_WIDTHS[i] for i in group)
        y = _dot_nt(u, w_s[off:off + width, :])
        off += width
        col = 0
        for i in group:
            out_refs[i][...] = y[:, col:col + IN_WIDTHS[i]].astype(out_refs[i].dtype)
            col += IN_WIDTHS[i]


def _inproj(x2, g, w, later_weights):
    t = x2.shape[0]
    steps = t // TOKEN_BLOCK
    row = lambda width: pl.BlockSpec((TOKEN_BLOCK, width), lambda i: (i, 0))
    widths = IN_WIDTHS
    dtypes = (BF16,) * (len(IN_WIDTHS) - 1) + (F32,)
    assert len(later_weights) == N_LATER_WEIGHTS
    slices = [pl.BlockSpec((lw.shape[0] // steps, lw.shape[1]), lambda i: (i, 0))
              for lw in later_weights]
    return pl.pallas_call(
        _inproj_kernel,
        grid=(steps,),
        in_specs=[row(D_MODEL), _const_spec((1, D_MODEL)), _const_spec((IN_COLS, D_MODEL))] + slices,
        out_specs=[row(wd) for wd in widths] + slices,
        out_shape=([jax.ShapeDtypeStruct((t, wd), dt) for wd, dt in zip(widths, dtypes)]
                   + [jax.ShapeDtypeStruct(lw.shape, BF16) for lw in later_weights]),
        scratch_shapes=[pltpu.VMEM((IN_COLS, D_MODEL), BF16)],
        compiler_params=pltpu.CompilerParams(
            dimension_semantics=("arbitrary",), vmem_limit_bytes=VMEM_LIMIT),
        name="inproj",
    )(x2, g, w, *later_weights)


def _cumsum_matrix():
    r = np.arange(GLA_GROUP)[:, None]
    s = np.arange(GLA_GROUP)[None, :]
    return (((r // GLA_CHUNK) == (s // GLA_CHUNK)) & (s <= r)).astype(np.float32)


def _gla_kernel(q_ref, k_ref, v_ref, g_ref, z_ref, wgu_ref, bg_ref, nrm_ref, cm_ref, o_ref,
                la_s, hl_s, cs_s, qdec_s, kina_s, kinb_s, kst_s, am_s, ds_s, dec_s, sc_s, *, seq):
    c = GLA_CHUNK
    nc = seq // c
    grp = GLA_GROUP
    cpg = grp // c
    n_groups = seq // grp
    wgu = wgu_ref[0]
    bg = bg_ref[0]
    fwd_p = lax.broadcasted_iota(jnp.int16, (grp, LANES), 1) < HALF
    own_p = fwd_p == (pl.program_id(1) % 2 == 0)
    zero_p = jnp.zeros((grp, LANES), BF16)

    def own_head_twice(tile):
        return jnp.where(own_p, tile, pltpu.roll(tile, HALF, axis=1)).astype(F32)

    fwd_3 = lax.broadcasted_iota(jnp.int32, (cpg, c, LANES), 2) < HALF
    lane_c = lax.broadcasted_iota(jnp.int32, (c, LANES), 1)
    row_c = lax.broadcasted_iota(jnp.int32, (c, LANES), 0)
    tri = ((lane_c < HALF) & (lane_c <= row_c)) | ((lane_c >= HALF) & ((lane_c - HALF) >= row_c))
    zero_c = jnp.zeros((c, LANES), F32)

    def gate_preact(i):
        rows = _rows(i, grp)
        la_s[rows, :] = _dot(z_ref[rows, :].astype(BF16), wgu)

    def decay_logs(i):
        rows = _rows(i, grp)
        pre = la_s[rows, :] + bg
        la = ((jnp.minimum(pre, 0.0) - jnp.log(1.0 + jnp.exp(-jnp.abs(pre))))
              * (1.0 / GLA_GATE_NORM))
        la_hi = la.astype(BF16)
        la_s[rows, :] = la
        hl_s[rows, 0:LANES] = la_hi
        hl_s[rows, LANES:2 * LANES] = (la - la_hi.astype(F32)).astype(BF16)

    def chunk_cumsum(i):
        rows = _rows(i, grp)
        cs = _dot(cm_ref[...], hl_s[rows, :])
        cs_s[rows, :] = cs[:, :LANES] + cs[:, LANES:]

    def decayed_qk(i):
        rows = _rows(i, grp)
        la3 = la_s[rows, :].reshape(cpg, c, LANES)
        cum3 = cs_s[rows, :].reshape(cpg, c, LANES)
        tot = cum3[:, c - 1:c, :]
        b3 = jnp.where(fwd_3, cum3, tot - cum3 + la3)
        b = b3.reshape(grp, LANES)
        rest = jnp.exp(tot - b3).reshape(grp, LANES)
        qq = own_head_twice(q_ref[rows, :])
        kk = own_head_twice(k_ref[rows, :])
        grow = jnp.exp(-b)
        kin = (kk * grow).astype(BF16)
        qdec_s[rows, :] = (qq / grow).astype(BF16)
        kina_s[rows, :] = jnp.where(fwd_p, kin, zero_p)
        kinb_s[rows, :] = jnp.where(fwd_p, zero_p, kin)
        kst_s[rows, :] = (kk * rest).astype(BF16)
        dec_s[_rows(i, cpg)] = jnp.broadcast_to(jnp.exp(tot), (cpg, 8, LANES))

    def chunk_products(i):
        for cc in range(cpg):
            n = i * cpg + cc
            rows = _rows(n, c)
            kin_bd = jnp.concatenate([kina_s[rows, :], kinb_s[rows, :]], axis=0)
            a2 = _dot_nt(qdec_s[rows, :], kin_bd)
            am_s[rows, :] = jnp.where(tri, a2, zero_c).astype(BF16)
            ds = _dot_tn(v_ref[rows, :], kst_s[rows, :])
            ds_s[n, :, 0:HALF] = ds[:, 0:HALF]
            ds_s[nc - 1 - n, :, HALF:LANES] = ds[:, HALF:LANES]

    _software_pipeline(n_groups, [gate_preact, decay_logs, chunk_cumsum, decayed_qk,
                                  chunk_products], unroll=GLA_PREP_UNROLL)

    fwd_row = lax.broadcasted_iota(jnp.int32, (8, LANES), 1) < HALF

    def scan_step(i, s):
        j = nc - 1 - i
        sb = s.astype(BF16)
        sc_s[i, :, 0:HALF] = sb[:, 0:HALF]
        sc_s[j, :, HALF:LANES] = sb[:, HALF:LANES]
        dec = jnp.where(fwd_row, dec_s[i], dec_s[j])[0:1]
        return dec * s + ds_s[i]

    lax.fori_loop(0, nc, scan_step, jnp.zeros((LANES, LANES), F32), unroll=GLA_UNROLL)

    nrm = nrm_ref[...]

    def mix(i):
        for cc in range(cpg):
            n = i * cpg + cc
            rows = _rows(n, c)
            vc = v_ref[rows, :]
            vvt = jnp.concatenate([vc, vc], axis=0).T
            la_s[rows, :] = _dot_nt(jnp.concatenate([qdec_s[rows, :], am_s[rows, :]], axis=1),
                                    jnp.concatenate([sc_s[n], vvt], axis=1))

    def norm_gate(i):
        rows = _rows(i, grp)
        gate = g_ref[rows, :].astype(F32)
        half = 0.5 * gate
        swish = half + half * jnp.tanh(half)
        o = (_rms(la_s[rows, :]) * nrm) * swish
        o_ref[rows, :] = o.astype(o_ref.dtype)

    _software_pipeline(n_groups, [mix, norm_gate], unroll=GLA_OUT_UNROLL)


def _gla(q, k, v, g, z, wgu, bg, nrm, batch, seq):
    t = batch * seq
    nc = seq // GLA_CHUNK
    tile = pl.BlockSpec((seq, LANES), lambda b, h: (b, h))
    pair_tile = pl.BlockSpec((seq, LANES), lambda b, h: (b, h // 2))
    tok_bf = pltpu.VMEM((seq, LANES), BF16)
    cm = jnp.asarray(_cumsum_matrix(), dtype=BF16)
    return pl.pallas_call(
        functools.partial(_gla_kernel, seq=seq),
        grid=(batch, GLA_HEADS),
        in_specs=[pair_tile, pair_tile, tile, tile,
                  pl.BlockSpec((seq, Z_W), lambda b, h: (b, 0)),
                  pl.BlockSpec((1, Z_W, LANES), lambda b, h: (h, 0, 0)),
                  pl.BlockSpec((1, 1, LANES), lambda b, h: (h, 0, 0)),
                  pl.BlockSpec((1, LANES), lambda b, h: (0, 0)),
                  _const_spec(cm.shape)],
        out_specs=tile,
        out_shape=jax.ShapeDtypeStruct((t, V_GLA_W), BF16),
        scratch_shapes=[pltpu.VMEM((seq, LANES), F32),
                        pltpu.VMEM((seq, 2 * LANES), BF16),
                        pltpu.VMEM((seq, LANES), F32),
                        tok_bf, tok_bf, tok_bf, tok_bf, tok_bf,
                        pltpu.VMEM((nc, LANES, LANES), F32),
                        pltpu.VMEM((nc, 8, LANES), F32),
                        pltpu.VMEM((nc, LANES, LANES), BF16)],
        compiler_params=pltpu.CompilerParams(
            dimension_semantics=("arbitrary", "arbitrary"), vmem_limit_bytes=VMEM_LIMIT),
        name="gla",
    )(q, k, v, g, z, wgu, bg, nrm, cm)


def _t5_buckets(rel):
    nb = REL_BUCKETS // 2
    ret = (rel > 0).astype(np.int32) * nb
    n = np.abs(rel)
    max_exact = nb // 2
    large = max_exact + (np.log(np.maximum(n, 1).astype(np.float32) / max_exact)
                         / math.log(REL_MAX_DIST / max_exact) * (nb - max_exact)).astype(np.int32)
    large = np.minimum(large, nb - 1)
    return ret + np.where(n < max_exact, n, large)


def _bucket_map():
    w = SWA_BLOCK
    cq = np.arange(w)[:, None]
    s = np.arange(3 * w)[None, :]
    rel = s - w - cq
    return np.where(np.abs(rel) <= SWA_WINDOW, _t5_buckets(rel), -1).astype(np.int32)


def _bias_kernel(bm_ref, tab_ref, o_ref):
    w = SWA_BLOCK
    pair = pl.program_id(0)
    bm = bm_ref[...]
    key = lax.broadcasted_iota(jnp.int32, bm.shape, 1)
    masked = jnp.full(bm.shape, MASK_VALUE, F32)
    for hh in range(2):
        head = 2 * pair + hh
        acc = masked
        for bkt in range(REL_BUCKETS):
            acc = jnp.where(bm == bkt, tab_ref[bkt, head], acc)
        hcols = slice(hh * 3 * w, (hh + 1) * 3 * w)
        o_ref[0, 0, :, hcols] = jnp.where(key >= w, acc, masked)
        o_ref[1, 0, :, hcols] = acc
        o_ref[2, 0, :, hcols] = jnp.where(key < 2 * w, acc, masked)


def _bias_tables(rel_bias):
    w = SWA_BLOCK
    pairs = SWA_Q_HEADS // 2
    return pl.pallas_call(
        _bias_kernel,
        grid=(pairs,),
        in_specs=[pl.BlockSpec((w, 3 * w), lambda p: (0, 0)),
                  pl.BlockSpec(memory_space=pltpu.SMEM)],
        out_specs=pl.BlockSpec((3, 1, w, 6 * w), lambda p: (0, p, 0, 0)),
        out_shape=jax.ShapeDtypeStruct((3, pairs, w, 6 * w), F32),
        compiler_params=pltpu.CompilerParams(dimension_semantics=("arbitrary",)),
        name="swa_bias",
    )(jnp.asarray(_bucket_map()), rel_bias.astype(F32))


def _swa_kernel(q_ref, k_ref, v_ref, bias_ref, sink_ref, o_ref, ka_s, kb_s, va_s, vb_s,
                lg_s, p_s, st_s, *, seq):
    w = SWA_BLOCK
    nb = seq // w
    kvh = pl.program_id(1)
    pairs = SWA_Q_HEADS // SWA_KV_HEADS // 2
    stage_rows = 512
    lane_s = lax.broadcasted_iota(jnp.int16, (stage_rows, LANES), 1)
    lo = lane_s < HALF
    own = lo == (kvh == 0)
    for s_ref in (ka_s, kb_s, va_s, vb_s):
        zpad = jnp.zeros((w, s_ref.shape[1]), BF16)
        s_ref[0:w, :] = zpad
        s_ref[w + seq:2 * w + seq, :] = zpad

    def stage(i, carry):
        src = pl.ds(pl.multiple_of(i * stage_rows, stage_rows), stage_rows)
        dst = pl.ds(pl.multiple_of(i * stage_rows + w, w), stage_rows)
        k2 = k_ref[src, :]
        v2 = v_ref[src, :]
        kk = jnp.where(own, k2, pltpu.roll(k2, HALF, axis=1))
        vv = jnp.where(own, v2, pltpu.roll(v2, HALF, axis=1))
        zero = jnp.zeros_like(kk)
        one = jnp.ones_like(kk)
        ka_s[dst, :] = jnp.where(lo, kk, zero)
        kb_s[dst, :] = jnp.where(lo, zero, kk)
        va_s[dst, 0:LANES] = jnp.where(lo, vv, zero)
        va_s[dst, LANES:2 * LANES] = jnp.where(lo, one, zero)
        vb_s[dst, 0:LANES] = jnp.where(lo, zero, vv)
        vb_s[dst, LANES:2 * LANES] = jnp.where(lo, zero, one)
        return carry

    lax.fori_loop(0, seq // stage_rows, stage, 0)

    lo_w = lax.broadcasted_iota(jnp.int32, (w, LANES), 1) < HALF

    def logits_stage(n, variant):
        qrows = pl.ds(pl.multiple_of(n * w, w), w)
        band = pl.ds(pl.multiple_of(n * w, w), 3 * w)
        kt = jnp.concatenate([ka_s[band, :], kb_s[band, :]], axis=0)
        for pair in range(pairs):
            cols = slice(pair * LANES, (pair + 1) * LANES)
            lg_s[pair] = _dot_nt(q_ref[qrows, cols], kt) + bias_ref[variant, pair]

    def softmax_stage():
        for pair in range(pairs):
            sinks, maxes = [], []
            for hh in range(2):
                hcols = slice(hh * 3 * w, (hh + 1) * 3 * w)
                lg = lg_s[pair, :, hcols]
                sink = sink_ref[kvh * 2 * pairs + 2 * pair + hh]
                m = jnp.maximum(jnp.max(lg, axis=-1, keepdims=True), sink)
                p_s[pair, :, hcols] = jnp.exp(lg - m).astype(BF16)
                sinks.append(sink)
                maxes.append(m)
            st_s[pair] = jnp.exp(jnp.where(lo_w, sinks[0], sinks[1])
                                 - jnp.where(lo_w, maxes[0], maxes[1]))

    def output_stage(n):
        qrows = pl.ds(pl.multiple_of(n * w, w), w)
        band = pl.ds(pl.multiple_of(n * w, w), 3 * w)
        vx = jnp.concatenate([va_s[band, :], vb_s[band, :]], axis=0)
        for pair in range(pairs):
            ox = _dot(p_s[pair], vx)
            o = ox[:, :LANES] / (ox[:, LANES:] + st_s[pair])
            o_ref[qrows, pair * LANES:(pair + 1) * LANES] = o.astype(o_ref.dtype)

    logits_stage(0, 0)
    softmax_stage()
    logits_stage(1, 1)

    def steady(n, carry):
        output_stage(n - 1)
        softmax_stage()
        logits_stage(n + 1, 1)
        return carry

    lax.fori_loop(1, nb - 2, steady, 0, unroll=SWA_UNROLL)
    output_stage(nb - 3)
    softmax_stage()
    logits_stage(nb - 1, 2)
    output_stage(nb - 2)
    softmax_stage()
    output_stage(nb - 1)


def _swa(q, k, v, bias, sink, batch, seq):
    t = batch * seq
    w = SWA_BLOCK
    qw = Q_SWA_W // SWA_KV_HEADS
    pairs = SWA_Q_HEADS // SWA_KV_HEADS // 2
    staged_k = pltpu.VMEM((seq + 2 * w, LANES), BF16)
    staged_v = pltpu.VMEM((seq + 2 * w, 2 * LANES), BF16)
    return pl.pallas_call(
        functools.partial(_swa_kernel, seq=seq),
        grid=(batch, SWA_KV_HEADS),
        in_specs=[pl.BlockSpec((seq, qw), lambda b, j: (b, j)),
                  pl.BlockSpec((seq, LANES), lambda b, j: (b, 0)),
                  pl.BlockSpec((seq, LANES), lambda b, j: (b, 0)),
                  pl.BlockSpec((3, pairs, w, 6 * w), lambda b, j: (0, j, 0, 0)),
                  pl.BlockSpec(memory_space=pltpu.SMEM)],
        out_specs=pl.BlockSpec((seq, qw), lambda b, j: (b, j)),
        out_shape=jax.ShapeDtypeStruct((t, Q_SWA_W), BF16),
        scratch_shapes=[staged_k, staged_k, staged_v, staged_v,
                        pltpu.VMEM((pairs, w, 6 * w), F32),
                        pltpu.VMEM((pairs, w, 6 * w), BF16),
                        pltpu.VMEM((pairs, w, LANES), F32)],
        compiler_params=pltpu.CompilerParams(
            dimension_semantics=("arbitrary", "arbitrary"), vmem_limit_bytes=VMEM_LIMIT),
        name="swa",
    )(q, k, v, bias, sink)


def _ffn_kernel(x_ref, oa_ref, os_ref, wout_ref, gpost_ref, gpre_ref, wup_ref, wdn_ref,
                gout_ref, o_ref):
    slab = FFN_TOKEN_BLOCK // FFN_SLABS
    slabs = [slice(s * slab, (s + 1) * slab) for s in range(FFN_SLABS)]
    mixes = [_dot(oa_ref[rows, :], wout_ref[0:V_GLA_W, :])
             + _dot(os_ref[rows, :], wout_ref[V_GLA_W:V_GLA_W + Q_SWA_W, :]) for rows in slabs]
    for rows, mix in zip(slabs, mixes):
        h = x_ref[rows, :] + _rms(mix) * gpost_ref[...]
        hn = (_rms(h) * gpre_ref[...]).astype(BF16)
        acc = jnp.zeros(h.shape, F32)
        for j in range(D_FF // FF_BLOCK):
            cols = slice(j * FF_BLOCK, (j + 1) * FF_BLOCK)
            zj = jnp.maximum(_dot(hn, wup_ref[:, cols]), 0.0)
            acc = acc + _dot((zj * zj).astype(BF16), wdn_ref[cols, :])
        o_ref[rows, :] = h + _rms(acc) * gout_ref[...]


def _ffn(x2, oa, os_, wout, gpost, gpre, wup, wdn, gout):
    t = x2.shape[0]
    row = lambda width: pl.BlockSpec((FFN_TOKEN_BLOCK, width), lambda i: (i, 0))
    return pl.pallas_call(
        _ffn_kernel,
        grid=(t // FFN_TOKEN_BLOCK,),
        in_specs=[row(D_MODEL), row(V_GLA_W), row(Q_SWA_W),
                  _const_spec((V_GLA_W + Q_SWA_W, D_MODEL)),
                  _const_spec((1, D_MODEL)), _const_spec((1, D_MODEL)),
                  _const_spec((D_MODEL, D_FF)), _const_spec((D_FF, D_MODEL)),
                  _const_spec((1, D_MODEL))],
        out_specs=row(D_MODEL),
        out_shape=jax.ShapeDtypeStruct((t, D_MODEL), F32),
        compiler_params=pltpu.CompilerParams(
            dimension_semantics=("arbitrary",), vmem_limit_bytes=VMEM_LIMIT),
        name="outproj_ffn",
    )(x2, oa, os_, wout, gpost, gpre, wup, wdn, gout)


def _layer(h, norm_mix_pre, w_in, w_gu_f, b_g_f, w_gu_b, b_g_b, gla_norm, swa_sink, bias_tab,
           w_out, norm_mix_post, norm_mlp_pre, w_up, w_down, norm_mlp_post):
    batch, seq, _ = h.shape
    x2 = h.reshape(batch * seq, D_MODEL)
    row = lambda g: g.reshape(1, -1).astype(F32)

    later = [w.astype(F32) for w in (w_out, w_up, w_down)]
    qa, ka, va, ga, qs, ks, vs, z, w_out_b, w_up_b, w_down_b = _inproj(
        x2, row(norm_mix_pre), w_in.astype(F32).T, later)

    wf = w_gu_f.reshape(GLA_RANK, GLA_HEADS, GLA_DK).transpose(1, 0, 2)
    wb = w_gu_b.reshape(GLA_RANK, GLA_HEADS, GLA_DK).transpose(1, 0, 2)
    zeros = jnp.zeros_like(wf)
    wgu = jnp.concatenate([jnp.concatenate([wf, zeros], axis=2),
                           jnp.concatenate([zeros, wb], axis=2)], axis=1).astype(BF16)
    bg = jnp.concatenate([b_g_f.reshape(GLA_HEADS, 1, GLA_DK),
                          b_g_b.reshape(GLA_HEADS, 1, GLA_DK)], axis=2).astype(F32)
    o_a = _gla(qa, ka, va, ga, z, wgu, bg, row(gla_norm), batch, seq)

    o_s = _swa(qs, ks, vs, bias_tab, swa_sink.astype(F32), batch, seq)

    out = _ffn(x2, o_a, o_s, w_out_b, row(norm_mix_post), row(norm_mlp_pre), w_up_b, w_down_b,
               row(norm_mlp_post))
    return out.reshape(batch, seq, D_MODEL)


def kernel(x, norm_mix_pre, w_in, w_gate_up_fwd, b_gate_fwd, w_gate_up_bwd, b_gate_bwd, gla_norm,
           swa_sink, rel_bias, w_out, norm_mix_post, norm_mlp_pre, w_up, w_down, norm_mlp_post):
    bias_tab = _bias_tables(rel_bias)
    h = x
    for l in range(w_in.shape[0]):
        h = _layer(h, norm_mix_pre[l], w_in[l], w_gate_up_fwd[l], b_gate_fwd[l], w_gate_up_bwd[l],
                   b_gate_bwd[l], gla_norm[l], swa_sink[l], bias_tab, w_out[l], norm_mix_post[l],
                   norm_mlp_pre[l], w_up[l], w_down[l], norm_mlp_post[l])
    return h
```

```python
import functools
import math

import jax
import jax.numpy as jnp
import numpy as np
from jax import lax
from jax.experimental import pallas as pl
from jax.experimental.pallas import tpu as pltpu

F32 = jnp.float32
BF16 = jnp.bfloat16

D_MODEL = 1024
GLA_HEADS = 4
GLA_DK = 64
GLA_DV = 128
GLA_RANK = 16
GLA_GATE_NORM = 16.0
GLA_CHUNK = 64
SWA_DH = 64
SWA_Q_HEADS = 8
SWA_KV_HEADS = 2
SWA_BLOCK = 128
SWA_WINDOW = 128
REL_BUCKETS = 32
REL_MAX_DIST = 128
D_FF = 4 * D_MODEL
NORM_EPS = 1e-6
MASK_VALUE = -1e30

LANES = 128
HALF = LANES // 2

QK_GLA_W = GLA_HEADS * GLA_DK
V_GLA_W = GLA_HEADS * GLA_DV
G_GLA_W = GLA_HEADS * GLA_DV
Q_SWA_W = SWA_Q_HEADS * SWA_DH
KV_SWA_W = SWA_KV_HEADS * SWA_DH
Z_W = 2 * GLA_RANK
IN_WIDTHS = (QK_GLA_W, QK_GLA_W, V_GLA_W, G_GLA_W, Q_SWA_W, KV_SWA_W, KV_SWA_W, Z_W)
IN_COLS = sum(IN_WIDTHS)
IN_DOT_GROUPS = ((0, 1), (2,), (3,), (4,), (5, 6), (7,))
N_LATER_WEIGHTS = 3

GLA_UNROLL = 64
SWA_UNROLL = 29
GLA_GROUP = 256
GLA_PREP_UNROLL = 12
GLA_OUT_UNROLL = 15
TOKEN_BLOCK = 1024
FFN_TOKEN_BLOCK = 1024
FF_BLOCK = 1024
FFN_SLABS = 4
VMEM_LIMIT = 56 * 1024 * 1024


def _rms(x):
    return x * lax.rsqrt(jnp.mean(x * x, axis=-1, keepdims=True) + NORM_EPS)


def _dot(a, b):
    return jnp.dot(a, b, preferred_element_type=F32)


def _dot_nt(a, b):
    return lax.dot_general(a, b, (((1,), (1,)), ((), ())), preferred_element_type=F32)


def _dot_tn(a, b):
    return lax.dot_general(a, b, (((0,), (0,)), ((), ())), preferred_element_type=F32)


def _const_spec(shape):
    nd = len(shape)
    return pl.BlockSpec(shape, lambda *_: (0,) * nd, pipeline_mode=pl.Buffered(1))


def _rows(item, size):
    start = item * size
    if not isinstance(start, int):
        start = pl.multiple_of(start, size)
    return pl.ds(start, size)


def _software_pipeline(n_items, stages, unroll=1):
    depth = len(stages)

    def trip(t, static):
        for k in reversed(range(depth)):
            if not static or 0 <= t - k < n_items:
                stages[k](t - k)

    for t in range(depth - 1):
        trip(t, True)

    def steady(t, carry):
        trip(t, False)
        return carry

    lax.fori_loop(depth - 1, n_items, steady, 0, unroll=unroll)
    for t in range(n_items, n_items + depth - 1):
        trip(t, True)


def _stage_w_in(wt_ref, w_s):
    z_src = 2 * QK_GLA_W + V_GLA_W + G_GLA_W
    tail = z_src + Z_W
    step = 128

    def copy_rows(src, dst, n_rows, scale):
        def body(i, carry):
            w = wt_ref[pl.ds(src + i * step, step), :]
            w_s[pl.ds(dst + i * step, step), :] = (w if scale == 1.0 else w * scale).astype(BF16)
            return carry
        lax.fori_loop(0, n_rows // step, body, 0)

    copy_rows(0, 0, QK_GLA_W, GLA_DK ** -0.5)
    copy_rows(QK_GLA_W, QK_GLA_W, z_src - QK_GLA_W, 1.0)
    copy_rows(tail, z_src, Q_SWA_W, SWA_DH ** -0.5)
    copy_rows(tail + Q_SWA_W, z_src + Q_SWA_W, 2 * KV_SWA_W, 1.0)
    w_s[IN_COLS - Z_W:IN_COLS, :] = wt_ref[z_src:tail, :].astype(BF16)


def _inproj_kernel(x_ref, g_ref, w_ref, *refs):
    n_proj = len(IN_WIDTHS)
    later_f32, refs = refs[:N_LATER_WEIGHTS], refs[N_LATER_WEIGHTS:]
    out_refs, later_bf16, w_s = refs[:n_proj], refs[n_proj:-1], refs[-1]

    @pl.when(pl.program_id(0) == 0)
    def _():
        _stage_w_in(w_ref, w_s)

    for src, dst in zip(later_f32, later_bf16):
        dst[...] = src[...].astype(BF16)

    u = (_rms(x_ref[...]) * g_ref[...]).astype(BF16)
    off = 0
    for group in IN_DOT_GROUPS:
        width = sum(IN_WIDTHS[i] for i in group)
        y = _dot_nt(u, w_s[off:off + width, :])
        off += width
        col = 0
        for i in group:
            out_refs[i][...] = y[:, col:col + IN_WIDTHS[i]].astype(out_refs[i].dtype)
            col += IN_WIDTHS[i]


def _inproj(x2, g, w, later_weights):
    t = x2.shape[0]
    steps = t // TOKEN_BLOCK
    row = lambda width: pl.BlockSpec((TOKEN_BLOCK, width), lambda i: (i, 0))
    widths = IN_WIDTHS
    dtypes = (BF16,) * (len(IN_WIDTHS) - 1) + (F32,)
    assert len(later_weights) == N_LATER_WEIGHTS
    slices = [pl.BlockSpec((lw.shape[0] // steps, lw.shape[1]), lambda i: (i, 0))
              for lw in later_weights]
    return pl.pallas_call(
        _inproj_kernel,
        grid=(steps,),
        in_specs=[row(D_MODEL), _const_spec((1, D_MODEL)), _const_spec((IN_COLS, D_MODEL))] + slices,
        out_specs=[row(wd) for wd in widths] + slices,
        out_shape=([jax.ShapeDtypeStruct((t, wd), dt) for wd, dt in zip(widths, dtypes)]
                   + [jax.ShapeDtypeStruct(lw.shape, BF16) for lw in later_weights]),
        scratch_shapes=[pltpu.VMEM((IN_COLS, D_MODEL), BF16)],
        compiler_params=pltpu.CompilerParams(
            dimension_semantics=("arbitrary",), vmem_limit_bytes=VMEM_LIMIT),
        name="inproj",
    )(x2, g, w, *later_weights)


def _cumsum_matrix():
    r = np.arange(GLA_GROUP)[:, None]
    s = np.arange(GLA_GROUP)[None, :]
    return (((r // GLA_CHUNK) == (s // GLA_CHUNK)) & (s <= r)).astype(np.float32)


def _gla_kernel(q_ref, k_ref, v_ref, g_ref, z_ref, wgu_ref, bg_ref, nrm_ref, cm_ref, o_ref,
                la_s, hl_s, cs_s, qdec_s, kina_s, kinb_s, kst_s, am_s, ds_s, dec_s, sc_s, *, seq):
    c = GLA_CHUNK
    nc = seq // c
    grp = GLA_GROUP
    cpg = grp // c
    n_groups = seq // grp
    wgu = wgu_ref[0]
    bg = bg_ref[0]
    fwd_p = lax.broadcasted_iota(jnp.int16, (grp, LANES), 1) < HALF
    own_p = fwd_p == (pl.program_id(1) % 2 == 0)
    zero_p = jnp.zeros((grp, LANES), BF16)

    def own_head_twice(tile):
        return jnp.where(own_p, tile, pltpu.roll(tile, HALF, axis=1)).astype(F32)

    fwd_3 = lax.broadcasted_iota(jnp.int32, (cpg, c, LANES), 2) < HALF
    lane_c = lax.broadcasted_iota(jnp.int32, (c, LANES), 1)
    row_c = lax.broadcasted_iota(jnp.int32, (c, LANES), 0)
    tri = ((lane_c < HALF) & (lane_c <= row_c)) | ((lane_c >= HALF) & ((lane_c - HALF) >= row_c))
    zero_c = jnp.zeros((c, LANES), F32)

    def gate_preact(i):
        rows = _rows(i, grp)
        la_s[rows, :] = _dot(z_ref[rows, :].astype(BF16), wgu)

    def decay_logs(i):
        rows = _rows(i, grp)
        pre = la_s[rows, :] + bg
        la = ((jnp.minimum(pre, 0.0) - jnp.log(1.0 + jnp.exp(-jnp.abs(pre))))
              * (1.0 / GLA_GATE_NORM))
        la_hi = la.astype(BF16)
        la_s[rows, :] = la
        hl_s[rows, 0:LANES] = la_hi
        hl_s[rows, LANES:2 * LANES] = (la - la_hi.astype(F32)).astype(BF16)

    def chunk_cumsum(i):
        rows = _rows(i, grp)
        cs = _dot(cm_ref[...], hl_s[rows, :])
        cs_s[rows, :] = cs[:, :LANES] + cs[:, LANES:]

    def decayed_qk(i):
        rows = _rows(i, grp)
        la3 = la_s[rows, :].reshape(cpg, c, LANES)
        cum3 = cs_s[rows, :].reshape(cpg, c, LANES)
        tot = cum3[:, c - 1:c, :]
        b3 = jnp.where(fwd_3, cum3, tot - cum3 + la3)
        b = b3.reshape(grp, LANES)
        rest = jnp.exp(tot - b3).reshape(grp, LANES)
        qq = own_head_twice(q_ref[rows, :])
        kk = own_head_twice(k_ref[rows, :])
        grow = jnp.exp(-b)
        kin = (kk * grow).astype(BF16)
        qdec_s[rows, :] = (qq / grow).astype(BF16)
        kina_s[rows, :] = jnp.where(fwd_p, kin, zero_p)
        kinb_s[rows, :] = jnp.where(fwd_p, zero_p, kin)
        kst_s[rows, :] = (kk * rest).astype(BF16)
        dec_s[_rows(i, cpg)] = jnp.broadcast_to(jnp.exp(tot), (cpg, 8, LANES))

    def chunk_products(i):
        for cc in range(cpg):
            n = i * cpg + cc
            rows = _rows(n, c)
            kin_bd = jnp.concatenate([kina_s[rows, :], kinb_s[rows, :]], axis=0)
            a2 = _dot_nt(qdec_s[rows, :], kin_bd)
            am_s[rows, :] = jnp.where(tri, a2, zero_c).astype(BF16)
            ds = _dot_tn(v_ref[rows, :], kst_s[rows, :])
            ds_s[n, :, 0:HALF] = ds[:, 0:HALF]
            ds_s[nc - 1 - n, :, HALF:LANES] = ds[:, HALF:LANES]

    _software_pipeline(n_groups, [gate_preact, decay_logs, chunk_cumsum, decayed_qk,
                                  chunk_products], unroll=GLA_PREP_UNROLL)

    fwd_row = lax.broadcasted_iota(jnp.int32, (8, LANES), 1) < HALF

    def scan_step(i, s):
        j = nc - 1 - i
        sb = s.astype(BF16)
        sc_s[i, :, 0:HALF] = sb[:, 0:HALF]
        sc_s[j, :, HALF:LANES] = sb[:, HALF:LANES]
        dec = jnp.where(fwd_row, dec_s[i], dec_s[j])[0:1]
        return dec * s + ds_s[i]

    lax.fori_loop(0, nc, scan_step, jnp.zeros((LANES, LANES), F32), unroll=GLA_UNROLL)

    nrm = nrm_ref[...]

    def mix(i):
        for cc in range(cpg):
            n = i * cpg + cc
            rows = _rows(n, c)
            vc = v_ref[rows, :]
            vvt = jnp.concatenate([vc, vc], axis=0).T
            la_s[rows, :] = _dot_nt(jnp.concatenate([qdec_s[rows, :], am_s[rows, :]], axis=1),
                                    jnp.concatenate([sc_s[n], vvt], axis=1))

    def norm_gate(i):
        rows = _rows(i, grp)
        gate = g_ref[rows, :].astype(F32)
        half = 0.5 * gate
        swish = half + half * jnp.tanh(half)
        o = (_rms(la_s[rows, :]) * nrm) * swish
        o_ref[rows, :] = o.astype(o_ref.dtype)

    _software_pipeline(n_groups, [mix, norm_gate], unroll=GLA_OUT_UNROLL)


def _gla(q, k, v, g, z, wgu, bg, nrm, batch, seq):
    t = batch * seq
    nc = seq // GLA_CHUNK
    tile = pl.BlockSpec((seq, LANES), lambda b, h: (b, h))
    pair_tile = pl.BlockSpec((seq, LANES), lambda b, h: (b, h // 2))
    tok_bf = pltpu.VMEM((seq, LANES), BF16)
    cm = jnp.asarray(_cumsum_matrix(), dtype=BF16)
    return pl.pallas_call(
        functools.partial(_gla_kernel, seq=seq),
        grid=(batch, GLA_HEADS),
        in_specs=[pair_tile, pair_tile, tile, tile,
                  pl.BlockSpec((seq, Z_W), lambda b, h: (b, 0)),
                  pl.BlockSpec((1, Z_W, LANES), lambda b, h: (h, 0, 0)),
                  pl.BlockSpec((1, 1, LANES), lambda b, h: (h, 0, 0)),
                  pl.BlockSpec((1, LANES), lambda b, h: (0, 0)),
                  _const_spec(cm.shape)],
        out_specs=tile,
        out_shape=jax.ShapeDtypeStruct((t, V_GLA_W), BF16),
        scratch_shapes=[pltpu.VMEM((seq, LANES), F32),
                        pltpu.VMEM((seq, 2 * LANES), BF16),
                        pltpu.VMEM((seq, LANES), F32),
                        tok_bf, tok_bf, tok_bf, tok_bf, tok_bf,
                        pltpu.VMEM((nc, LANES, LANES), F32),
                        pltpu.VMEM((nc, 8, LANES), F32),
                        pltpu.VMEM((nc, LANES, LANES), BF16)],
        compiler_params=pltpu.CompilerParams(
            dimension_semantics=("arbitrary", "arbitrary"), vmem_limit_bytes=VMEM_LIMIT),
        name="gla",
    )(q, k, v, g, z, wgu, bg, nrm, cm)


def _t5_buckets(rel):
    nb = REL_BUCKETS // 2
    ret = (rel > 0).astype(np.int32) * nb
    n = np.abs(rel)
    max_exact = nb // 2
    large = max_exact + (np.log(np.maximum(n, 1).astype(np.float32) / max_exact)
                         / math.log(REL_MAX_DIST / max_exact) * (nb - max_exact)).astype(np.int32)
    large = np.minimum(large, nb - 1)
    return ret + np.where(n < max_exact, n, large)


def _bucket_map():
    w = SWA_BLOCK
    cq = np.arange(w)[:, None]
    s = np.arange(3 * w)[None, :]
    rel = s - w - cq
    return np.where(np.abs(rel) <= SWA_WINDOW, _t5_buckets(rel), -1).astype(np.int32)


def _build_bias_tables(bm_ref, tab_ref, bias_s, kvh, pairs):
    w = SWA_BLOCK
    bm = bm_ref[...]
    key = lax.broadcasted_iota(jnp.int32, bm.shape, 1)
    masked = jnp.full(bm.shape, MASK_VALUE, F32)
    for pair in range(pairs):
        for hh in range(2):
            head = kvh * 2 * pairs + 2 * pair + hh
            acc = masked
            for bkt in range(REL_BUCKETS):
                acc = jnp.where(bm == bkt, tab_ref[bkt, head], acc)
            hcols = slice(hh * 3 * w, (hh + 1) * 3 * w)
            bias_s[kvh, 0, pair, :, hcols] = jnp.where(key >= w, acc, masked)
            bias_s[kvh, 1, pair, :, hcols] = acc
            bias_s[kvh, 2, pair, :, hcols] = jnp.where(key < 2 * w, acc, masked)


def _swa_kernel(q_ref, k_ref, v_ref, bm_ref, tab_ref, sink_ref, o_ref, ka_s, kb_s, va_s, vb_s,
                lg_s, p_s, st_s, bias_s, *, seq):
    w = SWA_BLOCK
    nb = seq // w
    kvh = pl.program_id(1)
    pairs = SWA_Q_HEADS // SWA_KV_HEADS // 2

    @pl.when(pl.program_id(0) == 0)
    def _():
        _build_bias_tables(bm_ref, tab_ref, bias_s, kvh, pairs)

    bias_ref = bias_s.at[kvh]
    stage_rows = 512
    lane_s = lax.broadcasted_iota(jnp.int16, (stage_rows, LANES), 1)
    lo = lane_s < HALF
    own = lo == (kvh == 0)
    for s_ref in (ka_s, kb_s, va_s, vb_s):
        zpad = jnp.zeros((w, s_ref.shape[1]), BF16)
        s_ref[0:w, :] = zpad
        s_ref[w + seq:2 * w + seq, :] = zpad

    def stage(i, carry):
        src = pl.ds(pl.multiple_of(i * stage_rows, stage_rows), stage_rows)
        dst = pl.ds(pl.multiple_of(i * stage_rows + w, w), stage_rows)
        k2 = k_ref[src, :]
        v2 = v_ref[src, :]
        kk = jnp.where(own, k2, pltpu.roll(k2, HALF, axis=1))
        vv = jnp.where(own, v2, pltpu.roll(v2, HALF, axis=1))
        zero = jnp.zeros_like(kk)
        one = jnp.ones_like(kk)
        ka_s[dst, :] = jnp.where(lo, kk, zero)
        kb_s[dst, :] = jnp.where(lo, zero, kk)
        va_s[dst, 0:LANES] = jnp.where(lo, vv, zero)
        va_s[dst, LANES:2 * LANES] = jnp.where(lo, one, zero)
        vb_s[dst, 0:LANES] = jnp.where(lo, zero, vv)
        vb_s[dst, LANES:2 * LANES] = jnp.where(lo, zero, one)
        return carry

    lax.fori_loop(0, seq // stage_rows, stage, 0)

    lo_w = lax.broadcasted_iota(jnp.int32, (w, LANES), 1) < HALF

    def logits_stage(n, variant):
        qrows = pl.ds(pl.multiple_of(n * w, w), w)
        band = pl.ds(pl.multiple_of(n * w, w), 3 * w)
        kt = jnp.concatenate([ka_s[band, :], kb_s[band, :]], axis=0)
        for pair in range(pairs):
            cols = slice(pair * LANES, (pair + 1) * LANES)
            lg_s[pair] = _dot_nt(q_ref[qrows, cols], kt) + bias_ref[variant, pair]

    def softmax_stage():
        for pair in range(pairs):
            sinks, maxes = [], []
            for hh in range(2):
                hcols = slice(hh * 3 * w, (hh + 1) * 3 * w)
                lg = lg_s[pair, :, hcols]
                sink = sink_ref[kvh * 2 * pairs + 2 * pair + hh]
                m = jnp.maximum(jnp.max(lg, axis=-1, keepdims=True), sink)
                p_s[pair, :, hcols] = jnp.exp(lg - m).astype(BF16)
                sinks.append(sink)
                maxes.append(m)
            st_s[pair] = jnp.exp(jnp.where(lo_w, sinks[0], sinks[1])
                                 - jnp.where(lo_w, maxes[0], maxes[1]))

    def output_stage(n):
        qrows = pl.ds(pl.multiple_of(n * w, w), w)
        band = pl.ds(pl.multiple_of(n * w, w), 3 * w)
        vx = jnp.concatenate([va_s[band, :], vb_s[band, :]], axis=0)
        for pair in range(pairs):
            ox = _dot(p_s[pair], vx)
            o = ox[:, :LANES] / (ox[:, LANES:] + st_s[pair])
            o_ref[qrows, pair * LANES:(pair + 1) * LANES] = o.astype(o_ref.dtype)

    logits_stage(0, 0)
    softmax_stage()
    logits_stage(1, 1)

    def steady(n, carry):
        output_stage(n - 1)
        softmax_stage()
        logits_stage(n + 1, 1)
        return carry

    lax.fori_loop(1, nb - 2, steady, 0, unroll=SWA_UNROLL)
    output_stage(nb - 3)
    softmax_stage()
    logits_stage(nb - 1, 2)
    output_stage(nb - 2)
    softmax_stage()
    output_stage(nb - 1)


def _swa(q, k, v, rel_bias, sink, batch, seq):
    t = batch * seq
    w = SWA_BLOCK
    qw = Q_SWA_W // SWA_KV_HEADS
    pairs = SWA_Q_HEADS // SWA_KV_HEADS // 2
    staged_k = pltpu.VMEM((seq + 2 * w, LANES), BF16)
    staged_v = pltpu.VMEM((seq + 2 * w, 2 * LANES), BF16)
    return pl.pallas_call(
        functools.partial(_swa_kernel, seq=seq),
        grid=(batch, SWA_KV_HEADS),
        in_specs=[pl.BlockSpec((seq, qw), lambda b, j: (b, j)),
                  pl.BlockSpec((seq, LANES), lambda b, j: (b, 0)),
                  pl.BlockSpec((seq, LANES), lambda b, j: (b, 0)),
                  _const_spec((w, 3 * w)),
                  pl.BlockSpec(memory_space=pltpu.SMEM),
                  pl.BlockSpec(memory_space=pltpu.SMEM)],
        out_specs=pl.BlockSpec((seq, qw), lambda b, j: (b, j)),
        out_shape=jax.ShapeDtypeStruct((t, Q_SWA_W), BF16),
        scratch_shapes=[staged_k, staged_k, staged_v, staged_v,
                        pltpu.VMEM((pairs, w, 6 * w), F32),
                        pltpu.VMEM((pairs, w, 6 * w), BF16),
                        pltpu.VMEM((pairs, w, LANES), F32),
                        pltpu.VMEM((SWA_KV_HEADS, 3, pairs, w, 6 * w), F32)],
        compiler_params=pltpu.CompilerParams(
            dimension_semantics=("arbitrary", "arbitrary"), vmem_limit_bytes=VMEM_LIMIT),
        name="swa",
    )(q, k, v, jnp.asarray(_bucket_map()), rel_bias.astype(F32), sink)


def _ffn_kernel(x_ref, oa_ref, os_ref, wout_ref, gpost_ref, gpre_ref, wup_ref, wdn_ref,
                gout_ref, o_ref):
    slab = FFN_TOKEN_BLOCK // FFN_SLABS
    slabs = [slice(s * slab, (s + 1) * slab) for s in range(FFN_SLABS)]
    mixes = [_dot(oa_ref[rows, :], wout_ref[0:V_GLA_W, :])
             + _dot(os_ref[rows, :], wout_ref[V_GLA_W:V_GLA_W + Q_SWA_W, :]) for rows in slabs]
    for rows, mix in zip(slabs, mixes):
        h = x_ref[rows, :] + _rms(mix) * gpost_ref[...]
        hn = (_rms(h) * gpre_ref[...]).astype(BF16)
        acc = jnp.zeros(h.shape, F32)
        for j in range(D_FF // FF_BLOCK):
            cols = slice(j * FF_BLOCK, (j + 1) * FF_BLOCK)
            zj = jnp.maximum(_dot(hn, wup_ref[:, cols]), 0.0)
            acc = acc + _dot((zj * zj).astype(BF16), wdn_ref[cols, :])
        o_ref[rows, :] = h + _rms(acc) * gout_ref[...]


def _ffn(x2, oa, os_, wout, gpost, gpre, wup, wdn, gout):
    t = x2.shape[0]
    row = lambda width: pl.BlockSpec((FFN_TOKEN_BLOCK, width), lambda i: (i, 0))
    return pl.pallas_call(
        _ffn_kernel,
        grid=(t // FFN_TOKEN_BLOCK,),
        in_specs=[row(D_MODEL), row(V_GLA_W), row(Q_SWA_W),
                  _const_spec((V_GLA_W + Q_SWA_W, D_MODEL)),
                  _const_spec((1, D_MODEL)), _const_spec((1, D_MODEL)),
                  _const_spec((D_MODEL, D_FF)), _const_spec((D_FF, D_MODEL)),
                  _const_spec((1, D_MODEL))],
        out_specs=row(D_MODEL),
        out_shape=jax.ShapeDtypeStruct((t, D_MODEL), F32),
        compiler_params=pltpu.CompilerParams(
            dimension_semantics=("arbitrary",), vmem_limit_bytes=VMEM_LIMIT),
        name="outproj_ffn",
    )(x2, oa, os_, wout, gpost, gpre, wup, wdn, gout)


def _layer(h, norm_mix_pre, w_in, w_gu_f, b_g_f, w_gu_b, b_g_b, gla_norm, swa_sink, rel_bias,
           w_out, norm_mix_post, norm_mlp_pre, w_up, w_down, norm_mlp_post):
    batch, seq, _ = h.shape
    x2 = h.reshape(batch * seq, D_MODEL)
    row = lambda g: g.reshape(1, -1).astype(F32)

    later = [w.astype(F32) for w in (w_out, w_up, w_down)]
    qa, ka, va, ga, qs, ks, vs, z, w_out_b, w_up_b, w_down_b = _inproj(
        x2, row(norm_mix_pre), w_in.astype(F32).T, later)

    wf = w_gu_f.reshape(GLA_RANK, GLA_HEADS, GLA_DK).transpose(1, 0, 2)
    wb = w_gu_b.reshape(GLA_RANK, GLA_HEADS, GLA_DK).transpose(1, 0, 2)
    zeros = jnp.zeros_like(wf)
    wgu = jnp.concatenate([jnp.concatenate([wf, zeros], axis=2),
                           jnp.concatenate([zeros, wb], axis=2)], axis=1).astype(BF16)
    bg = jnp.concatenate([b_g_f.reshape(GLA_HEADS, 1, GLA_DK),
                          b_g_b.reshape(GLA_HEADS, 1, GLA_DK)], axis=2).astype(F32)
    o_a = _gla(qa, ka, va, ga, z, wgu, bg, row(gla_norm), batch, seq)

    o_s = _swa(qs, ks, vs, rel_bias, swa_sink.astype(F32), batch, seq)

    out = _ffn(x2, o_a, o_s, w_out_b, row(norm_mix_post), row(norm_mlp_pre), w_up_b, w_down_b,
               row(norm_mlp_post))
    return out.reshape(batch, seq, D_MODEL)


def kernel(x, norm_mix_pre, w_in, w_gate_up_fwd, b_gate_fwd, w_gate_up_bwd, b_gate_bwd, gla_norm,
           swa_sink, rel_bias, w_out, norm_mix_post, norm_mlp_pre, w_up, w_down, norm_mlp_post):
    h = x
    for l in range(w_in.shape[0]):
        h = _layer(h, norm_mix_pre[l], w_in[l], w_gate_up_fwd[l], b_gate_fwd[l], w_gate_up_bwd[l],
                   b_gate_bwd[l], gla_norm[l], swa_sink[l], rel_bias, w_out[l], norm_mix_post[l],
                   norm_mlp_pre[l], w_up[l], w_down[l], norm_mlp_post[l])
    return h
```

```python
import functools
import math

import jax
import jax.numpy as jnp
import numpy as np
from jax import lax
from jax.experimental import pallas as pl
from jax.experimental.pallas import tpu as pltpu

F32 = jnp.float32
BF16 = jnp.bfloat16

D_MODEL = 1024
GLA_HEADS = 4
GLA_DK = 64
GLA_DV = 128
GLA_RANK = 16
GLA_GATE_NORM = 16.0
GLA_CHUNK = 64
SWA_DH = 64
SWA_Q_HEADS = 8
SWA_KV_HEADS = 2
SWA_BLOCK = 128
SWA_WINDOW = 128
REL_BUCKETS = 32
REL_MAX_DIST = 128
D_FF = 4 * D_MODEL
NORM_EPS = 1e-6
MASK_VALUE = -1e30

LANES = 128
HALF = LANES // 2

QK_GLA_W = GLA_HEADS * GLA_DK
V_GLA_W = GLA_HEADS * GLA_DV
G_GLA_W = GLA_HEADS * GLA_DV
Q_SWA_W = SWA_Q_HEADS * SWA_DH
KV_SWA_W = SWA_KV_HEADS * SWA_DH
Z_W = 2 * GLA_RANK
IN_WIDTHS = (QK_GLA_W, QK_GLA_W, V_GLA_W, G_GLA_W, Q_SWA_W, KV_SWA_W, KV_SWA_W, Z_W)
IN_COLS = sum(IN_WIDTHS)
IN_DOT_GROUPS = ((0, 1), (2,), (3,), (4,), (5, 6), (7,))
N_LATER_WEIGHTS = 3

GLA_UNROLL = 64
SWA_UNROLL = 29
GLA_GROUP = 256
GLA_PREP_UNROLL = 12
GLA_OUT_UNROLL = 15
TOKEN_BLOCK = 1024
FFN_TOKEN_BLOCK = 1024
FF_BLOCK = 1024
FFN_SLABS = 4
VMEM_LIMIT = 56 * 1024 * 1024


def _rms(x):
    return x * lax.rsqrt(jnp.mean(x * x, axis=-1, keepdims=True) + NORM_EPS)


def _dot(a, b):
    return jnp.dot(a, b, preferred_element_type=F32)


def _dot_nt(a, b):
    return lax.dot_general(a, b, (((1,), (1,)), ((), ())), preferred_element_type=F32)


def _dot_tn(a, b):
    return lax.dot_general(a, b, (((0,), (0,)), ((), ())), preferred_element_type=F32)


def _const_spec(shape):
    nd = len(shape)
    return pl.BlockSpec(shape, lambda *_: (0,) * nd, pipeline_mode=pl.Buffered(1))


def _rows(item, size):
    start = item * size
    if not isinstance(start, int):
        start = pl.multiple_of(start, size)
    return pl.ds(start, size)


def _software_pipeline(n_items, stages, unroll=1):
    depth = len(stages)

    def trip(t, static):
        for k in reversed(range(depth)):
            if not static or 0 <= t - k < n_items:
                stages[k](t - k)

    for t in range(depth - 1):
        trip(t, True)

    def steady(t, carry):
        trip(t, False)
        return carry

    lax.fori_loop(depth - 1, n_items, steady, 0, unroll=unroll)
    for t in range(n_items, n_items + depth - 1):
        trip(t, True)


def _stage_w_in(wt_ref, w_s):
    z_src = 2 * QK_GLA_W + V_GLA_W + G_GLA_W
    tail = z_src + Z_W
    step = 128

    def copy_rows(src, dst, n_rows, scale):
        def body(i, carry):
            w = wt_ref[pl.ds(src + i * step, step), :]
            w_s[pl.ds(dst + i * step, step), :] = (w if scale == 1.0 else w * scale).astype(BF16)
            return carry
        lax.fori_loop(0, n_rows // step, body, 0)

    copy_rows(0, 0, QK_GLA_W, GLA_DK ** -0.5)
    copy_rows(QK_GLA_W, QK_GLA_W, z_src - QK_GLA_W, 1.0)
    copy_rows(tail, z_src, Q_SWA_W, SWA_DH ** -0.5)
    copy_rows(tail + Q_SWA_W, z_src + Q_SWA_W, 2 * KV_SWA_W, 1.0)
    w_s[IN_COLS - Z_W:IN_COLS, :] = wt_ref[z_src:tail, :].astype(BF16)


def _inproj_kernel(x_ref, g_ref, w_ref, *refs):
    n_proj = len(IN_WIDTHS)
    later_f32, refs = refs[:N_LATER_WEIGHTS], refs[N_LATER_WEIGHTS:]
    out_refs, later_bf16, w_s = refs[:n_proj], refs[n_proj:-1], refs[-1]

    @pl.when(pl.program_id(0) == 0)
    def _():
        _stage_w_in(w_ref, w_s)

    for src, dst in zip(later_f32, later_bf16):
        dst[...] = src[...].astype(BF16)

    u = (_rms(x_ref[...]) * g_ref[...]).astype(BF16)
    off = 0
    for group in IN_DOT_GROUPS:
        width = sum(IN_WIDTHS[i] for i in group)
        y = _dot_nt(u, w_s[off:off + width, :])
        off += width
        col = 0
        for i in group:
            out_refs[i][...] = y[:, col:col + IN_WIDTHS[i]].astype(out_refs[i].dtype)
            col += IN_WIDTHS[i]


def _inproj(x2, g, w, later_weights):
    t = x2.shape[0]
    steps = t // TOKEN_BLOCK
    row = lambda width: pl.BlockSpec((TOKEN_BLOCK, width), lambda i: (i, 0))
    widths = IN_WIDTHS
    dtypes = (BF16,) * (len(IN_WIDTHS) - 1) + (F32,)
    assert len(later_weights) == N_LATER_WEIGHTS
    slices = [pl.BlockSpec((lw.shape[0] // steps, lw.shape[1]), lambda i: (i, 0))
              for lw in later_weights]
    return pl.pallas_call(
        _inproj_kernel,
        grid=(steps,),
        in_specs=[row(D_MODEL), _const_spec((1, D_MODEL)), _const_spec((IN_COLS, D_MODEL))] + slices,
        out_specs=[row(wd) for wd in widths] + slices,
        out_shape=([jax.ShapeDtypeStruct((t, wd), dt) for wd, dt in zip(widths, dtypes)]
                   + [jax.ShapeDtypeStruct(lw.shape, BF16) for lw in later_weights]),
        scratch_shapes=[pltpu.VMEM((IN_COLS, D_MODEL), BF16)],
        compiler_params=pltpu.CompilerParams(
            dimension_semantics=("arbitrary",), vmem_limit_bytes=VMEM_LIMIT),
        name="inproj",
    )(x2, g, w, *later_weights)


def _cumsum_matrix():
    r = np.arange(GLA_GROUP)[:, None]
    s = np.arange(GLA_GROUP)[None, :]
    return (((r // GLA_CHUNK) == (s // GLA_CHUNK)) & (s <= r)).astype(np.float32)


def _gla_kernel(q_ref, k_ref, v_ref, g_ref, z_ref, wgu_ref, bg_ref, nrm_ref, cm_ref, o_ref,
                la_s, hl_s, cs_s, qdec_s, kina_s, kinb_s, kst_s, am_s, ds_s, dec_s, sc_s, *, seq):
    c = GLA_CHUNK
    nc = seq // c
    grp = GLA_GROUP
    cpg = grp // c
    n_groups = seq // grp
    wgu = wgu_ref[0]
    bg = bg_ref[0]
    fwd_p = lax.broadcasted_iota(jnp.int16, (grp, LANES), 1) < HALF
    own_p = fwd_p == (pl.program_id(1) % 2 == 0)
    zero_p = jnp.zeros((grp, LANES), BF16)

    def own_head_twice(tile):
        return jnp.where(own_p, tile, pltpu.roll(tile, HALF, axis=1)).astype(F32)

    fwd_3 = lax.broadcasted_iota(jnp.int32, (cpg, c, LANES), 2) < HALF
    lane_c = lax.broadcasted_iota(jnp.int32, (c, LANES), 1)
    row_c = lax.broadcasted_iota(jnp.int32, (c, LANES), 0)
    tri = ((lane_c < HALF) & (lane_c <= row_c)) | ((lane_c >= HALF) & ((lane_c - HALF) >= row_c))
    zero_c = jnp.zeros((c, LANES), F32)

    def gate_preact(i):
        rows = _rows(i, grp)
        la_s[rows, :] = _dot(z_ref[rows, :].astype(BF16), wgu)

    def decay_logs(i):
        rows = _rows(i, grp)
        pre = la_s[rows, :] + bg
        la = ((jnp.minimum(pre, 0.0) - jnp.log(1.0 + jnp.exp(-jnp.abs(pre))))
              * (1.0 / GLA_GATE_NORM))
        la_hi = la.astype(BF16)
        la_s[rows, :] = la
        hl_s[rows, 0:LANES] = la_hi
        hl_s[rows, LANES:2 * LANES] = (la - la_hi.astype(F32)).astype(BF16)

    def chunk_cumsum(i):
        rows = _rows(i, grp)
        cs = _dot(cm_ref[...], hl_s[rows, :])
        cs_s[rows, :] = cs[:, :LANES] + cs[:, LANES:]

    def decayed_qk(i):
        rows = _rows(i, grp)
        la3 = la_s[rows, :].reshape(cpg, c, LANES)
        cum3 = cs_s[rows, :].reshape(cpg, c, LANES)
        tot = cum3[:, c - 1:c, :]
        b3 = jnp.where(fwd_3, cum3, tot - cum3 + la3)
        b = b3.reshape(grp, LANES)
        rest = jnp.exp(tot - b3).reshape(grp, LANES)
        qq = own_head_twice(q_ref[rows, :])
        kk = own_head_twice(k_ref[rows, :])
        grow = jnp.exp(-b)
        kin = (kk * grow).astype(BF16)
        qdec_s[rows, :] = (qq / grow).astype(BF16)
        kina_s[rows, :] = jnp.where(fwd_p, kin, zero_p)
        kinb_s[rows, :] = jnp.where(fwd_p, zero_p, kin)
        kst_s[rows, :] = (kk * rest).astype(BF16)
        dec = jnp.broadcast_to(jnp.exp(tot), (cpg, 8, LANES))
        for cc in range(cpg):
            n = i * cpg + cc
            dec_s[n, :, 0:HALF] = dec[cc, :, 0:HALF]
            dec_s[nc - 1 - n, :, HALF:LANES] = dec[cc, :, HALF:LANES]

    def chunk_products(i):
        for cc in range(cpg):
            n = i * cpg + cc
            rows = _rows(n, c)
            kin_bd = jnp.concatenate([kina_s[rows, :], kinb_s[rows, :]], axis=0)
            a2 = _dot_nt(qdec_s[rows, :], kin_bd)
            am_s[rows, :] = jnp.where(tri, a2, zero_c).astype(BF16)
            ds = _dot_tn(v_ref[rows, :], kst_s[rows, :])
            ds_s[n, :, 0:HALF] = ds[:, 0:HALF]
            ds_s[nc - 1 - n, :, HALF:LANES] = ds[:, HALF:LANES]

    _software_pipeline(n_groups, [gate_preact, decay_logs, chunk_cumsum, decayed_qk,
                                  chunk_products], unroll=GLA_PREP_UNROLL)

    def scan_step(i, s):
        j = nc - 1 - i
        sb = s.astype(BF16)
        sc_s[i, :, 0:HALF] = sb[:, 0:HALF]
        sc_s[j, :, HALF:LANES] = sb[:, HALF:LANES]
        return dec_s[i, 0:1, :] * s + ds_s[i]

    lax.fori_loop(0, nc, scan_step, jnp.zeros((LANES, LANES), F32), unroll=GLA_UNROLL)

    nrm = nrm_ref[...]

    def mix(i):
        for cc in range(cpg):
            n = i * cpg + cc
            rows = _rows(n, c)
            vc = v_ref[rows, :]
            vvt = jnp.concatenate([vc, vc], axis=0).T
            la_s[rows, :] = _dot_nt(jnp.concatenate([qdec_s[rows, :], am_s[rows, :]], axis=1),
                                    jnp.concatenate([sc_s[n], vvt], axis=1))

    def norm_gate(i):
        rows = _rows(i, grp)
        gate = g_ref[rows, :].astype(F32)
        half = 0.5 * gate
        swish = half + half * jnp.tanh(half)
        o = (_rms(la_s[rows, :]) * nrm) * swish
        o_ref[rows, :] = o.astype(o_ref.dtype)

    _software_pipeline(n_groups, [mix, norm_gate], unroll=GLA_OUT_UNROLL)


def _gla(q, k, v, g, z, wgu, bg, nrm, batch, seq):
    t = batch * seq
    nc = seq // GLA_CHUNK
    tile = pl.BlockSpec((seq, LANES), lambda b, h: (b, h))
    pair_tile = pl.BlockSpec((seq, LANES), lambda b, h: (b, h // 2))
    tok_bf = pltpu.VMEM((seq, LANES), BF16)
    cm = jnp.asarray(_cumsum_matrix(), dtype=BF16)
    return pl.pallas_call(
        functools.partial(_gla_kernel, seq=seq),
        grid=(batch, GLA_HEADS),
        in_specs=[pair_tile, pair_tile, tile, tile,
                  pl.BlockSpec((seq, Z_W), lambda b, h: (b, 0)),
                  pl.BlockSpec((1, Z_W, LANES), lambda b, h: (h, 0, 0)),
                  pl.BlockSpec((1, 1, LANES), lambda b, h: (h, 0, 0)),
                  pl.BlockSpec((1, LANES), lambda b, h: (0, 0)),
                  _const_spec(cm.shape)],
        out_specs=tile,
        out_shape=jax.ShapeDtypeStruct((t, V_GLA_W), BF16),
        scratch_shapes=[pltpu.VMEM((seq, LANES), F32),
                        pltpu.VMEM((seq, 2 * LANES), BF16),
                        pltpu.VMEM((seq, LANES), F32),
                        tok_bf, tok_bf, tok_bf, tok_bf, tok_bf,
                        pltpu.VMEM((nc, LANES, LANES), F32),
                        pltpu.VMEM((nc, 8, LANES), F32),
                        pltpu.VMEM((nc, LANES, LANES), BF16)],
        compiler_params=pltpu.CompilerParams(
            dimension_semantics=("arbitrary", "arbitrary"), vmem_limit_bytes=VMEM_LIMIT),
        name="gla",
    )(q, k, v, g, z, wgu, bg, nrm, cm)


def _t5_buckets(rel):
    nb = REL_BUCKETS // 2
    ret = (rel > 0).astype(np.int32) * nb
    n = np.abs(rel)
    max_exact = nb // 2
    large = max_exact + (np.log(np.maximum(n, 1).astype(np.float32) / max_exact)
                         / math.log(REL_MAX_DIST / max_exact) * (nb - max_exact)).astype(np.int32)
    large = np.minimum(large, nb - 1)
    return ret + np.where(n < max_exact, n, large)


def _bucket_map():
    w = SWA_BLOCK
    cq = np.arange(w)[:, None]
    s = np.arange(3 * w)[None, :]
    rel = s - w - cq
    return np.where(np.abs(rel) <= SWA_WINDOW, _t5_buckets(rel), -1).astype(np.int32)


def _build_bias_tables(bm_ref, tab_ref, bias_s, kvh, pairs):
    w = SWA_BLOCK
    bm = bm_ref[...]
    key = lax.broadcasted_iota(jnp.int32, bm.shape, 1)
    masked = jnp.full(bm.shape, MASK_VALUE, F32)
    for pair in range(pairs):
        for hh in range(2):
            head = kvh * 2 * pairs + 2 * pair + hh
            acc = masked
            for bkt in range(REL_BUCKETS):
                acc = jnp.where(bm == bkt, tab_ref[bkt, head], acc)
            hcols = slice(hh * 3 * w, (hh + 1) * 3 * w)
            bias_s[kvh, 0, pair, :, hcols] = jnp.where(key >= w, acc, masked)
            bias_s[kvh, 1, pair, :, hcols] = acc
            bias_s[kvh, 2, pair, :, hcols] = jnp.where(key < 2 * w, acc, masked)


def _swa_kernel(q_ref, k_ref, v_ref, bm_ref, tab_ref, sink_ref, o_ref, ka_s, kb_s, va_s, vb_s,
                lg_s, p_s, st_s, bias_s, *, seq):
    w = SWA_BLOCK
    nb = seq // w
    kvh = pl.program_id(1)
    pairs = SWA_Q_HEADS // SWA_KV_HEADS // 2

    @pl.when(pl.program_id(0) == 0)
    def _():
        _build_bias_tables(bm_ref, tab_ref, bias_s, kvh, pairs)

    bias_ref = bias_s.at[kvh]
    stage_rows = 512
    lane_s = lax.broadcasted_iota(jnp.int16, (stage_rows, LANES), 1)
    lo = lane_s < HALF
    own = lo == (kvh == 0)
    for s_ref in (ka_s, kb_s, va_s, vb_s):
        zpad = jnp.zeros((w, s_ref.shape[1]), BF16)
        s_ref[0:w, :] = zpad
        s_ref[w + seq:2 * w + seq, :] = zpad

    def stage(i, carry):
        src = pl.ds(pl.multiple_of(i * stage_rows, stage_rows), stage_rows)
        dst = pl.ds(pl.multiple_of(i * stage_rows + w, w), stage_rows)
        k2 = k_ref[src, :]
        v2 = v_ref[src, :]
        kk = jnp.where(own, k2, pltpu.roll(k2, HALF, axis=1))
        vv = jnp.where(own, v2, pltpu.roll(v2, HALF, axis=1))
        zero = jnp.zeros_like(kk)
        one = jnp.ones_like(kk)
        ka_s[dst, :] = jnp.where(lo, kk, zero)
        kb_s[dst, :] = jnp.where(lo, zero, kk)
        va_s[dst, 0:LANES] = jnp.where(lo, vv, zero)
        va_s[dst, LANES:2 * LANES] = jnp.where(lo, one, zero)
        vb_s[dst, 0:LANES] = jnp.where(lo, zero, vv)
        vb_s[dst, LANES:2 * LANES] = jnp.where(lo, zero, one)
        return carry

    lax.fori_loop(0, seq // stage_rows, stage, 0)

    lo_w = lax.broadcasted_iota(jnp.int32, (w, LANES), 1) < HALF

    def logits_stage(n, variant):
        qrows = pl.ds(pl.multiple_of(n * w, w), w)
        band = pl.ds(pl.multiple_of(n * w, w), 3 * w)
        kt = jnp.concatenate([ka_s[band, :], kb_s[band, :]], axis=0)
        for pair in range(pairs):
            cols = slice(pair * LANES, (pair + 1) * LANES)
            lg_s[pair] = _dot_nt(q_ref[qrows, cols], kt) + bias_ref[variant, pair]

    def softmax_stage():
        for pair in range(pairs):
            sinks, maxes = [], []
            for hh in range(2):
                hcols = slice(hh * 3 * w, (hh + 1) * 3 * w)
                lg = lg_s[pair, :, hcols]
                sink = sink_ref[kvh * 2 * pairs + 2 * pair + hh]
                m = jnp.maximum(jnp.max(lg, axis=-1, keepdims=True), sink)
                p_s[pair, :, hcols] = jnp.exp(lg - m).astype(BF16)
                sinks.append(sink)
                maxes.append(m)
            st_s[pair] = jnp.exp(jnp.where(lo_w, sinks[0], sinks[1])
                                 - jnp.where(lo_w, maxes[0], maxes[1]))

    def output_stage(n):
        qrows = pl.ds(pl.multiple_of(n * w, w), w)
        band = pl.ds(pl.multiple_of(n * w, w), 3 * w)
        vx = jnp.concatenate([va_s[band, :], vb_s[band, :]], axis=0)
        for pair in range(pairs):
            ox = _dot(p_s[pair], vx)
            o = ox[:, :LANES] / (ox[:, LANES:] + st_s[pair])
            o_ref[qrows, pair * LANES:(pair + 1) * LANES] = o.astype(o_ref.dtype)

    logits_stage(0, 0)
    softmax_stage()
    logits_stage(1, 1)

    def steady(n, carry):
        output_stage(n - 1)
        softmax_stage()
        logits_stage(n + 1, 1)
        return carry

    lax.fori_loop(1, nb - 2, steady, 0, unroll=SWA_UNROLL)
    output_stage(nb - 3)
    softmax_stage()
    logits_stage(nb - 1, 2)
    output_stage(nb - 2)
    softmax_stage()
    output_stage(nb - 1)


def _swa(q, k, v, rel_bias, sink, batch, seq):
    t = batch * seq
    w = SWA_BLOCK
    qw = Q_SWA_W // SWA_KV_HEADS
    pairs = SWA_Q_HEADS // SWA_KV_HEADS // 2
    staged_k = pltpu.VMEM((seq + 2 * w, LANES), BF16)
    staged_v = pltpu.VMEM((seq + 2 * w, 2 * LANES), BF16)
    return pl.pallas_call(
        functools.partial(_swa_kernel, seq=seq),
        grid=(batch, SWA_KV_HEADS),
        in_specs=[pl.BlockSpec((seq, qw), lambda b, j: (b, j)),
                  pl.BlockSpec((seq, LANES), lambda b, j: (b, 0)),
                  pl.BlockSpec((seq, LANES), lambda b, j: (b, 0)),
                  _const_spec((w, 3 * w)),
                  pl.BlockSpec(memory_space=pltpu.SMEM),
                  pl.BlockSpec(memory_space=pltpu.SMEM)],
        out_specs=pl.BlockSpec((seq, qw), lambda b, j: (b, j)),
        out_shape=jax.ShapeDtypeStruct((t, Q_SWA_W), BF16),
        scratch_shapes=[staged_k, staged_k, staged_v, staged_v,
                        pltpu.VMEM((pairs, w, 6 * w), F32),
                        pltpu.VMEM((pairs, w, 6 * w), BF16),
                        pltpu.VMEM((pairs, w, LANES), F32),
                        pltpu.VMEM((SWA_KV_HEADS, 3, pairs, w, 6 * w), F32)],
        compiler_params=pltpu.CompilerParams(
            dimension_semantics=("arbitrary", "arbitrary"), vmem_limit_bytes=VMEM_LIMIT),
        name="swa",
    )(q, k, v, jnp.asarray(_bucket_map()), rel_bias.astype(F32), sink)


def _ffn_kernel(x_ref, oa_ref, os_ref, wout_ref, gpost_ref, gpre_ref, wup_ref, wdn_ref,
                gout_ref, o_ref):
    slab = FFN_TOKEN_BLOCK // FFN_SLABS
    slabs = [slice(s * slab, (s + 1) * slab) for s in range(FFN_SLABS)]
    mixes = [_dot(oa_ref[rows, :], wout_ref[0:V_GLA_W, :])
             + _dot(os_ref[rows, :], wout_ref[V_GLA_W:V_GLA_W + Q_SWA_W, :]) for rows in slabs]
    for rows, mix in zip(slabs, mixes):
        h = x_ref[rows, :] + _rms(mix) * gpost_ref[...]
        hn = (_rms(h) * gpre_ref[...]).astype(BF16)
        acc = jnp.zeros(h.shape, F32)
        for j in range(D_FF // FF_BLOCK):
            cols = slice(j * FF_BLOCK, (j + 1) * FF_BLOCK)
            zj = jnp.maximum(_dot(hn, wup_ref[:, cols]), 0.0)
            acc = acc + _dot((zj * zj).astype(BF16), wdn_ref[cols, :])
        o_ref[rows, :] = h + _rms(acc) * gout_ref[...]


def _ffn(x2, oa, os_, wout, gpost, gpre, wup, wdn, gout):
    t = x2.shape[0]
    row = lambda width: pl.BlockSpec((FFN_TOKEN_BLOCK, width), lambda i: (i, 0))
    return pl.pallas_call(
        _ffn_kernel,
        grid=(t // FFN_TOKEN_BLOCK,),
        in_specs=[row(D_MODEL), row(V_GLA_W), row(Q_SWA_W),
                  _const_spec((V_GLA_W + Q_SWA_W, D_MODEL)),
                  _const_spec((1, D_MODEL)), _const_spec((1, D_MODEL)),
                  _const_spec((D_MODEL, D_FF)), _const_spec((D_FF, D_MODEL)),
                  _const_spec((1, D_MODEL))],
        out_specs=row(D_MODEL),
        out_shape=jax.ShapeDtypeStruct((t, D_MODEL), F32),
        compiler_params=pltpu.CompilerParams(
            dimension_semantics=("arbitrary",), vmem_limit_bytes=VMEM_LIMIT),
        name="outproj_ffn",
    )(x2, oa, os_, wout, gpost, gpre, wup, wdn, gout)


def _layer(h, norm_mix_pre, w_in, w_gu_f, b_g_f, w_gu_b, b_g_b, gla_norm, swa_sink, rel_bias,
           w_out, norm_mix_post, norm_mlp_pre, w_up, w_down, norm_mlp_post):
    batch, seq, _ = h.shape
    x2 = h.reshape(batch * seq, D_MODEL)
    row = lambda g: g.reshape(1, -1).astype(F32)

    later = [w.astype(F32) for w in (w_out, w_up, w_down)]
    qa, ka, va, ga, qs, ks, vs, z, w_out_b, w_up_b, w_down_b = _inproj(
        x2, row(norm_mix_pre), w_in.astype(F32).T, later)

    wf = w_gu_f.reshape(GLA_RANK, GLA_HEADS, GLA_DK).transpose(1, 0, 2)
    wb = w_gu_b.reshape(GLA_RANK, GLA_HEADS, GLA_DK).transpose(1, 0, 2)
    zeros = jnp.zeros_like(wf)
    wgu = jnp.concatenate([jnp.concatenate([wf, zeros], axis=2),
                           jnp.concatenate([zeros, wb], axis=2)], axis=1).astype(BF16)
    bg = jnp.concatenate([b_g_f.reshape(GLA_HEADS, 1, GLA_DK),
                          b_g_b.reshape(GLA_HEADS, 1, GLA_DK)], axis=2).astype(F32)
    o_a = _gla(qa, ka, va, ga, z, wgu, bg, row(gla_norm), batch, seq)

    o_s = _swa(qs, ks, vs, rel_bias, swa_sink.astype(F32), batch, seq)

    out = _ffn(x2, o_a, o_s, w_out_b, row(norm_mix_post), row(norm_mlp_pre), w_up_b, w_down_b,
               row(norm_mlp_post))
    return out.reshape(batch, seq, D_MODEL)


def kernel(x, norm_mix_pre, w_in, w_gate_up_fwd, b_gate_fwd, w_gate_up_bwd, b_gate_bwd, gla_norm,
           swa_sink, rel_bias, w_out, norm_mix_post, norm_mlp_pre, w_up, w_down, norm_mlp_post):
    h = x
    for l in range(w_in.shape[0]):
        h = _layer(h, norm_mix_pre[l], w_in[l], w_gate_up_fwd[l], b_gate_fwd[l], w_gate_up_bwd[l],
                   b_gate_bwd[l], gla_norm[l], swa_sink[l], rel_bias, w_out[l], norm_mix_post[l],
                   norm_mlp_pre[l], w_up[l], w_down[l], norm_mlp_post[l])
    return h
```
